```python
import jax, jax.numpy as jnp
from jax import lax
import numpy as np

D_MODEL = 1024
BATCH = 16
SEQ = 2048
DEPTH = 2

CHUNK = 64
Q_BLOCK = 128
N_MIXERS = 2
EPS = 1e-6
NEG_INF = -1e30

N_MEM = 256
MEM_HEADS = 4
MEM_HEAD_DIM = 64
MEM_WIDTH = MEM_HEADS * MEM_HEAD_DIM

POOL_WINDOWS = (2, 4, 8, 16)
POOL_GROUPS = len(POOL_WINDOWS)
POOL_GROUP_DIM = 192
POOL_WIDTH = POOL_GROUPS * POOL_GROUP_DIM

MLA_HEADS = 6
QK_NOPE = 128
QK_ROPE = 64
V_HEAD = 128
Q_LORA = 384
KV_LORA = 256
MLA_WIDTH = MLA_HEADS * V_HEAD
ROPE_THETA = 10000.0

MIX_WIDTH = POOL_WIDTH + MEM_WIDTH
POOL_IN = POOL_WIDTH + MEM_WIDTH + MIX_WIDTH
MLA_IN = Q_LORA + KV_LORA + QK_ROPE + MEM_WIDTH + MIX_WIDTH

N_POOL_LAYERS = (DEPTH + N_MIXERS - 1) // N_MIXERS
N_MLA_LAYERS = DEPTH // N_MIXERS

kernel_name = "hybrid_pool_mla_memory_trunk"


def rms_norm(x, g):
    x32 = x.astype(jnp.float32)
    y = x32 * lax.rsqrt(jnp.mean(x32 * x32, axis=-1, keepdims=True) + EPS)
    return (y * g.astype(jnp.float32)).astype(x.dtype)


def rope_tables(positions):
    inv_freq = ROPE_THETA ** (-(jnp.arange(0, QK_ROPE, 2, dtype=jnp.float32) / QK_ROPE))
    ang = positions.astype(jnp.float32)[..., None] * inv_freq
    return jnp.cos(ang), jnp.sin(ang)


def apply_rope(x, cos, sin):
    x32 = x.astype(jnp.float32)
    x1, x2 = jnp.split(x32, 2, axis=-1)
    out = jnp.concatenate([x1 * cos - x2 * sin, x2 * cos + x1 * sin], axis=-1)
    return out.astype(x.dtype)


def multi_scale_pool(u, w_group, scale):
    b, s, _ = u.shape
    u32 = u.astype(jnp.float32).reshape(b, s, POOL_GROUPS, POOL_GROUP_DIM)
    csum = jnp.cumsum(u32, axis=1)
    t = jnp.arange(s)
    pooled = []
    for gi, w in enumerate(POOL_WINDOWS):
        c = csum[:, :, gi]
        prev = jnp.pad(c, ((0, 0), (w, 0), (0, 0)))[:, :s]
        cnt = jnp.minimum(t + 1, w).astype(jnp.float32)
        pooled.append((c - prev) / cnt[None, :, None])
    mixed = jnp.stack(pooled, axis=2) - u32
    y = jnp.einsum('bsgc,gcd->bsgd', mixed.astype(u.dtype), w_group)
    return y.reshape(b, s, POOL_WIDTH) * scale


def memory_attention(q, mem_k, mem_v):
    b, s = q.shape[:2]
    sc = jnp.einsum('bqhd,bkhd->bhqk', q, mem_k).astype(jnp.float32) * (MEM_HEAD_DIM ** -0.5)
    p = jax.nn.softmax(sc, axis=-1).astype(mem_v.dtype)
    o = jnp.einsum('bhqk,bkhd->bqhd', p, mem_v)
    return o.reshape(b, s, MEM_WIDTH)


def mla_attention(q_nope, q_rope, k_nope, k_rope, v):
    s = q_nope.shape[1]
    scale = (QK_NOPE + QK_ROPE) ** -0.5
    outs = []
    for blk in range(s // Q_BLOCK):
        q0, q1 = blk * Q_BLOCK, (blk + 1) * Q_BLOCK
        kv_end = q1
        sc = (jnp.einsum('bqhd,bkhd->bhqk', q_nope[:, q0:q1], k_nope[:, :kv_end])
              + jnp.einsum('bqhd,bkd->bhqk', q_rope[:, q0:q1], k_rope[:, :kv_end]))
        q_chunk = (q0 + jnp.arange(Q_BLOCK)) // CHUNK
        k_chunk = jnp.arange(kv_end) // CHUNK
        mask = k_chunk[None, :] <= q_chunk[:, None]
        sc = jnp.where(mask, sc.astype(jnp.float32) * scale, NEG_INF)
        p = jax.nn.softmax(sc, axis=-1).astype(v.dtype)
        outs.append(jnp.einsum('bhqk,bkhd->bqhd', p, v[:, :kv_end]))
    return jnp.concatenate(outs, axis=1)


def setup_inputs(seed: int = 0) -> dict:
    key = jax.random.key(seed)
    ks = jax.random.split(key, 20)
    f32 = jnp.float32

    def nrm(k, shape, fan_in):
        return jax.random.normal(k, shape, f32) * (fan_in ** -0.5)

    def gain(k, shape):
        return 1.0 + 0.02 * jax.random.normal(k, shape, f32)

    x = jax.random.normal(ks[0], (BATCH, SEQ, D_MODEL), f32)
    mem = jax.random.normal(ks[1], (BATCH, N_MEM, D_MODEL), f32)
    start = jax.random.randint(ks[2], (BATCH, 1), 0, 4096, dtype=jnp.int32)
    positions = (start + jnp.arange(SEQ, dtype=jnp.int32)[None, :]).astype(jnp.int32)
    return {
        "x": x,
        "mem": mem,
        "positions": positions,
        "norm_g": gain(ks[3], (DEPTH, D_MODEL)),
        "mem_norm_g": gain(ks[4], (D_MODEL,)),
        "w_mem_kv": nrm(ks[5], (DEPTH, D_MODEL, 2 * MEM_WIDTH), D_MODEL),
        "w_out": nrm(ks[6], (DEPTH, MIX_WIDTH, D_MODEL), MIX_WIDTH),
        "pool_w_in": nrm(ks[7], (N_POOL_LAYERS, D_MODEL, POOL_IN), D_MODEL),
        "pool_w_group": nrm(ks[8], (N_POOL_LAYERS, POOL_GROUPS, POOL_GROUP_DIM, POOL_GROUP_DIM), POOL_GROUP_DIM),
        "pool_scale": gain(ks[9], (N_POOL_LAYERS, POOL_WIDTH)),
        "mla_w_in": nrm(ks[10], (N_MLA_LAYERS, D_MODEL, MLA_IN), D_MODEL),
        "mla_q_norm_g": gain(ks[11], (N_MLA_LAYERS, Q_LORA)),
        "mla_w_uq": nrm(ks[12], (N_MLA_LAYERS, Q_LORA, MLA_HEADS * (QK_NOPE + QK_ROPE)), Q_LORA),
        "mla_kv_norm_g": gain(ks[13], (N_MLA_LAYERS, KV_LORA)),
        "mla_w_ukv": nrm(ks[14], (N_MLA_LAYERS, KV_LORA, MLA_HEADS * (QK_NOPE + V_HEAD)), KV_LORA),
        "final_norm_g": gain(ks[15], (D_MODEL,)),
    }


def reference(x, mem, positions, norm_g, mem_norm_g, w_mem_kv, w_out, pool_w_in, pool_w_group,
              pool_scale, mla_w_in, mla_q_norm_g, mla_w_uq, mla_kv_norm_g, mla_w_ukv, final_norm_g):
    b, s, _ = x.shape
    cos, sin = rope_tables(positions)
    mem_n = rms_norm(mem, mem_norm_g)
    h = x
    for i in range(DEPTH):
        j = i // N_MIXERS
        hn = rms_norm(h, norm_g[i])
        mem_kv = jnp.einsum('bnd,de->bne', mem_n, w_mem_kv[i])
        mem_k, mem_v = jnp.split(mem_kv.reshape(b, N_MEM, 2, MEM_HEADS, MEM_HEAD_DIM), 2, axis=2)
        mem_k, mem_v = mem_k[:, :, 0], mem_v[:, :, 0]
        if i % N_MIXERS == 0:
            proj = jnp.einsum('bsd,de->bse', hn, pool_w_in[j])
            u, mq, gate = jnp.split(proj, [POOL_WIDTH, POOL_WIDTH + MEM_WIDTH], axis=-1)
            tok = multi_scale_pool(u, pool_w_group[j], pool_scale[j])
        else:
            proj = jnp.einsum('bsd,de->bse', hn, mla_w_in[j])
            c_q, c_kv, k_rope, mq, gate = jnp.split(
                proj, [Q_LORA, Q_LORA + KV_LORA, Q_LORA + KV_LORA + QK_ROPE,
                       Q_LORA + KV_LORA + QK_ROPE + MEM_WIDTH], axis=-1)
            q = jnp.einsum('bsr,re->bse', rms_norm(c_q, mla_q_norm_g[j]), mla_w_uq[j])
            q = q.reshape(b, s, MLA_HEADS, QK_NOPE + QK_ROPE)
            q_nope, q_rope = q[..., :QK_NOPE], q[..., QK_NOPE:]
            q_rope = apply_rope(q_rope, cos[:, :, None, :], sin[:, :, None, :])
            kv = jnp.einsum('bsr,re->bse', rms_norm(c_kv, mla_kv_norm_g[j]), mla_w_ukv[j])
            kv = kv.reshape(b, s, MLA_HEADS, QK_NOPE + V_HEAD)
            k_nope, v = kv[..., :QK_NOPE], kv[..., QK_NOPE:]
            k_rope = apply_rope(k_rope, cos, sin)
            tok = mla_attention(q_nope, q_rope, k_nope, k_rope, v).reshape(b, s, MLA_WIDTH)
        mem_o = memory_attention(mq.reshape(b, s, MEM_HEADS, MEM_HEAD_DIM), mem_k, mem_v)
        branch = jnp.concatenate([tok.astype(h.dtype), mem_o], axis=-1) * jax.nn.silu(gate)
        h = h + jnp.einsum('bse,ed->bsd', branch, w_out[i])
    return rms_norm(h, final_norm_g)
```

```python
import functools

import jax
import jax.numpy as jnp
from jax.experimental import pallas as pl
from jax.experimental.pallas import tpu as pltpu

D_MODEL = 1024
CHUNK = 64
EPS = 1e-6
NEG_INF = -1e30

N_MEM = 256
MEM_HEADS = 4
MEM_HEAD_DIM = 64
MEM_WIDTH = MEM_HEADS * MEM_HEAD_DIM

POOL_WINDOWS = (2, 4, 8, 16)
POOL_GROUP_DIM = 192
POOL_WIDTH = len(POOL_WINDOWS) * POOL_GROUP_DIM
POOL_HALO = 16

MLA_HEADS = 6
QK_NOPE = 128
QK_ROPE = 64
QK_DIM = QK_NOPE + QK_ROPE
V_HEAD = 128
Q_LORA = 384
KV_LORA = 256
MLA_WIDTH = MLA_HEADS * V_HEAD
ROPE_THETA = 10000.0

MIX_WIDTH = POOL_WIDTH + MEM_WIDTH

LANES = 128
SEQ_TILE = 512
ATTN_TILE = 256
VMEM_LIMIT = 48 * 1024 * 1024

BF16 = jnp.bfloat16
F32 = jnp.float32


def _rms_norm(x, g):
    ms = jnp.mean(x * x, axis=-1, keepdims=True)
    return x * jax.lax.rsqrt(ms + EPS) * g


def _silu(x):
    return x / (1.0 + jnp.exp(-x))


def _dot(a, b):
    return jnp.dot(a, b, preferred_element_type=F32)


def _memory_attention(mq, kbd, vbd):
    s = _dot(mq.astype(BF16), kbd) * (MEM_HEAD_DIM ** -0.5)
    probs = []
    for h in range(MEM_HEADS):
        sh = s[:, h * N_MEM:(h + 1) * N_MEM]
        e = jnp.exp(sh - jnp.max(sh, axis=-1, keepdims=True))
        probs.append((e / jnp.sum(e, axis=-1, keepdims=True)).astype(BF16))
    return _dot(jnp.concatenate(probs, axis=-1), vbd)


def _mem_kv_kernel(mem_ref, g_ref, w_ref, kbd_ref, vbd_ref):
    mem_n = _rms_norm(mem_ref[0], g_ref[...]).astype(BF16)
    kv = _dot(mem_n, w_ref[...])
    n_layers = kv.shape[1] // (2 * MEM_WIDTH)
    row_head = jax.lax.broadcasted_iota(jnp.int32, (MEM_WIDTH, N_MEM), 0) // MEM_HEAD_DIM
    col_head = jax.lax.broadcasted_iota(jnp.int32, (N_MEM, MEM_WIDTH), 1) // MEM_HEAD_DIM
    for layer in range(n_layers):
        base = layer * 2 * MEM_WIDTH
        k_t = kv[:, base:base + MEM_WIDTH].T
        v = kv[:, base + MEM_WIDTH:base + 2 * MEM_WIDTH]
        for h in range(MEM_HEADS):
            kbd_ref[0, layer, :, h * N_MEM:(h + 1) * N_MEM] = jnp.where(
                row_head == h, k_t, 0.0).astype(BF16)
            vbd_ref[0, layer, h * N_MEM:(h + 1) * N_MEM, :] = jnp.where(
                col_head == h, v, 0.0).astype(BF16)


def _mem_kv(mem, mem_norm_g, w_cat):
    b = mem.shape[0]
    n_layers = w_cat.shape[1] // (2 * MEM_WIDTH)
    return pl.pallas_call(
        _mem_kv_kernel,
        grid=(b,),
        in_specs=[
            pl.BlockSpec((1, N_MEM, D_MODEL), lambda i: (i, 0, 0)),
            pl.BlockSpec((1, D_MODEL), lambda i: (0, 0)),
            pl.BlockSpec(w_cat.shape, lambda i: (0, 0)),
        ],
        out_specs=[
            pl.BlockSpec((1, n_layers, MEM_WIDTH, MEM_HEADS * N_MEM), lambda i: (i, 0, 0, 0)),
            pl.BlockSpec((1, n_layers, MEM_HEADS * N_MEM, MEM_WIDTH), lambda i: (i, 0, 0, 0)),
        ],
        out_shape=[
            jax.ShapeDtypeStruct((b, n_layers, MEM_WIDTH, MEM_HEADS * N_MEM), BF16),
            jax.ShapeDtypeStruct((b, n_layers, MEM_HEADS * N_MEM, MEM_WIDTH), BF16),
        ],
        compiler_params=pltpu.CompilerParams(
            dimension_semantics=("parallel",), vmem_limit_bytes=VMEM_LIMIT),
        name="mem_kv",
    )(mem, mem_norm_g, w_cat)


def _window_sums(a):
    s2 = a + pltpu.roll(a, 1, 0)
    s4 = s2 + pltpu.roll(s2, 2, 0)
    s8 = s4 + pltpu.roll(s4, 4, 0)
    s16 = s8 + pltpu.roll(s8, 8, 0)
    return {2: s2, 4: s4, 8: s8, 16: s16}


def _pool_layer_kernel(x_ref, g_ref, w_u_ref, w_mq_ref, w_gate_ref, w_grp_ref, scale_ref,
                       kbd_ref, vbd_ref, w_out_ref, o_ref, ubuf):
    j = pl.program_id(1)
    t = x_ref.shape[1]
    x = x_ref[0]
    hn = _rms_norm(x, g_ref[...]).astype(BF16)

    @pl.when(j == 0)
    def _():
        ubuf[0:POOL_HALO, :] = jnp.zeros((POOL_HALO, POOL_WIDTH), F32)

    @pl.when(j > 0)
    def _():
        ubuf[0:POOL_HALO, :] = ubuf[t:t + POOL_HALO, :]

    ubuf[POOL_HALO:, :] = _dot(hn, w_u_ref[...])

    pos = j * t + jax.lax.broadcasted_iota(jnp.int32, (t, 1), 0)
    inv_cnt = {w: 1.0 / jnp.minimum(pos + 1, w).astype(F32) for w in POOL_WINDOWS}
    lane = jax.lax.broadcasted_iota(jnp.int32, (t, LANES), 1)
    mixed = []
    for c in range(POOL_WIDTH // LANES):
        sums = _window_sums(ubuf[:, c * LANES:(c + 1) * LANES])
        lo_grp = (c * LANES) // POOL_GROUP_DIM
        hi_grp = ((c + 1) * LANES - 1) // POOL_GROUP_DIM
        w_lo, w_hi = POOL_WINDOWS[lo_grp], POOL_WINDOWS[hi_grp]
        pooled = sums[w_lo][POOL_HALO:] * inv_cnt[w_lo]
        if hi_grp != lo_grp:
            split = hi_grp * POOL_GROUP_DIM - c * LANES
            pooled = jnp.where(lane < split, pooled, sums[w_hi][POOL_HALO:] * inv_cnt[w_hi])
        mixed.append((pooled - ubuf[POOL_HALO:, c * LANES:(c + 1) * LANES]).astype(BF16))
    mixed = jnp.concatenate(mixed, axis=-1)
    tok = _dot(mixed, w_grp_ref[...]) * scale_ref[...]

    mem_o = _memory_attention(_dot(hn, w_mq_ref[...]), kbd_ref[0, 0], vbd_ref[0, 0])
    gate = _silu(_dot(hn, w_gate_ref[...]))
    y = _dot((tok * gate[:, :POOL_WIDTH]).astype(BF16), w_out_ref[:POOL_WIDTH, :])
    y += _dot((mem_o * gate[:, POOL_WIDTH:]).astype(BF16), w_out_ref[POOL_WIDTH:, :])
    o_ref[0] = x + y


def _pool_layer(x, g, w_u, w_mq, w_gate, w_grp, scale, kbd, vbd, w_out):
    b, s, _ = x.shape
    t = SEQ_TILE
    const = lambda shape: pl.BlockSpec(shape, lambda i, j: (0,) * len(shape))
    return pl.pallas_call(
        _pool_layer_kernel,
        grid=(b, s // t),
        in_specs=[
            pl.BlockSpec((1, t, D_MODEL), lambda i, j: (i, j, 0)),
            const(g.shape), const(w_u.shape), const(w_mq.shape), const(w_gate.shape),
            const(w_grp.shape), const(scale.shape),
            pl.BlockSpec((1, 1) + kbd.shape[2:], lambda i, j: (i, 0, 0, 0)),
            pl.BlockSpec((1, 1) + vbd.shape[2:], lambda i, j: (i, 0, 0, 0)),
            const(w_out.shape),
        ],
        out_specs=pl.BlockSpec((1, t, D_MODEL), lambda i, j: (i, j, 0)),
        out_shape=jax.ShapeDtypeStruct(x.shape, F32),
        scratch_shapes=[pltpu.VMEM((POOL_HALO + t, POOL_WIDTH), F32)],
        compiler_params=pltpu.CompilerParams(
            dimension_semantics=("parallel", "arbitrary"), vmem_limit_bytes=VMEM_LIMIT),
        name="pool_layer",
    )(x, g, w_u, w_mq, w_gate, w_grp, scale, kbd, vbd, w_out)


def _rope(x, cos2, sin2):
    half = QK_ROPE // 2
    rot = jnp.concatenate([x[:, half:], x[:, :half]], axis=-1)
    return x * cos2 + rot * sin2


def _mla_pre_kernel(h_ref, g_ref, cos_ref, sin_ref, w_cq_ref, w_ckv_ref, w_kr_ref, w_mq_ref,
                    w_gate_ref, qg_ref, w_uq_ref, kvg_ref, w_ukv_ref, kbd_ref, vbd_ref,
                    w_out_mem_ref, q_ref, k_ref, v_ref, sg_ref, hp_ref):
    h = h_ref[0]
    hn = _rms_norm(h, g_ref[...]).astype(BF16)
    cos2, sin2 = cos_ref[0], sin_ref[0]

    cqn = _rms_norm(_dot(hn, w_cq_ref[...]), qg_ref[...]).astype(BF16)
    q = _dot(cqn, w_uq_ref[...])
    for hd in range(MLA_HEADS):
        qh = q[:, hd * QK_DIM:(hd + 1) * QK_DIM]
        q_ref[0, hd] = jnp.concatenate(
            [qh[:, :QK_NOPE], _rope(qh[:, QK_NOPE:], cos2, sin2)], axis=-1).astype(BF16)

    ckvn = _rms_norm(_dot(hn, w_ckv_ref[...]), kvg_ref[...]).astype(BF16)
    kv = _dot(ckvn, w_ukv_ref[...])
    k_rope = _rope(_dot(hn, w_kr_ref[...]), cos2, sin2)
    for hd in range(MLA_HEADS):
        base = hd * (QK_NOPE + V_HEAD)
        k_ref[0, hd] = jnp.concatenate(
            [kv[:, base:base + QK_NOPE], k_rope], axis=-1).astype(BF16)
        v_ref[0, hd] = kv[:, base + QK_NOPE:base + QK_NOPE + V_HEAD].astype(BF16)

    mem_o = _memory_attention(_dot(hn, w_mq_ref[...]), kbd_ref[0, 0], vbd_ref[0, 0])
    gate = _silu(_dot(hn, w_gate_ref[...]))
    sg_ref[0] = gate[:, :MLA_WIDTH]
    hp_ref[0] = h + _dot((mem_o * gate[:, MLA_WIDTH:]).astype(BF16), w_out_mem_ref[...])


def _mla_pre(h, g, cos2, sin2, w_cq, w_ckv, w_kr, w_mq, w_gate, qg, w_uq, kvg, w_ukv, kbd, vbd,
             w_out_mem):
    b, s, _ = h.shape
    t = SEQ_TILE
    const = lambda shape: pl.BlockSpec(shape, lambda i, j: (0,) * len(shape))
    row = lambda width: pl.BlockSpec((1, t, width), lambda i, j: (i, j, 0))
    head = lambda width: pl.BlockSpec((1, MLA_HEADS, t, width), lambda i, j: (i, 0, j, 0))
    return pl.pallas_call(
        _mla_pre_kernel,
        grid=(b, s // t),
        in_specs=[
            row(D_MODEL), const(g.shape), row(QK_ROPE), row(QK_ROPE),
            const(w_cq.shape), const(w_ckv.shape), const(w_kr.shape), const(w_mq.shape),
            const(w_gate.shape), const(qg.shape), const(w_uq.shape), const(kvg.shape),
            const(w_ukv.shape),
            pl.BlockSpec((1, 1) + kbd.shape[2:], lambda i, j: (i, 1, 0, 0)),
            pl.BlockSpec((1, 1) + vbd.shape[2:], lambda i, j: (i, 1, 0, 0)),
            const(w_out_mem.shape),
        ],
        out_specs=[head(QK_DIM), head(QK_DIM), head(V_HEAD), row(MLA_WIDTH), row(D_MODEL)],
        out_shape=[
            jax.ShapeDtypeStruct((b, MLA_HEADS, s, QK_DIM), BF16),
            jax.ShapeDtypeStruct((b, MLA_HEADS, s, QK_DIM), BF16),
            jax.ShapeDtypeStruct((b, MLA_HEADS, s, V_HEAD), BF16),
            jax.ShapeDtypeStruct((b, s, MLA_WIDTH), F32),
            jax.ShapeDtypeStruct((b, s, D_MODEL), F32),
        ],
        compiler_params=pltpu.CompilerParams(
            dimension_semantics=("parallel", "parallel"), vmem_limit_bytes=VMEM_LIMIT),
        name="mla_pre",
    )(h, g, cos2, sin2, w_cq, w_ckv, w_kr, w_mq, w_gate, qg, w_uq, kvg, w_ukv, kbd, vbd,
      w_out_mem)


def _mla_attn_kernel(q_ref, k_ref, v_ref, sg_ref, hp_ref, w_out_ref, g_ref, o_ref,
                     m_sc, l_sc, acc_sc, tok_sc):
    i = pl.program_id(1)
    tq = q_ref.shape[2]
    scale = QK_DIM ** -0.5
    q_chunk = jax.lax.broadcasted_iota(jnp.int32, (tq, tq), 0) // CHUNK
    k_chunk = jax.lax.broadcasted_iota(jnp.int32, (tq, tq), 1) // CHUNK
    diag_mask = k_chunk <= q_chunk

    for hd in range(MLA_HEADS):
        q = q_ref[0, hd]
        m_sc[...] = jnp.full(m_sc.shape, NEG_INF, F32)
        l_sc[...] = jnp.zeros(l_sc.shape, F32)
        acc_sc[...] = jnp.zeros(acc_sc.shape, F32)

        def step(jt, mask, q=q, hd=hd):
            start = pl.multiple_of(jt * tq, tq)
            k = k_ref[0, hd, pl.ds(start, tq), :]
            v = v_ref[0, hd, pl.ds(start, tq), :]
            s = jax.lax.dot_general(q, k, (((1,), (1,)), ((), ())),
                                    preferred_element_type=F32) * scale
            if mask is not None:
                s = jnp.where(mask, s, NEG_INF)
            m_old = m_sc[...]
            m_new = jnp.maximum(m_old, jnp.max(s, axis=-1, keepdims=True))
            alpha = jnp.exp(m_old - m_new)
            p = jnp.exp(s - m_new)
            l_sc[...] = alpha * l_sc[...] + jnp.sum(p, axis=-1, keepdims=True)
            acc_sc[...] = alpha * acc_sc[...] + _dot(p.astype(BF16), v)
            m_sc[...] = m_new

        def body(jt, carry):
            step(jt, None)
            return carry

        jax.lax.fori_loop(0, i, body, 0)
        step(i, diag_mask)
        tok_sc[:, hd * V_HEAD:(hd + 1) * V_HEAD] = acc_sc[...] / l_sc[...]

    branch = (tok_sc[...] * sg_ref[0]).astype(BF16)
    out = hp_ref[0] + _dot(branch, w_out_ref[...])
    o_ref[0] = _rms_norm(out, g_ref[...])


def _mla_attn(q, k, v, sg, hp, w_out_tok, final_g):
    b, _, s, _ = q.shape
    t = ATTN_TILE
    const = lambda shape: pl.BlockSpec(shape, lambda i, j: (0,) * len(shape))
    row = lambda width: pl.BlockSpec((1, t, width), lambda i, j: (i, j, 0))
    return pl.pallas_call(
        _mla_attn_kernel,
        grid=(b, s // t),
        in_specs=[
            pl.BlockSpec((1, MLA_HEADS, t, QK_DIM), lambda i, j: (i, 0, j, 0)),
            pl.BlockSpec((1, MLA_HEADS, s, QK_DIM), lambda i, j: (i, 0, 0, 0)),
            pl.BlockSpec((1, MLA_HEADS, s, V_HEAD), lambda i, j: (i, 0, 0, 0)),
            row(MLA_WIDTH), row(D_MODEL), const(w_out_tok.shape), const(final_g.shape),
        ],
        out_specs=row(D_MODEL),
        out_shape=jax.ShapeDtypeStruct((b, s, D_MODEL), F32),
        scratch_shapes=[
            pltpu.VMEM((t, 1), F32), pltpu.VMEM((t, 1), F32),
            pltpu.VMEM((t, V_HEAD), F32), pltpu.VMEM((t, MLA_WIDTH), F32),
        ],
        compiler_params=pltpu.CompilerParams(
            dimension_semantics=("parallel", "arbitrary"), vmem_limit_bytes=VMEM_LIMIT),
        name="mla_attn",
    )(q, k, v, sg, hp, w_out_tok, final_g)


def _block_diag(w_group):
    g, c, d = w_group.shape
    out = jnp.zeros((g * c, g * d), w_group.dtype)
    for i in range(g):
        out = out.at[i * c:(i + 1) * c, i * d:(i + 1) * d].set(w_group[i])
    return out


def kernel(x, mem, positions, norm_g, mem_norm_g, w_mem_kv, w_out, pool_w_in, pool_w_group,
           pool_scale, mla_w_in, mla_q_norm_g, mla_w_uq, mla_kv_norm_g, mla_w_ukv, final_norm_g):
    depth = w_mem_kv.shape[0]
    assert depth == 2 and pool_w_in.shape[0] == 1 and mla_w_in.shape[0] == 1
    bf = lambda a: a.astype(BF16)
    row = lambda a: a.reshape(1, -1)

    inv_freq = ROPE_THETA ** (-(jnp.arange(0, QK_ROPE, 2, dtype=F32) / QK_ROPE))
    ang = positions.astype(F32)[..., None] * inv_freq
    cos, sin = jnp.cos(ang), jnp.sin(ang)
    cos2 = jnp.concatenate([cos, cos], axis=-1)
    sin2 = jnp.concatenate([-sin, sin], axis=-1)

    w_mem_cat = bf(jnp.concatenate([w_mem_kv[i] for i in range(depth)], axis=1))
    kbd, vbd = _mem_kv(mem, row(mem_norm_g), w_mem_cat)

    pw = pool_w_in[0]
    h1 = _pool_layer(
        x, row(norm_g[0]),
        bf(pw[:, :POOL_WIDTH]), bf(pw[:, POOL_WIDTH:POOL_WIDTH + MEM_WIDTH]),
        bf(pw[:, POOL_WIDTH + MEM_WIDTH:]), bf(_block_diag(pool_w_group[0])),
        row(pool_scale[0]), kbd, vbd, bf(w_out[0]))

    mw = mla_w_in[0]
    o_ckv = Q_LORA
    o_kr = o_ckv + KV_LORA
    o_mq = o_kr + QK_ROPE
    o_gate = o_mq + MEM_WIDTH
    q, k, v, sg, hp = _mla_pre(
        h1, row(norm_g[1]), cos2, sin2,
        bf(mw[:, :o_ckv]), bf(mw[:, o_ckv:o_kr]), bf(mw[:, o_kr:o_mq]), bf(mw[:, o_mq:o_gate]),
        bf(mw[:, o_gate:]), row(mla_q_norm_g[0]), bf(mla_w_uq[0]), row(mla_kv_norm_g[0]),
        bf(mla_w_ukv[0]), kbd, vbd, bf(w_out[1, MLA_WIDTH:]))

    return _mla_attn(q, k, v, sg, hp, bf(w_out[1, :MLA_WIDTH]), row(final_norm_g))
```

```python
import math

import jax
import jax.numpy as jnp
from jax.experimental import pallas as pl
from jax.experimental.pallas import tpu as pltpu

D_MODEL = 1024
CHUNK = 64
EPS = 1e-6
NEG_INF = -1e30

N_MEM = 256
MEM_HEADS = 4
MEM_HEAD_DIM = 64
MEM_WIDTH = MEM_HEADS * MEM_HEAD_DIM

POOL_WINDOWS = (2, 4, 8, 16)
POOL_GROUP_DIM = 192
POOL_WIDTH = len(POOL_WINDOWS) * POOL_GROUP_DIM
POOL_HALO = 16

MLA_HEADS = 6
QK_NOPE = 128
QK_ROPE = 64
QK_DIM = QK_NOPE + QK_ROPE
V_HEAD = 128
Q_LORA = 384
KV_LORA = 256
MLA_WIDTH = MLA_HEADS * V_HEAD
ROPE_THETA = 10000.0
Q_SCALE = QK_DIM ** -0.5 * math.log2(math.e)

MIX_WIDTH = POOL_WIDTH + MEM_WIDTH

LANES = 128
SUBLANES = 8
SEQ_TILE = 512
ATTN_TILE = 256
VMEM_LIMIT = 48 * 1024 * 1024

BF16 = jnp.bfloat16
F32 = jnp.float32


def _rms_norm(x, g):
    ms = jnp.mean(x * x, axis=-1, keepdims=True)
    return x * jax.lax.rsqrt(ms + EPS) * g


def _silu(x):
    return x / (1.0 + jnp.exp(-x))


def _dot(a, b):
    return jnp.dot(a, b, preferred_element_type=F32)


def _dot_nt(a, b):
    return jax.lax.dot_general(a, b, (((1,), (1,)), ((), ())), preferred_element_type=F32)


def _memory_attention(mq, kbd, vbd):
    s = _dot(mq.astype(BF16), kbd) * (MEM_HEAD_DIM ** -0.5)
    probs = []
    for h in range(MEM_HEADS):
        sh = s[:, h * N_MEM:(h + 1) * N_MEM]
        e = jnp.exp(sh - jnp.max(sh, axis=-1, keepdims=True))
        probs.append((e / jnp.sum(e, axis=-1, keepdims=True)).astype(BF16))
    return _dot(jnp.concatenate(probs, axis=-1), vbd)


def _mem_kv_kernel(mem_ref, g_ref, w_ref, kbd_ref, vbd_ref):
    mem_n = _rms_norm(mem_ref[0], g_ref[...]).astype(BF16)
    kv = _dot(mem_n, w_ref[...])
    n_layers = kv.shape[1] // (2 * MEM_WIDTH)
    row_head = jax.lax.broadcasted_iota(jnp.int32, (MEM_WIDTH, N_MEM), 0) // MEM_HEAD_DIM
    col_head = jax.lax.broadcasted_iota(jnp.int32, (N_MEM, MEM_WIDTH), 1) // MEM_HEAD_DIM
    for layer in range(n_layers):
        base = layer * 2 * MEM_WIDTH
        k_t = kv[:, base:base + MEM_WIDTH].T
        v = kv[:, base + MEM_WIDTH:base + 2 * MEM_WIDTH]
        for h in range(MEM_HEADS):
            kbd_ref[0, layer, :, h * N_MEM:(h + 1) * N_MEM] = jnp.where(
                row_head == h, k_t, 0.0).astype(BF16)
            vbd_ref[0, layer, h * N_MEM:(h + 1) * N_MEM, :] = jnp.where(
                col_head == h, v, 0.0).astype(BF16)


def _mem_kv(mem, mem_norm_g, w_cat):
    b = mem.shape[0]
    n_layers = w_cat.shape[1] // (2 * MEM_WIDTH)
    return pl.pallas_call(
        _mem_kv_kernel,
        grid=(b,),
        in_specs=[
            pl.BlockSpec((1, N_MEM, D_MODEL), lambda i: (i, 0, 0)),
            pl.BlockSpec((1, D_MODEL), lambda i: (0, 0)),
            pl.BlockSpec(w_cat.shape, lambda i: (0, 0)),
        ],
        out_specs=[
            pl.BlockSpec((1, n_layers, MEM_WIDTH, MEM_HEADS * N_MEM), lambda i: (i, 0, 0, 0)),
            pl.BlockSpec((1, n_layers, MEM_HEADS * N_MEM, MEM_WIDTH), lambda i: (i, 0, 0, 0)),
        ],
        out_shape=[
            jax.ShapeDtypeStruct((b, n_layers, MEM_WIDTH, MEM_HEADS * N_MEM), BF16),
            jax.ShapeDtypeStruct((b, n_layers, MEM_HEADS * N_MEM, MEM_WIDTH), BF16),
        ],
        compiler_params=pltpu.CompilerParams(
            dimension_semantics=("parallel",), vmem_limit_bytes=VMEM_LIMIT),
        name="mem_kv",
    )(mem, mem_norm_g, w_cat)


def _window_sums(a):
    s2 = a + pltpu.roll(a, 1, 0)
    s4 = s2 + pltpu.roll(s2, 2, 0)
    s8 = s4 + pltpu.roll(s4, 4, 0)
    s16 = s8 + pltpu.roll(s8, 8, 0)
    return {2: s2, 4: s4, 8: s8, 16: s16}


def _pool_layer_kernel(x_ref, g_ref, w_u_ref, w_mq_ref, w_gate_ref, w_grp_ref, scale_ref,
                       kbd_ref, vbd_ref, w_out_ref, o_ref, ubuf):
    j = pl.program_id(1)
    t = x_ref.shape[1]
    x = x_ref[0]
    hn = _rms_norm(x, g_ref[...]).astype(BF16)

    @pl.when(j == 0)
    def _():
        ubuf[0:POOL_HALO, :] = jnp.zeros((POOL_HALO, POOL_WIDTH), F32)

    @pl.when(j > 0)
    def _():
        ubuf[0:POOL_HALO, :] = ubuf[t:t + POOL_HALO, :]

    ubuf[POOL_HALO:, :] = _dot(hn, w_u_ref[...])

    pos = j * t + jax.lax.broadcasted_iota(jnp.int32, (t, 1), 0)
    inv_cnt = {w: 1.0 / jnp.minimum(pos + 1, w).astype(F32) for w in POOL_WINDOWS}
    lane = jax.lax.broadcasted_iota(jnp.int32, (t, LANES), 1)
    mixed = []
    for c in range(POOL_WIDTH // LANES):
        sums = _window_sums(ubuf[:, c * LANES:(c + 1) * LANES])
        lo_grp = (c * LANES) // POOL_GROUP_DIM
        hi_grp = ((c + 1) * LANES - 1) // POOL_GROUP_DIM
        w_lo, w_hi = POOL_WINDOWS[lo_grp], POOL_WINDOWS[hi_grp]
        pooled = sums[w_lo][POOL_HALO:] * inv_cnt[w_lo]
        if hi_grp != lo_grp:
            split = hi_grp * POOL_GROUP_DIM - c * LANES
            pooled = jnp.where(lane < split, pooled, sums[w_hi][POOL_HALO:] * inv_cnt[w_hi])
        mixed.append((pooled - ubuf[POOL_HALO:, c * LANES:(c + 1) * LANES]).astype(BF16))
    mixed = jnp.concatenate(mixed, axis=-1)
    tok = _dot(mixed, w_grp_ref[...]) * scale_ref[...]

    mem_o = _memory_attention(_dot(hn, w_mq_ref[...]), kbd_ref[0, 0], vbd_ref[0, 0])
    gate = _silu(_dot(hn, w_gate_ref[...]))
    y = _dot((tok * gate[:, :POOL_WIDTH]).astype(BF16), w_out_ref[:POOL_WIDTH, :])
    y += _dot((mem_o * gate[:, POOL_WIDTH:]).astype(BF16), w_out_ref[POOL_WIDTH:, :])
    o_ref[0] = x + y


def _pool_layer(x, g, w_u, w_mq, w_gate, w_grp, scale, kbd, vbd, w_out):
    b, s, _ = x.shape
    t = SEQ_TILE
    const = lambda shape: pl.BlockSpec(shape, lambda i, j: (0,) * len(shape))
    return pl.pallas_call(
        _pool_layer_kernel,
        grid=(b, s // t),
        in_specs=[
            pl.BlockSpec((1, t, D_MODEL), lambda i, j: (i, j, 0)),
            const(g.shape), const(w_u.shape), const(w_mq.shape), const(w_gate.shape),
            const(w_grp.shape), const(scale.shape),
            pl.BlockSpec((1, 1) + kbd.shape[2:], lambda i, j: (i, 0, 0, 0)),
            pl.BlockSpec((1, 1) + vbd.shape[2:], lambda i, j: (i, 0, 0, 0)),
            const(w_out.shape),
        ],
        out_specs=pl.BlockSpec((1, t, D_MODEL), lambda i, j: (i, j, 0)),
        out_shape=jax.ShapeDtypeStruct(x.shape, F32),
        scratch_shapes=[pltpu.VMEM((POOL_HALO + t, POOL_WIDTH), F32)],
        compiler_params=pltpu.CompilerParams(
            dimension_semantics=("parallel", "arbitrary"), vmem_limit_bytes=VMEM_LIMIT),
        name="pool_layer",
    )(x, g, w_u, w_mq, w_gate, w_grp, scale, kbd, vbd, w_out)


def _mla_pre_kernel(h_ref, g_ref, cos_ref, sin_ref, cos_t_ref, sin_t_ref, w_cq_ref, w_ckv_ref,
                    w_kr_ref, w_mq_ref, w_gate_ref, qg_ref, w_uq_t_ref, kvg_ref, w_uk_ref,
                    w_uv_t_ref, kbd_ref, vbd_ref, w_out_mem_ref,
                    q_ref, k_ref, v_ref, sg_ref, hp_ref):
    h = h_ref[0]
    t = h.shape[0]
    hn = _rms_norm(h, g_ref[...]).astype(BF16)
    half = QK_ROPE // 2

    cqn = _rms_norm(_dot(hn, w_cq_ref[...]), qg_ref[...]).astype(BF16)
    q_t = _dot_nt(w_uq_t_ref[...], cqn) * Q_SCALE
    cos_t, sin_t = cos_t_ref[0], sin_t_ref[0]
    for hd in range(MLA_HEADS):
        base = hd * QK_DIM
        x1 = q_t[base + QK_NOPE:base + QK_NOPE + half]
        x2 = q_t[base + QK_NOPE + half:base + QK_DIM]
        q_ref[0, hd, 0:QK_NOPE, :] = q_t[base:base + QK_NOPE].astype(BF16)
        q_ref[0, hd, QK_NOPE:QK_NOPE + half, :] = (x1 * cos_t - x2 * sin_t).astype(BF16)
        q_ref[0, hd, QK_NOPE + half:QK_DIM, :] = (x2 * cos_t + x1 * sin_t).astype(BF16)

    ckvn = _rms_norm(_dot(hn, w_ckv_ref[...]), kvg_ref[...]).astype(BF16)
    k_nope = _dot(ckvn, w_uk_ref[...])
    v_t = _dot_nt(w_uv_t_ref[...], ckvn)
    kr = _dot(hn, w_kr_ref[...])
    kr_rot = jnp.concatenate([kr[:, half:], kr[:, :half]], axis=-1)
    k_rope = kr * cos_ref[0] + kr_rot * sin_ref[0]
    n_kt = t // ATTN_TILE
    for hd in range(MLA_HEADS):
        k_ref[0, hd] = jnp.concatenate(
            [k_nope[:, hd * QK_NOPE:(hd + 1) * QK_NOPE], k_rope], axis=-1).astype(BF16)
        for kt in range(n_kt):
            v_ref[0, hd, kt] = v_t[hd * V_HEAD:(hd + 1) * V_HEAD,
                                   kt * ATTN_TILE:(kt + 1) * ATTN_TILE].astype(BF16)

    mem_o = _memory_attention(_dot(hn, w_mq_ref[...]), kbd_ref[0, 0], vbd_ref[0, 0])
    gate = _silu(_dot(hn, w_gate_ref[...]))
    sg_ref[0] = gate[:, :MLA_WIDTH]
    hp_ref[0] = h + _dot((mem_o * gate[:, MLA_WIDTH:]).astype(BF16), w_out_mem_ref[...])


def _mla_pre(h, g, cos2, sin2, cos_t, sin_t, w_cq, w_ckv, w_kr, w_mq, w_gate, qg, w_uq_t, kvg,
             w_uk, w_uv_t, kbd, vbd, w_out_mem):
    b, s, _ = h.shape
    t = SEQ_TILE
    n_kt = t // ATTN_TILE
    const = lambda shape: pl.BlockSpec(shape, lambda i, j: (0,) * len(shape))
    row = lambda width: pl.BlockSpec((1, t, width), lambda i, j: (i, j, 0))
    col = lambda height: pl.BlockSpec((1, height, t), lambda i, j: (i, 0, j))
    return pl.pallas_call(
        _mla_pre_kernel,
        grid=(b, s // t),
        in_specs=[
            row(D_MODEL), const(g.shape), row(QK_ROPE), row(QK_ROPE),
            col(QK_ROPE // 2), col(QK_ROPE // 2),
            const(w_cq.shape), const(w_ckv.shape), const(w_kr.shape), const(w_mq.shape),
            const(w_gate.shape), const(qg.shape), const(w_uq_t.shape), const(kvg.shape),
            const(w_uk.shape), const(w_uv_t.shape),
            pl.BlockSpec((1, 1) + kbd.shape[2:], lambda i, j: (i, 1, 0, 0)),
            pl.BlockSpec((1, 1) + vbd.shape[2:], lambda i, j: (i, 1, 0, 0)),
            const(w_out_mem.shape),
        ],
        out_specs=[
            pl.BlockSpec((1, MLA_HEADS, QK_DIM, t), lambda i, j: (i, 0, 0, j)),
            pl.BlockSpec((1, MLA_HEADS, t, QK_DIM), lambda i, j: (i, 0, j, 0)),
            pl.BlockSpec((1, MLA_HEADS, n_kt, V_HEAD, ATTN_TILE), lambda i, j: (i, 0, j, 0, 0)),
            row(MLA_WIDTH), row(D_MODEL),
        ],
        out_shape=[
            jax.ShapeDtypeStruct((b, MLA_HEADS, QK_DIM, s), BF16),
            jax.ShapeDtypeStruct((b, MLA_HEADS, s, QK_DIM), BF16),
            jax.ShapeDtypeStruct((b, MLA_HEADS, s // ATTN_TILE, V_HEAD, ATTN_TILE), BF16),
            jax.ShapeDtypeStruct((b, s, MLA_WIDTH), F32),
            jax.ShapeDtypeStruct((b, s, D_MODEL), F32),
        ],
        compiler_params=pltpu.CompilerParams(
            dimension_semantics=("parallel", "parallel"), vmem_limit_bytes=VMEM_LIMIT),
        name="mla_pre",
    )(h, g, cos2, sin2, cos_t, sin_t, w_cq, w_ckv, w_kr, w_mq, w_gate, qg, w_uq_t, kvg, w_uk,
      w_uv_t, kbd, vbd, w_out_mem)


def _sublane_allreduce(x, op):
    for shift in (4, 2, 1):
        x = op(x, pltpu.roll(x, shift, 0))
    return x


def _mla_attn_kernel(q_ref, k_ref, v_ref, sg_ref, hp_ref, w_out_ref, g_ref, o_ref,
                     acc_sc, tok_sc):
    i = pl.program_id(1)
    tq = q_ref.shape[3]
    tk = v_ref.shape[4]
    groups = tk // SUBLANES
    k_chunk = jax.lax.broadcasted_iota(jnp.int32, (tk, tq), 0) // CHUNK
    q_chunk = jax.lax.broadcasted_iota(jnp.int32, (tk, tq), 1) // CHUNK
    diag_mask = k_chunk <= q_chunk

    acc_sc[...] = jnp.zeros(acc_sc.shape, F32)
    heads = range(MLA_HEADS)

    def step(jt, carry, mask):
        m_old, l_old = carry
        start = pl.multiple_of(jt * tk, tk)
        s = [_dot(k_ref[0, hd, pl.ds(start, tk), :], q_ref[0, hd]) for hd in heads]
        m_new, l_new, alpha, p = [], [], [], []
        for hd in heads:
            sh = s[hd] if mask is None else jnp.where(mask, s[hd], NEG_INF)
            s3 = sh.reshape(groups, SUBLANES, tq)
            m = jnp.maximum(m_old[hd], _sublane_allreduce(jnp.max(s3, axis=0), jnp.maximum))
            a = jnp.exp2(m_old[hd] - m)
            p3 = jnp.exp2(s3 - m[None])
            m_new.append(m)
            alpha.append(a)
            l_new.append(a * l_old[hd] + jnp.sum(p3, axis=0))
            p.append(p3.reshape(tk, tq).astype(BF16))
        pv = [_dot(v_ref[0, hd, jt], p[hd]) for hd in heads]
        for hd in heads:
            acc = acc_sc[hd].reshape(V_HEAD // SUBLANES, SUBLANES, tq)
            acc_sc[hd] = (alpha[hd][None] * acc).reshape(V_HEAD, tq) + pv[hd]
        return tuple(m_new), tuple(l_new)

    init = (tuple(jnp.full((SUBLANES, tq), NEG_INF, F32) for _ in heads),
            tuple(jnp.zeros((SUBLANES, tq), F32) for _ in heads))
    carry = jax.lax.fori_loop(0, i, lambda jt, c: step(jt, c, None), init)
    _, l_fin = step(i, carry, diag_mask)

    for hd in heads:
        l_tot = _sublane_allreduce(l_fin[hd], jnp.add)
        o_t = acc_sc[hd].reshape(V_HEAD // SUBLANES, SUBLANES, tq) / l_tot[None]
        tok_sc[:, hd * V_HEAD:(hd + 1) * V_HEAD] = o_t.reshape(V_HEAD, tq).T

    branch = (tok_sc[...] * sg_ref[0]).astype(BF16)
    out = hp_ref[0] + _dot(branch, w_out_ref[...])
    o_ref[0] = _rms_norm(out, g_ref[...])


def _mla_attn(q_t, k, v_t, sg, hp, w_out_tok, final_g):
    b, _, s, _ = k.shape
    t = ATTN_TILE
    const = lambda shape: pl.BlockSpec(shape, lambda i, j: (0,) * len(shape))
    row = lambda width: pl.BlockSpec((1, t, width), lambda i, j: (i, j, 0))
    return pl.pallas_call(
        _mla_attn_kernel,
        grid=(b, s // t),
        in_specs=[
            pl.BlockSpec((1, MLA_HEADS, QK_DIM, t), lambda i, j: (i, 0, 0, j)),
            pl.BlockSpec((1, MLA_HEADS, s, QK_DIM), lambda i, j: (i, 0, 0, 0)),
            pl.BlockSpec((1,) + v_t.shape[1:], lambda i, j: (i, 0, 0, 0, 0)),
            row(MLA_WIDTH), row(D_MODEL), const(w_out_tok.shape), const(final_g.shape),
        ],
        out_specs=row(D_MODEL),
        out_shape=jax.ShapeDtypeStruct((b, s, D_MODEL), F32),
        scratch_shapes=[
            pltpu.VMEM((MLA_HEADS, V_HEAD, t), F32),
            pltpu.VMEM((t, MLA_WIDTH), F32),
        ],
        compiler_params=pltpu.CompilerParams(
            dimension_semantics=("parallel", "arbitrary"), vmem_limit_bytes=VMEM_LIMIT),
        name="mla_attn",
    )(q_t, k, v_t, sg, hp, w_out_tok, final_g)


def _block_diag(w_group):
    g, c, d = w_group.shape
    out = jnp.zeros((g * c, g * d), w_group.dtype)
    for i in range(g):
        out = out.at[i * c:(i + 1) * c, i * d:(i + 1) * d].set(w_group[i])
    return out


def kernel(x, mem, positions, norm_g, mem_norm_g, w_mem_kv, w_out, pool_w_in, pool_w_group,
           pool_scale, mla_w_in, mla_q_norm_g, mla_w_uq, mla_kv_norm_g, mla_w_ukv, final_norm_g):
    depth = w_mem_kv.shape[0]
    assert depth == 2 and pool_w_in.shape[0] == 1 and mla_w_in.shape[0] == 1
    bf = lambda a: a.astype(BF16)
    row = lambda a: a.reshape(1, -1)

    inv_freq = ROPE_THETA ** (-(jnp.arange(0, QK_ROPE, 2, dtype=F32) / QK_ROPE))
    ang = positions.astype(F32)[..., None] * inv_freq
    cos, sin = jnp.cos(ang), jnp.sin(ang)
    cos2 = jnp.concatenate([cos, cos], axis=-1)
    sin2 = jnp.concatenate([-sin, sin], axis=-1)
    cos_t, sin_t = jnp.swapaxes(cos, 1, 2), jnp.swapaxes(sin, 1, 2)

    w_mem_cat = bf(jnp.concatenate([w_mem_kv[i] for i in range(depth)], axis=1))
    kbd, vbd = _mem_kv(mem, row(mem_norm_g), w_mem_cat)

    pw = pool_w_in[0]
    h1 = _pool_layer(
        x, row(norm_g[0]),
        bf(pw[:, :POOL_WIDTH]), bf(pw[:, POOL_WIDTH:POOL_WIDTH + MEM_WIDTH]),
        bf(pw[:, POOL_WIDTH + MEM_WIDTH:]), bf(_block_diag(pool_w_group[0])),
        row(pool_scale[0]), kbd, vbd, bf(w_out[0]))

    mw = mla_w_in[0]
    o_ckv = Q_LORA
    o_kr = o_ckv + KV_LORA
    o_mq = o_kr + QK_ROPE
    o_gate = o_mq + MEM_WIDTH
    w_ukv = mla_w_ukv[0].reshape(KV_LORA, MLA_HEADS, QK_NOPE + V_HEAD)
    w_uk = w_ukv[:, :, :QK_NOPE].reshape(KV_LORA, MLA_HEADS * QK_NOPE)
    w_uv_t = w_ukv[:, :, QK_NOPE:].reshape(KV_LORA, MLA_WIDTH).T
    q_t, k, v_t, sg, hp = _mla_pre(
        h1, row(norm_g[1]), cos2, sin2, cos_t, sin_t,
        bf(mw[:, :o_ckv]), bf(mw[:, o_ckv:o_kr]), bf(mw[:, o_kr:o_mq]), bf(mw[:, o_mq:o_gate]),
        bf(mw[:, o_gate:]), row(mla_q_norm_g[0]), bf(mla_w_uq[0].T), row(mla_kv_norm_g[0]),
        bf(w_uk), bf(w_uv_t), kbd, vbd, bf(w_out[1, MLA_WIDTH:]))

    return _mla_attn(q_t, k, v_t, sg, hp, bf(w_out[1, :MLA_WIDTH]), row(final_norm_g))
```

```python
import math

import jax
import jax.numpy as jnp
from jax.experimental import pallas as pl
from jax.experimental.pallas import tpu as pltpu

D_MODEL = 1024
CHUNK = 64
EPS = 1e-6
NEG_INF = -1e30

N_MEM = 256
MEM_HEADS = 4
MEM_HEAD_DIM = 64
MEM_WIDTH = MEM_HEADS * MEM_HEAD_DIM

POOL_WINDOWS = (2, 4, 8, 16)
POOL_GROUP_DIM = 192
POOL_WIDTH = len(POOL_WINDOWS) * POOL_GROUP_DIM
POOL_HALO = 16

MLA_HEADS = 6
QK_NOPE = 128
QK_ROPE = 64
QK_DIM = QK_NOPE + QK_ROPE
V_HEAD = 128
Q_LORA = 384
KV_LORA = 256
MLA_WIDTH = MLA_HEADS * V_HEAD
ROPE_THETA = 10000.0
Q_SCALE = QK_DIM ** -0.5 * math.log2(math.e)

MIX_WIDTH = POOL_WIDTH + MEM_WIDTH

LANES = 128
SUBLANES = 8
SEQ_TILE = 512
ROW_CHUNK = 256
ATTN_TILE = 512
VMEM_LIMIT = 56 * 1024 * 1024

BF16 = jnp.bfloat16
F32 = jnp.float32


def _rms_norm(x, g):
    ms = jnp.mean(x * x, axis=-1, keepdims=True)
    return x * jax.lax.rsqrt(ms + EPS) * g


def _silu(x):
    return x / (1.0 + jnp.exp(-x))


def _dot(a, b):
    return jnp.dot(a, b, preferred_element_type=F32)


def _dot_nt(a, b):
    return jax.lax.dot_general(a, b, (((1,), (1,)), ((), ())), preferred_element_type=F32)


def _memory_attention(mq, kbd, vbd):
    s = _dot(mq.astype(BF16), kbd) * (MEM_HEAD_DIM ** -0.5)
    probs = []
    for h in range(MEM_HEADS):
        sh = s[:, h * N_MEM:(h + 1) * N_MEM]
        e = jnp.exp(sh - jnp.max(sh, axis=-1, keepdims=True))
        probs.append((e / jnp.sum(e, axis=-1, keepdims=True)).astype(BF16))
    return _dot(jnp.concatenate(probs, axis=-1), vbd)


def _mem_kv_kernel(mem_ref, g_ref, w_ref, kbd_ref, vbd_ref):
    mem_n = _rms_norm(mem_ref[0], g_ref[...]).astype(BF16)
    kv = _dot(mem_n, w_ref[...])
    n_layers = kv.shape[1] // (2 * MEM_WIDTH)
    row_head = jax.lax.broadcasted_iota(jnp.int32, (MEM_WIDTH, N_MEM), 0) // MEM_HEAD_DIM
    col_head = jax.lax.broadcasted_iota(jnp.int32, (N_MEM, MEM_WIDTH), 1) // MEM_HEAD_DIM
    for layer in range(n_layers):
        base = layer * 2 * MEM_WIDTH
        k_t = kv[:, base:base + MEM_WIDTH].T
        v = kv[:, base + MEM_WIDTH:base + 2 * MEM_WIDTH]
        for h in range(MEM_HEADS):
            kbd_ref[0, layer, :, h * N_MEM:(h + 1) * N_MEM] = jnp.where(
                row_head == h, k_t, 0.0).astype(BF16)
            vbd_ref[0, layer, h * N_MEM:(h + 1) * N_MEM, :] = jnp.where(
                col_head == h, v, 0.0).astype(BF16)


def _mem_kv(mem, mem_norm_g, w_cat):
    b = mem.shape[0]
    n_layers = w_cat.shape[1] // (2 * MEM_WIDTH)
    return pl.pallas_call(
        _mem_kv_kernel,
        grid=(b,),
        in_specs=[
            pl.BlockSpec((1, N_MEM, D_MODEL), lambda i: (i, 0, 0)),
            pl.BlockSpec((1, D_MODEL), lambda i: (0, 0)),
            pl.BlockSpec(w_cat.shape, lambda i: (0, 0)),
        ],
        out_specs=[
            pl.BlockSpec((1, n_layers, MEM_WIDTH, MEM_HEADS * N_MEM), lambda i: (i, 0, 0, 0)),
            pl.BlockSpec((1, n_layers, MEM_HEADS * N_MEM, MEM_WIDTH), lambda i: (i, 0, 0, 0)),
        ],
        out_shape=[
            jax.ShapeDtypeStruct((b, n_layers, MEM_WIDTH, MEM_HEADS * N_MEM), BF16),
            jax.ShapeDtypeStruct((b, n_layers, MEM_HEADS * N_MEM, MEM_WIDTH), BF16),
        ],
        compiler_params=pltpu.CompilerParams(
            dimension_semantics=("parallel",), vmem_limit_bytes=VMEM_LIMIT),
        name="mem_kv",
    )(mem, mem_norm_g, w_cat)


def _window_sums(a):
    s2 = a + pltpu.roll(a, 1, 0)
    s4 = s2 + pltpu.roll(s2, 2, 0)
    s8 = s4 + pltpu.roll(s4, 4, 0)
    s16 = s8 + pltpu.roll(s8, 8, 0)
    return {2: s2, 4: s4, 8: s8, 16: s16}


def _pooled_mix(ubuf, row0, n, pos0):
    pos = pos0 + jax.lax.broadcasted_iota(jnp.int32, (n, 1), 0)
    inv_cnt = {w: 1.0 / jnp.minimum(pos + 1, w).astype(F32) for w in POOL_WINDOWS}
    lane = jax.lax.broadcasted_iota(jnp.int32, (n, LANES), 1)
    mixed = []
    for c in range(POOL_WIDTH // LANES):
        cols = slice(c * LANES, (c + 1) * LANES)
        sums = _window_sums(ubuf[row0:row0 + POOL_HALO + n, cols])
        lo_grp = (c * LANES) // POOL_GROUP_DIM
        hi_grp = ((c + 1) * LANES - 1) // POOL_GROUP_DIM
        w_lo, w_hi = POOL_WINDOWS[lo_grp], POOL_WINDOWS[hi_grp]
        pooled = sums[w_lo][POOL_HALO:] * inv_cnt[w_lo]
        if hi_grp != lo_grp:
            split = hi_grp * POOL_GROUP_DIM - c * LANES
            pooled = jnp.where(lane < split, pooled, sums[w_hi][POOL_HALO:] * inv_cnt[w_hi])
        u = ubuf[POOL_HALO + row0:POOL_HALO + row0 + n, cols]
        mixed.append((pooled - u).astype(BF16))
    return jnp.concatenate(mixed, axis=-1)


def _pool_layer_kernel(x_ref, g_ref, w_u_ref, w_mq_ref, w_gate_ref, w_grp_ref, scale_ref,
                       kbd_ref, vbd_ref, w_out_ref, o_ref, ubuf):
    j = pl.program_id(1)
    t = x_ref.shape[1]
    n = ROW_CHUNK
    chunks = range(t // n)

    @pl.when(j == 0)
    def _():
        ubuf[0:POOL_HALO, :] = jnp.zeros((POOL_HALO, POOL_WIDTH), F32)

    @pl.when(j > 0)
    def _():
        ubuf[0:POOL_HALO, :] = ubuf[t:t + POOL_HALO, :]

    xs = [x_ref[0, c * n:(c + 1) * n, :] for c in chunks]
    hns = [_rms_norm(x, g_ref[...]).astype(BF16) for x in xs]
    for c in chunks:
        ubuf[POOL_HALO + c * n:POOL_HALO + (c + 1) * n, :] = _dot(hns[c], w_u_ref[...])
    mqs = [_dot(hn, w_mq_ref[...]) for hn in hns]
    gates = [_silu(_dot(hn, w_gate_ref[...])) for hn in hns]
    mixed = [_pooled_mix(ubuf, c * n, n, j * t + c * n) for c in chunks]
    toks = [_dot(m, w_grp_ref[...]) * scale_ref[...] for m in mixed]
    mem_os = [_memory_attention(mq, kbd_ref[0, 0], vbd_ref[0, 0]) for mq in mqs]
    for c in chunks:
        y = _dot((toks[c] * gates[c][:, :POOL_WIDTH]).astype(BF16), w_out_ref[:POOL_WIDTH, :])
        y += _dot((mem_os[c] * gates[c][:, POOL_WIDTH:]).astype(BF16), w_out_ref[POOL_WIDTH:, :])
        o_ref[0, c * n:(c + 1) * n, :] = xs[c] + y


def _pool_layer(x, g, w_u, w_mq, w_gate, w_grp, scale, kbd, vbd, w_out):
    b, s, _ = x.shape
    t = SEQ_TILE
    const = lambda shape: pl.BlockSpec(shape, lambda i, j: (0,) * len(shape))
    return pl.pallas_call(
        _pool_layer_kernel,
        grid=(b, s // t),
        in_specs=[
            pl.BlockSpec((1, t, D_MODEL), lambda i, j: (i, j, 0)),
            const(g.shape), const(w_u.shape), const(w_mq.shape), const(w_gate.shape),
            const(w_grp.shape), const(scale.shape),
            pl.BlockSpec((1, 1) + kbd.shape[2:], lambda i, j: (i, 0, 0, 0)),
            pl.BlockSpec((1, 1) + vbd.shape[2:], lambda i, j: (i, 0, 0, 0)),
            const(w_out.shape),
        ],
        out_specs=pl.BlockSpec((1, t, D_MODEL), lambda i, j: (i, j, 0)),
        out_shape=jax.ShapeDtypeStruct(x.shape, F32),
        scratch_shapes=[pltpu.VMEM((POOL_HALO + t, POOL_WIDTH), F32)],
        compiler_params=pltpu.CompilerParams(
            dimension_semantics=("parallel", "arbitrary"), vmem_limit_bytes=VMEM_LIMIT),
        name="pool_layer",
    )(x, g, w_u, w_mq, w_gate, w_grp, scale, kbd, vbd, w_out)


def _mla_pre_kernel(h_ref, g_ref, cos_ref, sin_ref, cos_t_ref, sin_t_ref, w_cq_ref, w_ckv_ref,
                    w_kr_ref, w_mq_ref, w_gate_ref, qg_ref, w_uq_t_ref, kvg_ref, w_uk_ref,
                    w_uv_t_ref, kbd_ref, vbd_ref, w_out_mem_ref,
                    q_ref, k_ref, v_ref, sg_ref, hp_ref):
    t = h_ref.shape[1]
    n = ROW_CHUNK
    chunks = range(t // n)
    heads = range(MLA_HEADS)
    half = QK_ROPE // 2
    rows = [slice(c * n, (c + 1) * n) for c in chunks]

    hs = [h_ref[0, rows[c], :] for c in chunks]
    hns = [_rms_norm(h, g_ref[...]).astype(BF16) for h in hs]
    cqns = [_rms_norm(_dot(hn, w_cq_ref[...]), qg_ref[...]).astype(BF16) for hn in hns]
    ckvns = [_rms_norm(_dot(hn, w_ckv_ref[...]), kvg_ref[...]).astype(BF16) for hn in hns]
    krs = [_dot(hn, w_kr_ref[...]) for hn in hns]
    mqs = [_dot(hn, w_mq_ref[...]) for hn in hns]
    gates = [_silu(_dot(hn, w_gate_ref[...])) for hn in hns]

    q_ts = [_dot_nt(w_uq_t_ref[...], cqn) * Q_SCALE for cqn in cqns]
    for c in chunks:
        cos_t, sin_t = cos_t_ref[0, :, rows[c]], sin_t_ref[0, :, rows[c]]
        for hd in heads:
            base = hd * QK_DIM
            x1 = q_ts[c][base + QK_NOPE:base + QK_NOPE + half]
            x2 = q_ts[c][base + QK_NOPE + half:base + QK_DIM]
            q_ref[0, hd, 0:QK_NOPE, rows[c]] = q_ts[c][base:base + QK_NOPE].astype(BF16)
            q_ref[0, hd, QK_NOPE:QK_NOPE + half, rows[c]] = (
                x1 * cos_t - x2 * sin_t).astype(BF16)
            q_ref[0, hd, QK_NOPE + half:QK_DIM, rows[c]] = (
                x2 * cos_t + x1 * sin_t).astype(BF16)

    k_nopes = [_dot(ckvn, w_uk_ref[...]) for ckvn in ckvns]
    v_ts = [_dot_nt(w_uv_t_ref[...], ckvn) for ckvn in ckvns]
    for c in chunks:
        kr = krs[c]
        kr_rot = jnp.concatenate([kr[:, half:], kr[:, :half]], axis=-1)
        k_rope = kr * cos_ref[0, rows[c], :] + kr_rot * sin_ref[0, rows[c], :]
        kt, off = divmod(c * n, ATTN_TILE)
        for hd in heads:
            k_ref[0, hd, rows[c], :] = jnp.concatenate(
                [k_nopes[c][:, hd * QK_NOPE:(hd + 1) * QK_NOPE], k_rope], axis=-1).astype(BF16)
            v_ref[0, hd, kt, :, off:off + n] = v_ts[c][hd * V_HEAD:(hd + 1) * V_HEAD].astype(BF16)

    mem_os = [_memory_attention(mq, kbd_ref[0, 0], vbd_ref[0, 0]) for mq in mqs]
    for c in chunks:
        sg_ref[0, rows[c], :] = gates[c][:, :MLA_WIDTH]
        hp_ref[0, rows[c], :] = hs[c] + _dot(
            (mem_os[c] * gates[c][:, MLA_WIDTH:]).astype(BF16), w_out_mem_ref[...])


def _mla_pre(h, g, cos2, sin2, cos_t, sin_t, w_cq, w_ckv, w_kr, w_mq, w_gate, qg, w_uq_t, kvg,
             w_uk, w_uv_t, kbd, vbd, w_out_mem):
    b, s, _ = h.shape
    t = SEQ_TILE
    n_kt = t // ATTN_TILE
    const = lambda shape: pl.BlockSpec(shape, lambda i, j: (0,) * len(shape))
    row = lambda width: pl.BlockSpec((1, t, width), lambda i, j: (i, j, 0))
    col = lambda height: pl.BlockSpec((1, height, t), lambda i, j: (i, 0, j))
    return pl.pallas_call(
        _mla_pre_kernel,
        grid=(b, s // t),
        in_specs=[
            row(D_MODEL), const(g.shape), row(QK_ROPE), row(QK_ROPE),
            col(QK_ROPE // 2), col(QK_ROPE // 2),
            const(w_cq.shape), const(w_ckv.shape), const(w_kr.shape), const(w_mq.shape),
            const(w_gate.shape), const(qg.shape), const(w_uq_t.shape), const(kvg.shape),
            const(w_uk.shape), const(w_uv_t.shape),
            pl.BlockSpec((1, 1) + kbd.shape[2:], lambda i, j: (i, 1, 0, 0)),
            pl.BlockSpec((1, 1) + vbd.shape[2:], lambda i, j: (i, 1, 0, 0)),
            const(w_out_mem.shape),
        ],
        out_specs=[
            pl.BlockSpec((1, MLA_HEADS, QK_DIM, t), lambda i, j: (i, 0, 0, j)),
            pl.BlockSpec((1, MLA_HEADS, t, QK_DIM), lambda i, j: (i, 0, j, 0)),
            pl.BlockSpec((1, MLA_HEADS, n_kt, V_HEAD, ATTN_TILE), lambda i, j: (i, 0, j, 0, 0)),
            row(MLA_WIDTH), row(D_MODEL),
        ],
        out_shape=[
            jax.ShapeDtypeStruct((b, MLA_HEADS, QK_DIM, s), BF16),
            jax.ShapeDtypeStruct((b, MLA_HEADS, s, QK_DIM), BF16),
            jax.ShapeDtypeStruct((b, MLA_HEADS, s // ATTN_TILE, V_HEAD, ATTN_TILE), BF16),
            jax.ShapeDtypeStruct((b, s, MLA_WIDTH), F32),
            jax.ShapeDtypeStruct((b, s, D_MODEL), F32),
        ],
        compiler_params=pltpu.CompilerParams(
            dimension_semantics=("parallel", "parallel"), vmem_limit_bytes=VMEM_LIMIT),
        name="mla_pre",
    )(h, g, cos2, sin2, cos_t, sin_t, w_cq, w_ckv, w_kr, w_mq, w_gate, qg, w_uq_t, kvg, w_uk,
      w_uv_t, kbd, vbd, w_out_mem)


def _sublane_allreduce(x, op):
    for shift in (4, 2, 1):
        x = op(x, pltpu.roll(x, shift, 0))
    return x


def _mla_attn_kernel(q_ref, k_ref, v_ref, sg_ref, hp_ref, w_out_ref, g_ref, o_ref,
                     acc_sc, tok_sc):
    i = pl.program_id(1)
    tq = q_ref.shape[3]
    tk = v_ref.shape[4]
    groups = tk // SUBLANES
    k_chunk = jax.lax.broadcasted_iota(jnp.int32, (tk, tq), 0) // CHUNK
    q_chunk = jax.lax.broadcasted_iota(jnp.int32, (tk, tq), 1) // CHUNK
    diag_mask = k_chunk <= q_chunk

    acc_sc[...] = jnp.zeros(acc_sc.shape, F32)
    heads = range(MLA_HEADS)

    def step(jt, carry, mask):
        m_old, l_old = carry
        start = pl.multiple_of(jt * tk, tk)
        s = [_dot(k_ref[0, hd, pl.ds(start, tk), :], q_ref[0, hd]) for hd in heads]
        m_new, l_new, alpha, p = [], [], [], []
        for hd in heads:
            sh = s[hd] if mask is None else jnp.where(mask, s[hd], NEG_INF)
            s3 = sh.reshape(groups, SUBLANES, tq)
            m = jnp.maximum(m_old[hd], _sublane_allreduce(jnp.max(s3, axis=0), jnp.maximum))
            a = jnp.exp2(m_old[hd] - m)
            p3 = jnp.exp2(s3 - m[None])
            m_new.append(m)
            alpha.append(a)
            l_new.append(a * l_old[hd] + jnp.sum(p3, axis=0))
            p.append(p3.reshape(tk, tq).astype(BF16))
        pv = [_dot(v_ref[0, hd, jt], p[hd]) for hd in heads]
        for hd in heads:
            acc = acc_sc[hd].reshape(V_HEAD // SUBLANES, SUBLANES, tq)
            acc_sc[hd] = (alpha[hd][None] * acc).reshape(V_HEAD, tq) + pv[hd]
        return tuple(m_new), tuple(l_new)

    init = (tuple(jnp.full((SUBLANES, tq), NEG_INF, F32) for _ in heads),
            tuple(jnp.zeros((SUBLANES, tq), F32) for _ in heads))
    carry = jax.lax.fori_loop(0, i, lambda jt, c: step(jt, c, None), init)
    _, l_fin = step(i, carry, diag_mask)

    for hd in heads:
        l_tot = _sublane_allreduce(l_fin[hd], jnp.add)
        o_t = acc_sc[hd].reshape(V_HEAD // SUBLANES, SUBLANES, tq) / l_tot[None]
        tok_sc[:, hd * V_HEAD:(hd + 1) * V_HEAD] = o_t.reshape(V_HEAD, tq).T

    rows = [slice(c * ROW_CHUNK, (c + 1) * ROW_CHUNK) for c in range(tq // ROW_CHUNK)]
    branches = [(tok_sc[r, :] * sg_ref[0, r, :]).astype(BF16) for r in rows]
    outs = [hp_ref[0, r, :] + _dot(br, w_out_ref[...]) for r, br in zip(rows, branches)]
    for r, out in zip(rows, outs):
        o_ref[0, r, :] = _rms_norm(out, g_ref[...])


def _mla_attn(q_t, k, v_t, sg, hp, w_out_tok, final_g):
    b, _, s, _ = k.shape
    t = ATTN_TILE
    const = lambda shape: pl.BlockSpec(shape, lambda i, j: (0,) * len(shape))
    row = lambda width: pl.BlockSpec((1, t, width), lambda i, j: (i, j, 0))
    return pl.pallas_call(
        _mla_attn_kernel,
        grid=(b, s // t),
        in_specs=[
            pl.BlockSpec((1, MLA_HEADS, QK_DIM, t), lambda i, j: (i, 0, 0, j)),
            pl.BlockSpec((1, MLA_HEADS, s, QK_DIM), lambda i, j: (i, 0, 0, 0)),
            pl.BlockSpec((1,) + v_t.shape[1:], lambda i, j: (i, 0, 0, 0, 0)),
            row(MLA_WIDTH), row(D_MODEL), const(w_out_tok.shape), const(final_g.shape),
        ],
        out_specs=row(D_MODEL),
        out_shape=jax.ShapeDtypeStruct((b, s, D_MODEL), F32),
        scratch_shapes=[
            pltpu.VMEM((MLA_HEADS, V_HEAD, t), F32),
            pltpu.VMEM((t, MLA_WIDTH), F32),
        ],
        compiler_params=pltpu.CompilerParams(
            dimension_semantics=("parallel", "arbitrary"), vmem_limit_bytes=VMEM_LIMIT),
        name="mla_attn",
    )(q_t, k, v_t, sg, hp, w_out_tok, final_g)


def _block_diag(w_group):
    g, c, d = w_group.shape
    out = jnp.zeros((g * c, g * d), w_group.dtype)
    for i in range(g):
        out = out.at[i * c:(i + 1) * c, i * d:(i + 1) * d].set(w_group[i])
    return out


def kernel(x, mem, positions, norm_g, mem_norm_g, w_mem_kv, w_out, pool_w_in, pool_w_group,
           pool_scale, mla_w_in, mla_q_norm_g, mla_w_uq, mla_kv_norm_g, mla_w_ukv, final_norm_g):
    depth = w_mem_kv.shape[0]
    assert depth == 2 and pool_w_in.shape[0] == 1 and mla_w_in.shape[0] == 1
    bf = lambda a: a.astype(BF16)
    row = lambda a: a.reshape(1, -1)

    inv_freq = ROPE_THETA ** (-(jnp.arange(0, QK_ROPE, 2, dtype=F32) / QK_ROPE))
    ang = positions.astype(F32)[..., None] * inv_freq
    cos, sin = jnp.cos(ang), jnp.sin(ang)
    cos2 = jnp.concatenate([cos, cos], axis=-1)
    sin2 = jnp.concatenate([-sin, sin], axis=-1)
    cos_t, sin_t = jnp.swapaxes(cos, 1, 2), jnp.swapaxes(sin, 1, 2)

    w_mem_cat = bf(jnp.concatenate([w_mem_kv[i] for i in range(depth)], axis=1))
    kbd, vbd = _mem_kv(mem, row(mem_norm_g), w_mem_cat)

    pw = pool_w_in[0]
    h1 = _pool_layer(
        x, row(norm_g[0]),
        bf(pw[:, :POOL_WIDTH]), bf(pw[:, POOL_WIDTH:POOL_WIDTH + MEM_WIDTH]),
        bf(pw[:, POOL_WIDTH + MEM_WIDTH:]), bf(_block_diag(pool_w_group[0])),
        row(pool_scale[0]), kbd, vbd, bf(w_out[0]))

    mw = mla_w_in[0]
    o_ckv = Q_LORA
    o_kr = o_ckv + KV_LORA
    o_mq = o_kr + QK_ROPE
    o_gate = o_mq + MEM_WIDTH
    w_ukv = mla_w_ukv[0].reshape(KV_LORA, MLA_HEADS, QK_NOPE + V_HEAD)
    w_uk = w_ukv[:, :, :QK_NOPE].reshape(KV_LORA, MLA_HEADS * QK_NOPE)
    w_uv_t = w_ukv[:, :, QK_NOPE:].reshape(KV_LORA, MLA_WIDTH).T
    q_t, k, v_t, sg, hp = _mla_pre(
        h1, row(norm_g[1]), cos2, sin2, cos_t, sin_t,
        bf(mw[:, :o_ckv]), bf(mw[:, o_ckv:o_kr]), bf(mw[:, o_kr:o_mq]), bf(mw[:, o_mq:o_gate]),
        bf(mw[:, o_gate:]), row(mla_q_norm_g[0]), bf(mla_w_uq[0].T), row(mla_kv_norm_g[0]),
        bf(w_uk), bf(w_uv_t), kbd, vbd, bf(w_out[1, MLA_WIDTH:]))

    return _mla_attn(q_t, k, v_t, sg, hp, bf(w_out[1, :MLA_WIDTH]), row(final_norm_g))
```

```python
import math

import jax
import jax.numpy as jnp
from jax.experimental import pallas as pl
from jax.experimental.pallas import tpu as pltpu

D_MODEL = 1024
CHUNK = 64
EPS = 1e-6
NEG_INF = -1e30

N_MEM = 256
MEM_HEADS = 4
MEM_HEAD_DIM = 64
MEM_WIDTH = MEM_HEADS * MEM_HEAD_DIM

POOL_WINDOWS = (2, 4, 8, 16)
POOL_GROUP_DIM = 192
POOL_WIDTH = len(POOL_WINDOWS) * POOL_GROUP_DIM
POOL_HALO = 16

MLA_HEADS = 6
QK_NOPE = 128
QK_ROPE = 64
QK_DIM = QK_NOPE + QK_ROPE
V_HEAD = 128
Q_LORA = 384
KV_LORA = 256
MLA_WIDTH = MLA_HEADS * V_HEAD
ROPE_THETA = 10000.0
Q_SCALE = QK_DIM ** -0.5 * math.log2(math.e)

MIX_WIDTH = POOL_WIDTH + MEM_WIDTH

LANES = 128
SUBLANES = 8
SEQ_TILE = 512
ROW_CHUNK = 256
ATTN_TILE = 512
VMEM_LIMIT = 56 * 1024 * 1024

BF16 = jnp.bfloat16
F32 = jnp.float32


def _rms_norm(x, g):
    ms = jnp.mean(x * x, axis=-1, keepdims=True)
    return x * jax.lax.rsqrt(ms + EPS) * g


def _silu(x):
    return x / (1.0 + jnp.exp(-x))


def _dot(a, b):
    return jnp.dot(a, b, preferred_element_type=F32)


def _dot_nt(a, b):
    return jax.lax.dot_general(a, b, (((1,), (1,)), ((), ())), preferred_element_type=F32)


def _memory_attention(mq, kbd, vbd):
    s = _dot(mq.astype(BF16), kbd) * (MEM_HEAD_DIM ** -0.5)
    probs = []
    for h in range(MEM_HEADS):
        sh = s[:, h * N_MEM:(h + 1) * N_MEM]
        e = jnp.exp(sh - jnp.max(sh, axis=-1, keepdims=True))
        probs.append((e / jnp.sum(e, axis=-1, keepdims=True)).astype(BF16))
    return _dot(jnp.concatenate(probs, axis=-1), vbd)


def _mem_kv_kernel(mem_ref, g_ref, w_ref, kbd_ref, vbd_ref):
    mem_n = _rms_norm(mem_ref[0], g_ref[...]).astype(BF16)
    kv = _dot(mem_n, w_ref[...])
    n_layers = kv.shape[1] // (2 * MEM_WIDTH)
    row_head = jax.lax.broadcasted_iota(jnp.int32, (MEM_WIDTH, N_MEM), 0) // MEM_HEAD_DIM
    col_head = jax.lax.broadcasted_iota(jnp.int32, (N_MEM, MEM_WIDTH), 1) // MEM_HEAD_DIM
    for layer in range(n_layers):
        base = layer * 2 * MEM_WIDTH
        k_t = kv[:, base:base + MEM_WIDTH].T
        v = kv[:, base + MEM_WIDTH:base + 2 * MEM_WIDTH]
        for h in range(MEM_HEADS):
            kbd_ref[0, layer, :, h * N_MEM:(h + 1) * N_MEM] = jnp.where(
                row_head == h, k_t, 0.0).astype(BF16)
            vbd_ref[0, layer, h * N_MEM:(h + 1) * N_MEM, :] = jnp.where(
                col_head == h, v, 0.0).astype(BF16)


def _mem_kv(mem, mem_norm_g, w_cat):
    b = mem.shape[0]
    n_layers = w_cat.shape[1] // (2 * MEM_WIDTH)
    return pl.pallas_call(
        _mem_kv_kernel,
        grid=(b,),
        in_specs=[
            pl.BlockSpec((1, N_MEM, D_MODEL), lambda i: (i, 0, 0)),
            pl.BlockSpec((1, D_MODEL), lambda i: (0, 0)),
            pl.BlockSpec(w_cat.shape, lambda i: (0, 0)),
        ],
        out_specs=[
            pl.BlockSpec((1, n_layers, MEM_WIDTH, MEM_HEADS * N_MEM), lambda i: (i, 0, 0, 0)),
            pl.BlockSpec((1, n_layers, MEM_HEADS * N_MEM, MEM_WIDTH), lambda i: (i, 0, 0, 0)),
        ],
        out_shape=[
            jax.ShapeDtypeStruct((b, n_layers, MEM_WIDTH, MEM_HEADS * N_MEM), BF16),
            jax.ShapeDtypeStruct((b, n_layers, MEM_HEADS * N_MEM, MEM_WIDTH), BF16),
        ],
        compiler_params=pltpu.CompilerParams(
            dimension_semantics=("parallel",), vmem_limit_bytes=VMEM_LIMIT),
        name="mem_kv",
    )(mem, mem_norm_g, w_cat)


def _split_bf16(a):
    hi = a.astype(BF16)
    return hi, (a - hi.astype(F32)).astype(BF16)


def _pool_fold_kernel(w_u_ref, w_bd_ref, o_ref):
    b_hi, b_lo = _split_bf16(w_bd_ref[...])
    n = ROW_CHUNK
    for c in range(w_u_ref.shape[0] // n):
        a_hi, a_lo = _split_bf16(w_u_ref[c * n:(c + 1) * n, :])
        o_ref[c * n:(c + 1) * n, :] = (
            _dot(a_hi, b_hi) + _dot(a_hi, b_lo) + _dot(a_lo, b_hi)).astype(BF16)


def _pool_fold(w_u, w_bd):
    return pl.pallas_call(
        _pool_fold_kernel,
        out_shape=jax.ShapeDtypeStruct(w_u.shape, BF16),
        compiler_params=pltpu.CompilerParams(vmem_limit_bytes=VMEM_LIMIT),
        name="pool_fold",
    )(w_u, w_bd)


def _window_sums(a):
    s2 = a + pltpu.roll(a, 1, 0)
    s4 = s2 + pltpu.roll(s2, 2, 0)
    s8 = s4 + pltpu.roll(s4, 4, 0)
    s16 = s8 + pltpu.roll(s8, 8, 0)
    return {2: s2, 4: s4, 8: s8, 16: s16}


def _pooled_mix(ubuf, row0, n, pos0):
    pos = pos0 + jax.lax.broadcasted_iota(jnp.int32, (n, 1), 0)
    inv_cnt = {w: 1.0 / jnp.minimum(pos + 1, w).astype(F32) for w in POOL_WINDOWS}
    lane = jax.lax.broadcasted_iota(jnp.int32, (n, LANES), 1)
    mixed = []
    for c in range(POOL_WIDTH // LANES):
        cols = slice(c * LANES, (c + 1) * LANES)
        sums = _window_sums(ubuf[row0:row0 + POOL_HALO + n, cols])
        lo_grp = (c * LANES) // POOL_GROUP_DIM
        hi_grp = ((c + 1) * LANES - 1) // POOL_GROUP_DIM
        w_lo, w_hi = POOL_WINDOWS[lo_grp], POOL_WINDOWS[hi_grp]
        pooled = sums[w_lo][POOL_HALO:] * inv_cnt[w_lo]
        if hi_grp != lo_grp:
            split = hi_grp * POOL_GROUP_DIM - c * LANES
            pooled = jnp.where(lane < split, pooled, sums[w_hi][POOL_HALO:] * inv_cnt[w_hi])
        mixed.append(pooled - ubuf[POOL_HALO + row0:POOL_HALO + row0 + n, cols])
    return jnp.concatenate(mixed, axis=-1)


def _pool_layer_kernel(x_ref, g_ref, w_u_ref, w_mq_ref, w_gate_ref, scale_ref,
                       kbd_ref, vbd_ref, w_out_ref, o_ref, ubuf):
    j = pl.program_id(1)
    t = x_ref.shape[1]
    n = ROW_CHUNK
    chunks = range(t // n)

    @pl.when(j == 0)
    def _():
        ubuf[0:POOL_HALO, :] = jnp.zeros((POOL_HALO, POOL_WIDTH), F32)

    @pl.when(j > 0)
    def _():
        ubuf[0:POOL_HALO, :] = ubuf[t:t + POOL_HALO, :]

    xs = [x_ref[0, c * n:(c + 1) * n, :] for c in chunks]
    hns = [_rms_norm(x, g_ref[...]).astype(BF16) for x in xs]
    for c in chunks:
        ubuf[POOL_HALO + c * n:POOL_HALO + (c + 1) * n, :] = _dot(hns[c], w_u_ref[...])
    mqs = [_dot(hn, w_mq_ref[...]) for hn in hns]
    gates = [_silu(_dot(hn, w_gate_ref[...])) for hn in hns]
    toks = [_pooled_mix(ubuf, c * n, n, j * t + c * n) * scale_ref[...] for c in chunks]
    mem_os = [_memory_attention(mq, kbd_ref[0, 0], vbd_ref[0, 0]) for mq in mqs]
    for c in chunks:
        y = _dot((toks[c] * gates[c][:, :POOL_WIDTH]).astype(BF16), w_out_ref[:POOL_WIDTH, :])
        y += _dot((mem_os[c] * gates[c][:, POOL_WIDTH:]).astype(BF16), w_out_ref[POOL_WIDTH:, :])
        o_ref[0, c * n:(c + 1) * n, :] = xs[c] + y


def _pool_layer(x, g, w_u, w_mq, w_gate, scale, kbd, vbd, w_out):
    b, s, _ = x.shape
    t = SEQ_TILE
    const = lambda shape: pl.BlockSpec(shape, lambda i, j: (0,) * len(shape))
    return pl.pallas_call(
        _pool_layer_kernel,
        grid=(b, s // t),
        in_specs=[
            pl.BlockSpec((1, t, D_MODEL), lambda i, j: (i, j, 0)),
            const(g.shape), const(w_u.shape), const(w_mq.shape), const(w_gate.shape),
            const(scale.shape),
            pl.BlockSpec((1, 1) + kbd.shape[2:], lambda i, j: (i, 0, 0, 0)),
            pl.BlockSpec((1, 1) + vbd.shape[2:], lambda i, j: (i, 0, 0, 0)),
            const(w_out.shape),
        ],
        out_specs=pl.BlockSpec((1, t, D_MODEL), lambda i, j: (i, j, 0)),
        out_shape=jax.ShapeDtypeStruct(x.shape, F32),
        scratch_shapes=[pltpu.VMEM((POOL_HALO + t, POOL_WIDTH), F32)],
        compiler_params=pltpu.CompilerParams(
            dimension_semantics=("parallel", "arbitrary"), vmem_limit_bytes=VMEM_LIMIT),
        name="pool_layer",
    )(x, g, w_u, w_mq, w_gate, scale, kbd, vbd, w_out)


def _rope_t(x, cos_t, sin_t):
    half = QK_ROPE // 2
    x1, x2 = x[:half], x[half:]
    return x1 * cos_t - x2 * sin_t, x2 * cos_t + x1 * sin_t


def _mla_pre_kernel(h_ref, g_ref, pos_ref, inv_freq_ref, w_cq_t_ref, w_ckv_ref,
                    w_kr_t_ref, w_mq_ref, w_gate_ref, qg_ref, w_uq_t_ref, kvg_ref, w_uk_ref,
                    w_uv_t_ref, kbd_ref, vbd_ref, w_out_mem_ref,
                    q_ref, k_ref, v_ref, sg_ref, hp_ref):
    t = h_ref.shape[1]
    n = ROW_CHUNK
    chunks = range(t // n)
    heads = range(MLA_HEADS)
    half = QK_ROPE // 2
    rows = [slice(c * n, (c + 1) * n) for c in chunks]

    hs = [h_ref[0, rows[c], :] for c in chunks]
    hns = [_rms_norm(h, g_ref[...]).astype(BF16) for h in hs]
    mqs = [_dot(hn, w_mq_ref[...]) for hn in hns]
    mem_os = [_memory_attention(mq, kbd_ref[0, 0], vbd_ref[0, 0]) for mq in mqs]
    gates = [_silu(_dot(hn, w_gate_ref[...])) for hn in hns]
    for c in chunks:
        sg_ref[0, rows[c], :] = gates[c][:, :MLA_WIDTH]
        hp_ref[0, rows[c], :] = hs[c] + _dot(
            (mem_os[c] * gates[c][:, MLA_WIDTH:]).astype(BF16), w_out_mem_ref[...])

    cq_ts = [_dot_nt(w_cq_t_ref[...], hn) for hn in hns]
    kr_ts = [_dot_nt(w_kr_t_ref[...], hn) for hn in hns]
    ckvns = [_rms_norm(_dot(hn, w_ckv_ref[...]), kvg_ref[...]).astype(BF16) for hn in hns]
    angs =[pos_ref[0, :, rows[c]].astype(F32) * inv_freq_ref[...] for c in chunks]
    cos_ts, sin_ts = [jnp.cos(a) for a in angs], [jnp.sin(a) for a in angs]
    cqn_ts = [(cq * jax.lax.rsqrt(jnp.mean(cq * cq, axis=0, keepdims=True) + EPS)
               * qg_ref[...]).astype(BF16) for cq in cq_ts]

    q_ts = [_dot(w_uq_t_ref[...], cqn_t) * Q_SCALE for cqn_t in cqn_ts]
    for c in chunks:
        for hd in heads:
            base = hd * QK_DIM
            r1, r2 = _rope_t(q_ts[c][base + QK_NOPE:base + QK_DIM], cos_ts[c], sin_ts[c])
            q_ref[0, hd, 0:QK_NOPE, rows[c]] = q_ts[c][base:base + QK_NOPE].astype(BF16)
            q_ref[0, hd, QK_NOPE:QK_NOPE + half, rows[c]] = r1.astype(BF16)
            q_ref[0, hd, QK_NOPE + half:QK_DIM, rows[c]] = r2.astype(BF16)

    k_nopes = [_dot(ckvn, w_uk_ref[...]) for ckvn in ckvns]
    v_ts = [_dot_nt(w_uv_t_ref[...], ckvn) for ckvn in ckvns]
    for c in chunks:
        r1, r2 = _rope_t(kr_ts[c], cos_ts[c], sin_ts[c])
        k_rope = jnp.concatenate([r1, r2, jnp.zeros((LANES - QK_ROPE, n), F32)], axis=0).T
        k_rope = k_rope[:, :QK_ROPE]
        kt, off = divmod(c * n, ATTN_TILE)
        for hd in heads:
            k_ref[0, hd, rows[c], :] = jnp.concatenate(
                [k_nopes[c][:, hd * QK_NOPE:(hd + 1) * QK_NOPE], k_rope], axis=-1).astype(BF16)
            v_ref[0, hd, kt, :, off:off + n] = v_ts[c][hd * V_HEAD:(hd + 1) * V_HEAD].astype(BF16)


def _mla_pre(h, g, pos, inv_freq, w_cq_t, w_ckv, w_kr_t, w_mq, w_gate, qg, w_uq_t, kvg,
             w_uk, w_uv_t, kbd, vbd, w_out_mem):
    b, s, _ = h.shape
    t = SEQ_TILE
    n_kt = t // ATTN_TILE
    const = lambda shape: pl.BlockSpec(shape, lambda i, j: (0,) * len(shape))
    row = lambda width: pl.BlockSpec((1, t, width), lambda i, j: (i, j, 0))
    return pl.pallas_call(
        _mla_pre_kernel,
        grid=(b, s // t),
        in_specs=[
            row(D_MODEL), const(g.shape),
            pl.BlockSpec((1, 1, t), lambda i, j: (i, 0, j)), const(inv_freq.shape),
            const(w_cq_t.shape), const(w_ckv.shape), const(w_kr_t.shape), const(w_mq.shape),
            const(w_gate.shape), const(qg.shape), const(w_uq_t.shape), const(kvg.shape),
            const(w_uk.shape), const(w_uv_t.shape),
            pl.BlockSpec((1, 1) + kbd.shape[2:], lambda i, j: (i, 1, 0, 0)),
            pl.BlockSpec((1, 1) + vbd.shape[2:], lambda i, j: (i, 1, 0, 0)),
            const(w_out_mem.shape),
        ],
        out_specs=[
            pl.BlockSpec((1, MLA_HEADS, QK_DIM, t), lambda i, j: (i, 0, 0, j)),
            pl.BlockSpec((1, MLA_HEADS, t, QK_DIM), lambda i, j: (i, 0, j, 0)),
            pl.BlockSpec((1, MLA_HEADS, n_kt, V_HEAD, ATTN_TILE), lambda i, j: (i, 0, j, 0, 0)),
            row(MLA_WIDTH), row(D_MODEL),
        ],
        out_shape=[
            jax.ShapeDtypeStruct((b, MLA_HEADS, QK_DIM, s), BF16),
            jax.ShapeDtypeStruct((b, MLA_HEADS, s, QK_DIM), BF16),
            jax.ShapeDtypeStruct((b, MLA_HEADS, s // ATTN_TILE, V_HEAD, ATTN_TILE), BF16),
            jax.ShapeDtypeStruct((b, s, MLA_WIDTH), F32),
            jax.ShapeDtypeStruct((b, s, D_MODEL), F32),
        ],
        compiler_params=pltpu.CompilerParams(
            dimension_semantics=("parallel", "parallel"), vmem_limit_bytes=VMEM_LIMIT),
        name="mla_pre",
    )(h, g, pos, inv_freq, w_cq_t, w_ckv, w_kr_t, w_mq, w_gate, qg, w_uq_t, kvg, w_uk,
      w_uv_t, kbd, vbd, w_out_mem)


def _sublane_allreduce(x, op):
    for shift in (4, 2, 1):
        x = op(x, pltpu.roll(x, shift, 0))
    return x


def _mla_attn_kernel(q_ref, k_ref, v_ref, sg_ref, hp_ref, w_out_ref, g_ref, o_ref,
                     acc_sc, tok_sc):
    i = pl.program_id(1)
    tq = q_ref.shape[3]
    tk = v_ref.shape[4]
    half = tk // 2

    acc_sc[...] = jnp.zeros(acc_sc.shape, F32)
    heads = range(MLA_HEADS)

    def step(jt, carry, k0, kn, q0, qn, masked):
        m_old, l_old = carry
        start = pl.multiple_of(jt * tk, tk) + k0
        qs = slice(q0, q0 + qn)
        s = [_dot(k_ref[0, hd, pl.ds(start, kn), :], q_ref[0, hd, :, qs]) for hd in heads]
        if masked:
            k_chunk = (k0 + jax.lax.broadcasted_iota(jnp.int32, (kn, qn), 0)) // CHUNK
            q_chunk = (q0 + jax.lax.broadcasted_iota(jnp.int32, (kn, qn), 1)) // CHUNK
            mask = k_chunk <= q_chunk
        m_new, l_new, alpha, p = [], [], [], []
        for hd in heads:
            sh = jnp.where(mask, s[hd], NEG_INF) if masked else s[hd]
            s3 = sh.reshape(kn // SUBLANES, SUBLANES, qn)
            m_prev, l_prev = m_old[hd][:, qs], l_old[hd][:, qs]
            m = jnp.maximum(m_prev, _sublane_allreduce(jnp.max(s3, axis=0), jnp.maximum))
            a = jnp.exp2(m_prev - m)
            p3 = jnp.exp2(s3 - m[None])
            l = a * l_prev + jnp.sum(p3, axis=0)
            if qn != tq:
                m, l = (jnp.concatenate(
                    ([old[:, :q0]] if q0 else []) + [new]
                    + ([old[:, q0 + qn:]] if q0 + qn < tq else []), axis=1)
                    for old, new in ((m_old[hd], m), (l_old[hd], l)))
            m_new.append(m)
            l_new.append(l)
            alpha.append(a)
            p.append(p3.reshape(kn, qn).astype(BF16))
        pv = [_dot(v_ref[0, hd, jt, :, k0:k0 + kn], p[hd]) for hd in heads]
        for hd in heads:
            acc = acc_sc[hd, :, qs].reshape(V_HEAD // SUBLANES, SUBLANES, qn)
            acc_sc[hd, :, qs] = (alpha[hd][None] * acc).reshape(V_HEAD, qn) + pv[hd]
        return tuple(m_new), tuple(l_new)

    init = (tuple(jnp.full((SUBLANES, tq), NEG_INF, F32) for _ in heads),
            tuple(jnp.zeros((SUBLANES, tq), F32) for _ in heads))
    carry = jax.lax.fori_loop(0, i, lambda jt, c: step(jt, c, 0, tk, 0, tq, False), init)
    carry = step(i, carry, 0, half, 0, tq, True)
    _, l_fin = step(i, carry, half, tk - half, half, tq - half, True)

    for hd in heads:
        l_tot = _sublane_allreduce(l_fin[hd], jnp.add)
        o_t = acc_sc[hd].reshape(V_HEAD // SUBLANES, SUBLANES, tq) / l_tot[None]
        tok_sc[:, hd * V_HEAD:(hd + 1) * V_HEAD] = o_t.reshape(V_HEAD, tq).T

    rows = [slice(c * ROW_CHUNK, (c + 1) * ROW_CHUNK) for c in range(tq // ROW_CHUNK)]
    branches = [(tok_sc[r, :] * sg_ref[0, r, :]).astype(BF16) for r in rows]
    outs = [hp_ref[0, r, :] + _dot(br, w_out_ref[...]) for r, br in zip(rows, branches)]
    for r, out in zip(rows, outs):
        o_ref[0, r, :] = _rms_norm(out, g_ref[...])


def _mla_attn(q_t, k, v_t, sg, hp, w_out_tok, final_g):
    b, _, s, _ = k.shape
    t = ATTN_TILE
    const = lambda shape: pl.BlockSpec(shape, lambda i, j: (0,) * len(shape))
    row = lambda width: pl.BlockSpec((1, t, width), lambda i, j: (i, j, 0))
    return pl.pallas_call(
        _mla_attn_kernel,
        grid=(b, s // t),
        in_specs=[
            pl.BlockSpec((1, MLA_HEADS, QK_DIM, t), lambda i, j: (i, 0, 0, j)),
            pl.BlockSpec((1, MLA_HEADS, s, QK_DIM), lambda i, j: (i, 0, 0, 0)),
            pl.BlockSpec((1,) + v_t.shape[1:], lambda i, j: (i, 0, 0, 0, 0)),
            row(MLA_WIDTH), row(D_MODEL), const(w_out_tok.shape), const(final_g.shape),
        ],
        out_specs=row(D_MODEL),
        out_shape=jax.ShapeDtypeStruct((b, s, D_MODEL), F32),
        scratch_shapes=[
            pltpu.VMEM((MLA_HEADS, V_HEAD, t), F32),
            pltpu.VMEM((t, MLA_WIDTH), F32),
        ],
        compiler_params=pltpu.CompilerParams(
            dimension_semantics=("parallel", "arbitrary"), vmem_limit_bytes=VMEM_LIMIT),
        name="mla_attn",
    )(q_t, k, v_t, sg, hp, w_out_tok, final_g)


def _block_diag(w_group):
    g, c, d = w_group.shape
    out = jnp.zeros((g * c, g * d), w_group.dtype)
    for i in range(g):
        out = out.at[i * c:(i + 1) * c, i * d:(i + 1) * d].set(w_group[i])
    return out


def kernel(x, mem, positions, norm_g, mem_norm_g, w_mem_kv, w_out, pool_w_in, pool_w_group,
           pool_scale, mla_w_in, mla_q_norm_g, mla_w_uq, mla_kv_norm_g, mla_w_ukv, final_norm_g):
    depth = w_mem_kv.shape[0]
    assert depth == 2 and pool_w_in.shape[0] == 1 and mla_w_in.shape[0] == 1
    bf = lambda a: a.astype(BF16)
    row = lambda a: a.reshape(1, -1)

    inv_freq = ROPE_THETA ** (-(jnp.arange(0, QK_ROPE, 2, dtype=F32) / QK_ROPE))

    w_mem_cat = bf(jnp.concatenate([w_mem_kv[i] for i in range(depth)], axis=1))
    kbd, vbd = _mem_kv(mem, row(mem_norm_g), w_mem_cat)

    pw = pool_w_in[0]
    h1 = _pool_layer(
        x, row(norm_g[0]),
        _pool_fold(pw[:, :POOL_WIDTH], _block_diag(pool_w_group[0])),
        bf(pw[:, POOL_WIDTH:POOL_WIDTH + MEM_WIDTH]), bf(pw[:, POOL_WIDTH + MEM_WIDTH:]),
        row(pool_scale[0]), kbd, vbd, bf(w_out[0]))

    mw = mla_w_in[0]
    o_ckv = Q_LORA
    o_kr = o_ckv + KV_LORA
    o_mq = o_kr + QK_ROPE
    o_gate = o_mq + MEM_WIDTH
    w_ukv = mla_w_ukv[0].reshape(KV_LORA, MLA_HEADS, QK_NOPE + V_HEAD)
    w_uk = w_ukv[:, :, :QK_NOPE].reshape(KV_LORA, MLA_HEADS * QK_NOPE)
    w_uv_t = w_ukv[:, :, QK_NOPE:].reshape(KV_LORA, MLA_WIDTH).T
    q_t, k, v_t, sg, hp = _mla_pre(
        h1, row(norm_g[1]), positions[:, None, :], inv_freq[:, None],
        bf(mw[:, :o_ckv].T), bf(mw[:, o_ckv:o_kr]), bf(mw[:, o_kr:o_mq].T), bf(mw[:, o_mq:o_gate]),
        bf(mw[:, o_gate:]), mla_q_norm_g[0][:, None], bf(mla_w_uq[0].T), row(mla_kv_norm_g[0]),
        bf(w_uk), bf(w_uv_t), kbd, vbd, bf(w_out[1, MLA_WIDTH:]))

    return _mla_attn(q_t, k, v_t, sg, hp, bf(w_out[1, :MLA_WIDTH]), row(final_norm_g))
```

```python
import math

import jax
import jax.numpy as jnp
from jax.experimental import pallas as pl
from jax.experimental.pallas import tpu as pltpu

D_MODEL = 1024
CHUNK = 64
EPS = 1e-6
NEG_INF = -1e30

N_MEM = 256
MEM_HEADS = 4
MEM_HEAD_DIM = 64
MEM_WIDTH = MEM_HEADS * MEM_HEAD_DIM

POOL_WINDOWS = (2, 4, 8, 16)
POOL_GROUP_DIM = 192
POOL_WIDTH = len(POOL_WINDOWS) * POOL_GROUP_DIM
POOL_HALO = 16

MLA_HEADS = 6
QK_NOPE = 128
QK_ROPE = 64
QK_DIM = QK_NOPE + QK_ROPE
V_HEAD = 128
Q_LORA = 384
KV_LORA = 256
MLA_WIDTH = MLA_HEADS * V_HEAD
ROPE_THETA = 10000.0
Q_SCALE = QK_DIM ** -0.5 * math.log2(math.e)

MIX_WIDTH = POOL_WIDTH + MEM_WIDTH

LANES = 128
SUBLANES = 8
SEQ_TILE = 512
ROW_CHUNK = 256
ATTN_TILE = 512
VMEM_LIMIT = 56 * 1024 * 1024

BF16 = jnp.bfloat16
F32 = jnp.float32


def _rms_norm(x, g):
    ms = jnp.mean(x * x, axis=-1, keepdims=True)
    return x * jax.lax.rsqrt(ms + EPS) * g


def _silu(x):
    return x / (1.0 + jnp.exp(-x))


def _dot(a, b):
    return jnp.dot(a, b, preferred_element_type=F32)


def _dot_nt(a, b):
    return jax.lax.dot_general(a, b, (((1,), (1,)), ((), ())), preferred_element_type=F32)


def _memory_attention(mq, kbd, vbd):
    s = _dot(mq.astype(BF16), kbd) * (MEM_HEAD_DIM ** -0.5)
    probs = []
    for h in range(MEM_HEADS):
        sh = s[:, h * N_MEM:(h + 1) * N_MEM]
        e = jnp.exp(sh - jnp.max(sh, axis=-1, keepdims=True))
        probs.append((e / jnp.sum(e, axis=-1, keepdims=True)).astype(BF16))
    return _dot(jnp.concatenate(probs, axis=-1), vbd)


def _mem_kv_kernel(mem_ref, g_ref, w_ref, kbd_ref, vbd_ref):
    mem_n = _rms_norm(mem_ref[0], g_ref[...]).astype(BF16)
    kv = _dot(mem_n, w_ref[...])
    n_layers = kv.shape[1] // (2 * MEM_WIDTH)
    row_head = jax.lax.broadcasted_iota(jnp.int32, (MEM_WIDTH, N_MEM), 0) // MEM_HEAD_DIM
    col_head = jax.lax.broadcasted_iota(jnp.int32, (N_MEM, MEM_WIDTH), 1) // MEM_HEAD_DIM
    for layer in range(n_layers):
        base = layer * 2 * MEM_WIDTH
        k_t = kv[:, base:base + MEM_WIDTH].T
        v = kv[:, base + MEM_WIDTH:base + 2 * MEM_WIDTH]
        for h in range(MEM_HEADS):
            kbd_ref[0, layer, :, h * N_MEM:(h + 1) * N_MEM] = jnp.where(
                row_head == h, k_t, 0.0).astype(BF16)
            vbd_ref[0, layer, h * N_MEM:(h + 1) * N_MEM, :] = jnp.where(
                col_head == h, v, 0.0).astype(BF16)


def _mem_kv(mem, mem_norm_g, w_cat):
    b = mem.shape[0]
    n_layers = w_cat.shape[1] // (2 * MEM_WIDTH)
    return pl.pallas_call(
        _mem_kv_kernel,
        grid=(b,),
        in_specs=[
            pl.BlockSpec((1, N_MEM, D_MODEL), lambda i: (i, 0, 0)),
            pl.BlockSpec((1, D_MODEL), lambda i: (0, 0)),
            pl.BlockSpec(w_cat.shape, lambda i: (0, 0)),
        ],
        out_specs=[
            pl.BlockSpec((1, n_layers, MEM_WIDTH, MEM_HEADS * N_MEM), lambda i: (i, 0, 0, 0)),
            pl.BlockSpec((1, n_layers, MEM_HEADS * N_MEM, MEM_WIDTH), lambda i: (i, 0, 0, 0)),
        ],
        out_shape=[
            jax.ShapeDtypeStruct((b, n_layers, MEM_WIDTH, MEM_HEADS * N_MEM), BF16),
            jax.ShapeDtypeStruct((b, n_layers, MEM_HEADS * N_MEM, MEM_WIDTH), BF16),
        ],
        compiler_params=pltpu.CompilerParams(
            dimension_semantics=("parallel",), vmem_limit_bytes=VMEM_LIMIT),
        name="mem_kv",
    )(mem, mem_norm_g, w_cat)


def _split_bf16(a):
    hi = a.astype(BF16)
    return hi, (a - hi.astype(F32)).astype(BF16)


def _pool_fold_kernel(w_u_ref, w_bd_ref, o_ref):
    b_hi, b_lo = _split_bf16(w_bd_ref[...])
    n = ROW_CHUNK
    for c in range(w_u_ref.shape[0] // n):
        a_hi, a_lo = _split_bf16(w_u_ref[c * n:(c + 1) * n, :])
        o_ref[c * n:(c + 1) * n, :] = (
            _dot(a_hi, b_hi) + _dot(a_hi, b_lo) + _dot(a_lo, b_hi)).astype(BF16)


def _pool_fold(w_u, w_bd):
    return pl.pallas_call(
        _pool_fold_kernel,
        out_shape=jax.ShapeDtypeStruct(w_u.shape, BF16),
        compiler_params=pltpu.CompilerParams(vmem_limit_bytes=VMEM_LIMIT),
        name="pool_fold",
    )(w_u, w_bd)


def _window_sums(a):
    s2 = a + pltpu.roll(a, 1, 0)
    s4 = s2 + pltpu.roll(s2, 2, 0)
    s8 = s4 + pltpu.roll(s4, 4, 0)
    s16 = s8 + pltpu.roll(s8, 8, 0)
    return {2: s2, 4: s4, 8: s8, 16: s16}


def _pooled_mix(ubuf, row0, n, pos0):
    pos = pos0 + jax.lax.broadcasted_iota(jnp.int32, (n, 1), 0)
    inv_cnt = {w: 1.0 / jnp.minimum(pos + 1, w).astype(F32) for w in POOL_WINDOWS}
    lane = jax.lax.broadcasted_iota(jnp.int32, (n, LANES), 1)
    mixed = []
    for c in range(POOL_WIDTH // LANES):
        cols = slice(c * LANES, (c + 1) * LANES)
        sums = _window_sums(ubuf[row0:row0 + POOL_HALO + n, cols])
        lo_grp = (c * LANES) // POOL_GROUP_DIM
        hi_grp = ((c + 1) * LANES - 1) // POOL_GROUP_DIM
        w_lo, w_hi = POOL_WINDOWS[lo_grp], POOL_WINDOWS[hi_grp]
        pooled = sums[w_lo][POOL_HALO:] * inv_cnt[w_lo]
        if hi_grp != lo_grp:
            split = hi_grp * POOL_GROUP_DIM - c * LANES
            pooled = jnp.where(lane < split, pooled, sums[w_hi][POOL_HALO:] * inv_cnt[w_hi])
        mixed.append(pooled - ubuf[POOL_HALO + row0:POOL_HALO + row0 + n, cols])
    return jnp.concatenate(mixed, axis=-1)


def _pool_layer_kernel(x_ref, g_ref, w_u_ref, w_mq_ref, w_gate_ref, scale_ref,
                       kbd_ref, vbd_ref, w_out_ref, o_ref, ubuf):
    j = pl.program_id(1)
    t = x_ref.shape[1]
    n = ROW_CHUNK
    chunks = range(t // n)

    @pl.when(j == 0)
    def _():
        ubuf[0:POOL_HALO, :] = jnp.zeros((POOL_HALO, POOL_WIDTH), F32)

    @pl.when(j > 0)
    def _():
        ubuf[0:POOL_HALO, :] = ubuf[t:t + POOL_HALO, :]

    xs = [x_ref[0, c * n:(c + 1) * n, :] for c in chunks]
    hns = [_rms_norm(x, g_ref[...]).astype(BF16) for x in xs]
    for c in chunks:
        ubuf[POOL_HALO + c * n:POOL_HALO + (c + 1) * n, :] = _dot(hns[c], w_u_ref[...])
    mqs = [_dot(hn, w_mq_ref[...]) for hn in hns]
    gates = [_silu(_dot(hn, w_gate_ref[...])) for hn in hns]
    toks = [_pooled_mix(ubuf, c * n, n, j * t + c * n) * scale_ref[...] for c in chunks]
    mem_os = [_memory_attention(mq, kbd_ref[0, 0], vbd_ref[0, 0]) for mq in mqs]
    for c in chunks:
        y = _dot((toks[c] * gates[c][:, :POOL_WIDTH]).astype(BF16), w_out_ref[:POOL_WIDTH, :])
        y += _dot((mem_os[c] * gates[c][:, POOL_WIDTH:]).astype(BF16), w_out_ref[POOL_WIDTH:, :])
        o_ref[0, c * n:(c + 1) * n, :] = xs[c] + y


def _pool_layer(x, g, w_u, w_mq, w_gate, scale, kbd, vbd, w_out):
    b, s, _ = x.shape
    t = SEQ_TILE
    const = lambda shape: pl.BlockSpec(shape, lambda i, j: (0,) * len(shape))
    return pl.pallas_call(
        _pool_layer_kernel,
        grid=(b, s // t),
        in_specs=[
            pl.BlockSpec((1, t, D_MODEL), lambda i, j: (i, j, 0)),
            const(g.shape), const(w_u.shape), const(w_mq.shape), const(w_gate.shape),
            const(scale.shape),
            pl.BlockSpec((1, 1) + kbd.shape[2:], lambda i, j: (i, 0, 0, 0)),
            pl.BlockSpec((1, 1) + vbd.shape[2:], lambda i, j: (i, 0, 0, 0)),
            const(w_out.shape),
        ],
        out_specs=pl.BlockSpec((1, t, D_MODEL), lambda i, j: (i, j, 0)),
        out_shape=jax.ShapeDtypeStruct(x.shape, F32),
        scratch_shapes=[pltpu.VMEM((POOL_HALO + t, POOL_WIDTH), F32)],
        compiler_params=pltpu.CompilerParams(
            dimension_semantics=("parallel", "arbitrary"), vmem_limit_bytes=VMEM_LIMIT),
        name="pool_layer",
    )(x, g, w_u, w_mq, w_gate, scale, kbd, vbd, w_out)


def _rope_t(x, cos_t, sin_t):
    half = QK_ROPE // 2
    x1, x2 = x[:half], x[half:]
    return x1 * cos_t - x2 * sin_t, x2 * cos_t + x1 * sin_t


def _mla_pre_kernel(h_ref, g_ref, pos_ref, inv_freq_ref, w_cq_t_ref, w_ckv_ref,
                    w_kr_t_ref, w_mq_ref, w_gate_ref, qg_ref, w_uq_t_ref, kvg_ref, w_uk_ref,
                    w_uv_t_ref, kbd_ref, vbd_ref, w_out_mem_ref,
                    q_ref, k_ref, v_ref, sg_ref, hp_ref):
    t = h_ref.shape[1]
    n = ROW_CHUNK
    chunks = range(t // n)
    heads = range(MLA_HEADS)
    half = QK_ROPE // 2
    rows = [slice(c * n, (c + 1) * n) for c in chunks]

    hs = [h_ref[0, rows[c], :] for c in chunks]
    xns = [_rms_norm(h, g_ref[...]) for h in hs]
    hns = [xn.astype(BF16) for xn in xns]
    mqs = [_dot(hn, w_mq_ref[...]) for hn in hns]
    mem_os = [_memory_attention(mq, kbd_ref[0, 0], vbd_ref[0, 0]) for mq in mqs]
    gates = [_silu(_dot(hn, w_gate_ref[...])) for hn in hns]
    for c in chunks:
        sg_ref[0, rows[c], :] = gates[c][:, :MLA_WIDTH]
        hp_ref[0, rows[c], :] = hs[c] + _dot(
            (mem_os[c] * gates[c][:, MLA_WIDTH:]).astype(BF16), w_out_mem_ref[...])

    hn_ts = [xn.T.astype(BF16) for xn in xns]
    cq_ts = [_dot(w_cq_t_ref[...], hn_t) for hn_t in hn_ts]
    kr_ts = [_dot(w_kr_t_ref[...], hn_t) for hn_t in hn_ts]
    ckvns = [_rms_norm(_dot(hn, w_ckv_ref[...]), kvg_ref[...]).astype(BF16) for hn in hns]
    angs =[pos_ref[0, :, rows[c]].astype(F32) * inv_freq_ref[...] for c in chunks]
    cos_ts, sin_ts = [jnp.cos(a) for a in angs], [jnp.sin(a) for a in angs]
    cqn_ts = [(cq * jax.lax.rsqrt(jnp.mean(cq * cq, axis=0, keepdims=True) + EPS)
               * qg_ref[...]).astype(BF16) for cq in cq_ts]

    q_ts = [_dot(w_uq_t_ref[...], cqn_t) * Q_SCALE for cqn_t in cqn_ts]
    for c in chunks:
        for hd in heads:
            base = hd * QK_DIM
            r1, r2 = _rope_t(q_ts[c][base + QK_NOPE:base + QK_DIM], cos_ts[c], sin_ts[c])
            q_ref[0, hd, 0:QK_NOPE, rows[c]] = q_ts[c][base:base + QK_NOPE].astype(BF16)
            q_ref[0, hd, QK_NOPE:QK_NOPE + half, rows[c]] = r1.astype(BF16)
            q_ref[0, hd, QK_NOPE + half:QK_DIM, rows[c]] = r2.astype(BF16)

    k_nopes = [_dot(ckvn, w_uk_ref[...]) for ckvn in ckvns]
    v_ts = [_dot_nt(w_uv_t_ref[...], ckvn) for ckvn in ckvns]
    for c in chunks:
        r1, r2 = _rope_t(kr_ts[c], cos_ts[c], sin_ts[c])
        k_rope = jnp.concatenate([r1, r2, jnp.zeros((LANES - QK_ROPE, n), F32)], axis=0).T
        k_rope = k_rope[:, :QK_ROPE]
        kt, off = divmod(c * n, ATTN_TILE)
        for hd in heads:
            k_ref[0, hd, rows[c], :] = jnp.concatenate(
                [k_nopes[c][:, hd * QK_NOPE:(hd + 1) * QK_NOPE], k_rope], axis=-1).astype(BF16)
            v_ref[0, hd, kt, :, off:off + n] = v_ts[c][hd * V_HEAD:(hd + 1) * V_HEAD].astype(BF16)


def _mla_pre(h, g, pos, inv_freq, w_cq_t, w_ckv, w_kr_t, w_mq, w_gate, qg, w_uq_t, kvg,
             w_uk, w_uv_t, kbd, vbd, w_out_mem):
    b, s, _ = h.shape
    t = SEQ_TILE
    n_kt = t // ATTN_TILE
    const = lambda shape: pl.BlockSpec(shape, lambda i, j: (0,) * len(shape))
    row = lambda width: pl.BlockSpec((1, t, width), lambda i, j: (i, j, 0))
    return pl.pallas_call(
        _mla_pre_kernel,
        grid=(b, s // t),
        in_specs=[
            row(D_MODEL), const(g.shape),
            pl.BlockSpec((1, 1, t), lambda i, j: (i, 0, j)), const(inv_freq.shape),
            const(w_cq_t.shape), const(w_ckv.shape), const(w_kr_t.shape), const(w_mq.shape),
            const(w_gate.shape), const(qg.shape), const(w_uq_t.shape), const(kvg.shape),
            const(w_uk.shape), const(w_uv_t.shape),
            pl.BlockSpec((1, 1) + kbd.shape[2:], lambda i, j: (i, 1, 0, 0)),
            pl.BlockSpec((1, 1) + vbd.shape[2:], lambda i, j: (i, 1, 0, 0)),
            const(w_out_mem.shape),
        ],
        out_specs=[
            pl.BlockSpec((1, MLA_HEADS, QK_DIM, t), lambda i, j: (i, 0, 0, j)),
            pl.BlockSpec((1, MLA_HEADS, t, QK_DIM), lambda i, j: (i, 0, j, 0)),
            pl.BlockSpec((1, MLA_HEADS, n_kt, V_HEAD, ATTN_TILE), lambda i, j: (i, 0, j, 0, 0)),
            row(MLA_WIDTH), row(D_MODEL),
        ],
        out_shape=[
            jax.ShapeDtypeStruct((b, MLA_HEADS, QK_DIM, s), BF16),
            jax.ShapeDtypeStruct((b, MLA_HEADS, s, QK_DIM), BF16),
            jax.ShapeDtypeStruct((b, MLA_HEADS, s // ATTN_TILE, V_HEAD, ATTN_TILE), BF16),
            jax.ShapeDtypeStruct((b, s, MLA_WIDTH), F32),
            jax.ShapeDtypeStruct((b, s, D_MODEL), F32),
        ],
        compiler_params=pltpu.CompilerParams(
            dimension_semantics=("parallel", "parallel"), vmem_limit_bytes=VMEM_LIMIT),
        name="mla_pre",
    )(h, g, pos, inv_freq, w_cq_t, w_ckv, w_kr_t, w_mq, w_gate, qg, w_uq_t, kvg, w_uk,
      w_uv_t, kbd, vbd, w_out_mem)


def _sublane_allreduce(x, op):
    for shift in (4, 2, 1):
        x = op(x, pltpu.roll(x, shift, 0))
    return x


def _mla_attn_kernel(q_ref, k_ref, v_ref, sg_ref, hp_ref, w_out_ref, g_ref, o_ref,
                     acc_sc, tok_sc):
    i = pl.program_id(1)
    tq = q_ref.shape[3]
    tk = v_ref.shape[4]
    half = tk // 2

    acc_sc[...] = jnp.zeros(acc_sc.shape, F32)
    heads = range(MLA_HEADS)

    def step(jt, carry, k0, kn, q0, qn, masked):
        m_old, l_old = carry
        start = pl.multiple_of(jt * tk, tk) + k0
        qs = slice(q0, q0 + qn)
        s = [_dot(k_ref[0, hd, pl.ds(start, kn), :], q_ref[0, hd, :, qs]) for hd in heads]
        if masked:
            k_chunk = (k0 + jax.lax.broadcasted_iota(jnp.int32, (kn, qn), 0)) // CHUNK
            q_chunk = (q0 + jax.lax.broadcasted_iota(jnp.int32, (kn, qn), 1)) // CHUNK
            mask = k_chunk <= q_chunk
        m_new, l_new, alpha, p = [], [], [], []
        for hd in heads:
            sh = jnp.where(mask, s[hd], NEG_INF) if masked else s[hd]
            s3 = sh.reshape(kn // SUBLANES, SUBLANES, qn)
            m_prev, l_prev = m_old[hd][:, qs], l_old[hd][:, qs]
            m = jnp.maximum(m_prev, _sublane_allreduce(jnp.max(s3, axis=0), jnp.maximum))
            a = jnp.exp2(m_prev - m)
            p3 = jnp.exp2(s3 - m[None])
            l = a * l_prev + jnp.sum(p3, axis=0)
            if qn != tq:
                m, l = (jnp.concatenate(
                    ([old[:, :q0]] if q0 else []) + [new]
                    + ([old[:, q0 + qn:]] if q0 + qn < tq else []), axis=1)
                    for old, new in ((m_old[hd], m), (l_old[hd], l)))
            m_new.append(m)
            l_new.append(l)
            alpha.append(a)
            p.append(p3.reshape(kn, qn).astype(BF16))
        pv = [_dot(v_ref[0, hd, jt, :, k0:k0 + kn], p[hd]) for hd in heads]
        for hd in heads:
            acc = acc_sc[hd, :, qs].reshape(V_HEAD // SUBLANES, SUBLANES, qn)
            acc_sc[hd, :, qs] = (alpha[hd][None] * acc).reshape(V_HEAD, qn) + pv[hd]
        return tuple(m_new), tuple(l_new)

    init = (tuple(jnp.full((SUBLANES, tq), NEG_INF, F32) for _ in heads),
            tuple(jnp.zeros((SUBLANES, tq), F32) for _ in heads))
    carry = jax.lax.fori_loop(0, i, lambda jt, c: step(jt, c, 0, tk, 0, tq, False), init)
    carry = step(i, carry, 0, half, 0, tq, True)
    _, l_fin = step(i, carry, half, tk - half, half, tq - half, True)

    for hd in heads:
        l_tot = _sublane_allreduce(l_fin[hd], jnp.add)
        o_t = acc_sc[hd].reshape(V_HEAD // SUBLANES, SUBLANES, tq) / l_tot[None]
        tok_sc[:, hd * V_HEAD:(hd + 1) * V_HEAD] = o_t.reshape(V_HEAD, tq).T

    rows = [slice(c * ROW_CHUNK, (c + 1) * ROW_CHUNK) for c in range(tq // ROW_CHUNK)]
    branches = [(tok_sc[r, :] * sg_ref[0, r, :]).astype(BF16) for r in rows]
    outs = [hp_ref[0, r, :] + _dot(br, w_out_ref[...]) for r, br in zip(rows, branches)]
    for r, out in zip(rows, outs):
        o_ref[0, r, :] = _rms_norm(out, g_ref[...])


def _mla_attn(q_t, k, v_t, sg, hp, w_out_tok, final_g):
    b, _, s, _ = k.shape
    t = ATTN_TILE
    const = lambda shape: pl.BlockSpec(shape, lambda i, j: (0,) * len(shape))
    row = lambda width: pl.BlockSpec((1, t, width), lambda i, j: (i, j, 0))
    return pl.pallas_call(
        _mla_attn_kernel,
        grid=(b, s // t),
        in_specs=[
            pl.BlockSpec((1, MLA_HEADS, QK_DIM, t), lambda i, j: (i, 0, 0, j)),
            pl.BlockSpec((1, MLA_HEADS, s, QK_DIM), lambda i, j: (i, 0, 0, 0)),
            pl.BlockSpec((1,) + v_t.shape[1:], lambda i, j: (i, 0, 0, 0, 0)),
            row(MLA_WIDTH), row(D_MODEL), const(w_out_tok.shape), const(final_g.shape),
        ],
        out_specs=row(D_MODEL),
        out_shape=jax.ShapeDtypeStruct((b, s, D_MODEL), F32),
        scratch_shapes=[
            pltpu.VMEM((MLA_HEADS, V_HEAD, t), F32),
            pltpu.VMEM((t, MLA_WIDTH), F32),
        ],
        compiler_params=pltpu.CompilerParams(
            dimension_semantics=("parallel", "arbitrary"), vmem_limit_bytes=VMEM_LIMIT),
        name="mla_attn",
    )(q_t, k, v_t, sg, hp, w_out_tok, final_g)


def _block_diag(w_group):
    g, c, d = w_group.shape
    out = jnp.zeros((g * c, g * d), w_group.dtype)
    for i in range(g):
        out = out.at[i * c:(i + 1) * c, i * d:(i + 1) * d].set(w_group[i])
    return out


def kernel(x, mem, positions, norm_g, mem_norm_g, w_mem_kv, w_out, pool_w_in, pool_w_group,
           pool_scale, mla_w_in, mla_q_norm_g, mla_w_uq, mla_kv_norm_g, mla_w_ukv, final_norm_g):
    depth = w_mem_kv.shape[0]
    assert depth == 2 and pool_w_in.shape[0] == 1 and mla_w_in.shape[0] == 1
    bf = lambda a: a.astype(BF16)
    row = lambda a: a.reshape(1, -1)

    inv_freq = ROPE_THETA ** (-(jnp.arange(0, QK_ROPE, 2, dtype=F32) / QK_ROPE))

    w_mem_cat = bf(jnp.concatenate([w_mem_kv[i] for i in range(depth)], axis=1))
    kbd, vbd = _mem_kv(mem, row(mem_norm_g), w_mem_cat)

    pw = pool_w_in[0]
    h1 = _pool_layer(
        x, row(norm_g[0]),
        _pool_fold(pw[:, :POOL_WIDTH], _block_diag(pool_w_group[0])),
        bf(pw[:, POOL_WIDTH:POOL_WIDTH + MEM_WIDTH]), bf(pw[:, POOL_WIDTH + MEM_WIDTH:]),
        row(pool_scale[0]), kbd, vbd, bf(w_out[0]))

    mw = mla_w_in[0]
    o_ckv = Q_LORA
    o_kr = o_ckv + KV_LORA
    o_mq = o_kr + QK_ROPE
    o_gate = o_mq + MEM_WIDTH
    w_ukv = mla_w_ukv[0].reshape(KV_LORA, MLA_HEADS, QK_NOPE + V_HEAD)
    w_uk = w_ukv[:, :, :QK_NOPE].reshape(KV_LORA, MLA_HEADS * QK_NOPE)
    w_uv_t = w_ukv[:, :, QK_NOPE:].reshape(KV_LORA, MLA_WIDTH).T
    q_t, k, v_t, sg, hp = _mla_pre(
        h1, row(norm_g[1]), positions[:, None, :], inv_freq[:, None],
        bf(mw[:, :o_ckv].T), bf(mw[:, o_ckv:o_kr]), bf(mw[:, o_kr:o_mq].T), bf(mw[:, o_mq:o_gate]),
        bf(mw[:, o_gate:]), mla_q_norm_g[0][:, None], bf(mla_w_uq[0].T), row(mla_kv_norm_g[0]),
        bf(w_uk), bf(w_uv_t), kbd, vbd, bf(w_out[1, MLA_WIDTH:]))

    return _mla_attn(q_t, k, v_t, sg, hp, bf(w_out[1, :MLA_WIDTH]), row(final_norm_g))
```

```python
import math

import jax
import jax.numpy as jnp
from jax.experimental import pallas as pl
from jax.experimental.pallas import tpu as pltpu

D_MODEL = 1024
CHUNK = 64
EPS = 1e-6
NEG_INF = -1e30

N_MEM = 256
MEM_HEADS = 4
MEM_HEAD_DIM = 64
MEM_WIDTH = MEM_HEADS * MEM_HEAD_DIM

POOL_WINDOWS = (2, 4, 8, 16)
POOL_GROUP_DIM = 192
POOL_WIDTH = len(POOL_WINDOWS) * POOL_GROUP_DIM
POOL_HALO = 16

MLA_HEADS = 6
QK_NOPE = 128
QK_ROPE = 64
QK_DIM = QK_NOPE + QK_ROPE
V_HEAD = 128
Q_LORA = 384
KV_LORA = 256
MLA_WIDTH = MLA_HEADS * V_HEAD
ROPE_THETA = 10000.0
Q_SCALE = QK_DIM ** -0.5 * math.log2(math.e)

MIX_WIDTH = POOL_WIDTH + MEM_WIDTH

LANES = 128
SUBLANES = 8
SEQ_TILE = 1024
ROW_CHUNK = 256
ATTN_TILE = 512
VMEM_LIMIT = 56 * 1024 * 1024

BF16 = jnp.bfloat16
F32 = jnp.float32


def _rms_norm(x, g):
    ms = jnp.mean(x * x, axis=-1, keepdims=True)
    return x * jax.lax.rsqrt(ms + EPS) * g


def _silu(x):
    return x / (1.0 + jnp.exp(-x))


def _dot(a, b):
    return jnp.dot(a, b, preferred_element_type=F32)


def _dot_nt(a, b):
    return jax.lax.dot_general(a, b, (((1,), (1,)), ((), ())), preferred_element_type=F32)


def _memory_attention(mq, kbd, vbd):
    s = _dot(mq.astype(BF16), kbd) * (MEM_HEAD_DIM ** -0.5)
    probs = []
    for h in range(MEM_HEADS):
        sh = s[:, h * N_MEM:(h + 1) * N_MEM]
        e = jnp.exp(sh - jnp.max(sh, axis=-1, keepdims=True))
        probs.append((e / jnp.sum(e, axis=-1, keepdims=True)).astype(BF16))
    return _dot(jnp.concatenate(probs, axis=-1), vbd)


def _mem_kv_kernel(mem_ref, g_ref, w_ref, kbd_ref, vbd_ref):
    mem_n = _rms_norm(mem_ref[0], g_ref[...]).astype(BF16)
    kv = _dot(mem_n, w_ref[...])
    n_layers = kv.shape[1] // (2 * MEM_WIDTH)
    row_head = jax.lax.broadcasted_iota(jnp.int32, (MEM_WIDTH, N_MEM), 0) // MEM_HEAD_DIM
    col_head = jax.lax.broadcasted_iota(jnp.int32, (N_MEM, MEM_WIDTH), 1) // MEM_HEAD_DIM
    for layer in range(n_layers):
        base = layer * 2 * MEM_WIDTH
        k_t = kv[:, base:base + MEM_WIDTH].T
        v = kv[:, base + MEM_WIDTH:base + 2 * MEM_WIDTH]
        for h in range(MEM_HEADS):
            kbd_ref[0, layer, :, h * N_MEM:(h + 1) * N_MEM] = jnp.where(
                row_head == h, k_t, 0.0).astype(BF16)
            vbd_ref[0, layer, h * N_MEM:(h + 1) * N_MEM, :] = jnp.where(
                col_head == h, v, 0.0).astype(BF16)


def _mem_kv(mem, mem_norm_g, w_cat):
    b = mem.shape[0]
    n_layers = w_cat.shape[1] // (2 * MEM_WIDTH)
    return pl.pallas_call(
        _mem_kv_kernel,
        grid=(b,),
        in_specs=[
            pl.BlockSpec((1, N_MEM, D_MODEL), lambda i: (i, 0, 0)),
            pl.BlockSpec((1, D_MODEL), lambda i: (0, 0)),
            pl.BlockSpec(w_cat.shape, lambda i: (0, 0)),
        ],
        out_specs=[
            pl.BlockSpec((1, n_layers, MEM_WIDTH, MEM_HEADS * N_MEM), lambda i: (i, 0, 0, 0)),
            pl.BlockSpec((1, n_layers, MEM_HEADS * N_MEM, MEM_WIDTH), lambda i: (i, 0, 0, 0)),
        ],
        out_shape=[
            jax.ShapeDtypeStruct((b, n_layers, MEM_WIDTH, MEM_HEADS * N_MEM), BF16),
            jax.ShapeDtypeStruct((b, n_layers, MEM_HEADS * N_MEM, MEM_WIDTH), BF16),
        ],
        compiler_params=pltpu.CompilerParams(
            dimension_semantics=("parallel",), vmem_limit_bytes=VMEM_LIMIT),
        name="mem_kv",
    )(mem, mem_norm_g, w_cat)


def _split_bf16(a):
    hi = a.astype(BF16)
    return hi, (a - hi.astype(F32)).astype(BF16)


def _pool_fold_kernel(w_u_ref, w_bd_ref, o_ref):
    b_hi, b_lo = _split_bf16(w_bd_ref[...])
    n = ROW_CHUNK
    for c in range(w_u_ref.shape[0] // n):
        a_hi, a_lo = _split_bf16(w_u_ref[c * n:(c + 1) * n, :])
        o_ref[c * n:(c + 1) * n, :] = (
            _dot(a_hi, b_hi) + _dot(a_hi, b_lo) + _dot(a_lo, b_hi)).astype(BF16)


def _pool_fold(w_u, w_bd):
    return pl.pallas_call(
        _pool_fold_kernel,
        out_shape=jax.ShapeDtypeStruct(w_u.shape, BF16),
        compiler_params=pltpu.CompilerParams(vmem_limit_bytes=VMEM_LIMIT),
        name="pool_fold",
    )(w_u, w_bd)


def _window_sums(a):
    s2 = a + pltpu.roll(a, 1, 0)
    s4 = s2 + pltpu.roll(s2, 2, 0)
    s8 = s4 + pltpu.roll(s4, 4, 0)
    s16 = s8 + pltpu.roll(s8, 8, 0)
    return {2: s2, 4: s4, 8: s8, 16: s16}


def _pooled_mix(ubuf, row0, n, pos0):
    pos = pos0 + jax.lax.broadcasted_iota(jnp.int32, (n, 1), 0)
    inv_cnt = {w: 1.0 / jnp.minimum(pos + 1, w).astype(F32) for w in POOL_WINDOWS}
    lane = jax.lax.broadcasted_iota(jnp.int32, (n, LANES), 1)
    mixed = []
    for c in range(POOL_WIDTH // LANES):
        cols = slice(c * LANES, (c + 1) * LANES)
        sums = _window_sums(ubuf[row0:row0 + POOL_HALO + n, cols])
        lo_grp = (c * LANES) // POOL_GROUP_DIM
        hi_grp = ((c + 1) * LANES - 1) // POOL_GROUP_DIM
        w_lo, w_hi = POOL_WINDOWS[lo_grp], POOL_WINDOWS[hi_grp]
        pooled = sums[w_lo][POOL_HALO:] * inv_cnt[w_lo]
        if hi_grp != lo_grp:
            split = hi_grp * POOL_GROUP_DIM - c * LANES
            pooled = jnp.where(lane < split, pooled, sums[w_hi][POOL_HALO:] * inv_cnt[w_hi])
        mixed.append(pooled - ubuf[POOL_HALO + row0:POOL_HALO + row0 + n, cols])
    return jnp.concatenate(mixed, axis=-1)


def _pool_layer_kernel(x_ref, g_ref, w_u_ref, w_mq_ref, w_gate_ref, scale_ref,
                       kbd_ref, vbd_ref, w_out_ref, o_ref, ubuf):
    j = pl.program_id(1)
    t = x_ref.shape[1]
    n = ROW_CHUNK
    chunks = range(t // n)

    @pl.when(j == 0)
    def _():
        ubuf[0:POOL_HALO, :] = jnp.zeros((POOL_HALO, POOL_WIDTH), F32)

    @pl.when(j > 0)
    def _():
        ubuf[0:POOL_HALO, :] = ubuf[t:t + POOL_HALO, :]

    xs = [x_ref[0, c * n:(c + 1) * n, :] for c in chunks]
    hns = [_rms_norm(x, g_ref[...]).astype(BF16) for x in xs]
    for c in chunks:
        ubuf[POOL_HALO + c * n:POOL_HALO + (c + 1) * n, :] = _dot(hns[c], w_u_ref[...])
    mqs = [_dot(hn, w_mq_ref[...]) for hn in hns]
    gates = [_silu(_dot(hn, w_gate_ref[...])) for hn in hns]
    toks = [_pooled_mix(ubuf, c * n, n, j * t + c * n) * scale_ref[...] for c in chunks]
    mem_os = [_memory_attention(mq, kbd_ref[0, 0], vbd_ref[0, 0]) for mq in mqs]
    for c in chunks:
        y = _dot((toks[c] * gates[c][:, :POOL_WIDTH]).astype(BF16), w_out_ref[:POOL_WIDTH, :])
        y += _dot((mem_os[c] * gates[c][:, POOL_WIDTH:]).astype(BF16), w_out_ref[POOL_WIDTH:, :])
        o_ref[0, c * n:(c + 1) * n, :] = xs[c] + y


def _pool_layer(x, g, w_u, w_mq, w_gate, scale, kbd, vbd, w_out):
    b, s, _ = x.shape
    t = SEQ_TILE
    const = lambda shape: pl.BlockSpec(shape, lambda i, j: (0,) * len(shape))
    return pl.pallas_call(
        _pool_layer_kernel,
        grid=(b, s // t),
        in_specs=[
            pl.BlockSpec((1, t, D_MODEL), lambda i, j: (i, j, 0)),
            const(g.shape), const(w_u.shape), const(w_mq.shape), const(w_gate.shape),
            const(scale.shape),
            pl.BlockSpec((1, 1) + kbd.shape[2:], lambda i, j: (i, 0, 0, 0)),
            pl.BlockSpec((1, 1) + vbd.shape[2:], lambda i, j: (i, 0, 0, 0)),
            const(w_out.shape),
        ],
        out_specs=pl.BlockSpec((1, t, D_MODEL), lambda i, j: (i, j, 0)),
        out_shape=jax.ShapeDtypeStruct(x.shape, F32),
        scratch_shapes=[pltpu.VMEM((POOL_HALO + t, POOL_WIDTH), F32)],
        compiler_params=pltpu.CompilerParams(
            dimension_semantics=("parallel", "arbitrary"), vmem_limit_bytes=VMEM_LIMIT),
        name="pool_layer",
    )(x, g, w_u, w_mq, w_gate, scale, kbd, vbd, w_out)


def _rope_t(x, cos_t, sin_t):
    half = QK_ROPE // 2
    x1, x2 = x[:half], x[half:]
    return x1 * cos_t - x2 * sin_t, x2 * cos_t + x1 * sin_t


def _mla_pre_kernel(h_ref, g_ref, pos_ref, inv_freq_ref, w_cq_t_ref, w_ckv_ref,
                    w_kr_t_ref, w_mq_ref, w_gate_ref, qg_ref, w_uq_t_ref, kvg_ref, w_uk_ref,
                    w_uv_t_ref, kbd_ref, vbd_ref, w_out_mem_ref,
                    q_ref, k_ref, v_ref, sg_ref, hp_ref):
    t = h_ref.shape[1]
    n = ROW_CHUNK
    chunks = range(t // n)
    heads = range(MLA_HEADS)
    half = QK_ROPE // 2
    rows = [slice(c * n, (c + 1) * n) for c in chunks]

    hs = [h_ref[0, rows[c], :] for c in chunks]
    xns = [_rms_norm(h, g_ref[...]) for h in hs]
    hns = [xn.astype(BF16) for xn in xns]
    mqs = [_dot(hn, w_mq_ref[...]) for hn in hns]
    mem_os = [_memory_attention(mq, kbd_ref[0, 0], vbd_ref[0, 0]) for mq in mqs]
    gates = [_silu(_dot(hn, w_gate_ref[...])) for hn in hns]
    for c in chunks:
        sg_ref[0, rows[c], :] = gates[c][:, :MLA_WIDTH]
        hp_ref[0, rows[c], :] = hs[c] + _dot(
            (mem_os[c] * gates[c][:, MLA_WIDTH:]).astype(BF16), w_out_mem_ref[...])

    hn_ts = [xn.T.astype(BF16) for xn in xns]
    cq_ts = [_dot(w_cq_t_ref[...], hn_t) for hn_t in hn_ts]
    kr_ts = [_dot(w_kr_t_ref[...], hn_t) for hn_t in hn_ts]
    ckvns = [_rms_norm(_dot(hn, w_ckv_ref[...]), kvg_ref[...]).astype(BF16) for hn in hns]
    angs =[pos_ref[0, :, rows[c]].astype(F32) * inv_freq_ref[...] for c in chunks]
    cos_ts, sin_ts = [jnp.cos(a) for a in angs], [jnp.sin(a) for a in angs]
    cqn_ts = [(cq * jax.lax.rsqrt(jnp.mean(cq * cq, axis=0, keepdims=True) + EPS)
               * qg_ref[...]).astype(BF16) for cq in cq_ts]

    q_ts = [_dot(w_uq_t_ref[...], cqn_t) * Q_SCALE for cqn_t in cqn_ts]
    for c in chunks:
        for hd in heads:
            base = hd * QK_DIM
            r1, r2 = _rope_t(q_ts[c][base + QK_NOPE:base + QK_DIM], cos_ts[c], sin_ts[c])
            q_ref[0, hd, 0:QK_NOPE, rows[c]] = q_ts[c][base:base + QK_NOPE].astype(BF16)
            q_ref[0, hd, QK_NOPE:QK_NOPE + half, rows[c]] = r1.astype(BF16)
            q_ref[0, hd, QK_NOPE + half:QK_DIM, rows[c]] = r2.astype(BF16)

    k_nopes = [_dot(ckvn, w_uk_ref[...]) for ckvn in ckvns]
    v_ts = [_dot_nt(w_uv_t_ref[...], ckvn) for ckvn in ckvns]
    for c in chunks:
        r1, r2 = _rope_t(kr_ts[c], cos_ts[c], sin_ts[c])
        k_rope = jnp.concatenate([r1, r2, jnp.zeros((LANES - QK_ROPE, n), F32)], axis=0).T
        k_rope = k_rope[:, :QK_ROPE]
        kt, off = divmod(c * n, ATTN_TILE)
        for hd in heads:
            k_ref[0, hd, rows[c], :] = jnp.concatenate(
                [k_nopes[c][:, hd * QK_NOPE:(hd + 1) * QK_NOPE], k_rope], axis=-1).astype(BF16)
            v_ref[0, hd, kt, :, off:off + n] = v_ts[c][hd * V_HEAD:(hd + 1) * V_HEAD].astype(BF16)


def _mla_pre(h, g, pos, inv_freq, w_cq_t, w_ckv, w_kr_t, w_mq, w_gate, qg, w_uq_t, kvg,
             w_uk, w_uv_t, kbd, vbd, w_out_mem):
    b, s, _ = h.shape
    t = SEQ_TILE
    n_kt = t // ATTN_TILE
    const = lambda shape: pl.BlockSpec(shape, lambda i, j: (0,) * len(shape))
    row = lambda width: pl.BlockSpec((1, t, width), lambda i, j: (i, j, 0))
    return pl.pallas_call(
        _mla_pre_kernel,
        grid=(b, s // t),
        in_specs=[
            row(D_MODEL), const(g.shape),
            pl.BlockSpec((1, 1, t), lambda i, j: (i, 0, j)), const(inv_freq.shape),
            const(w_cq_t.shape), const(w_ckv.shape), const(w_kr_t.shape), const(w_mq.shape),
            const(w_gate.shape), const(qg.shape), const(w_uq_t.shape), const(kvg.shape),
            const(w_uk.shape), const(w_uv_t.shape),
            pl.BlockSpec((1, 1) + kbd.shape[2:], lambda i, j: (i, 1, 0, 0)),
            pl.BlockSpec((1, 1) + vbd.shape[2:], lambda i, j: (i, 1, 0, 0)),
            const(w_out_mem.shape),
        ],
        out_specs=[
            pl.BlockSpec((1, MLA_HEADS, QK_DIM, t), lambda i, j: (i, 0, 0, j)),
            pl.BlockSpec((1, MLA_HEADS, t, QK_DIM), lambda i, j: (i, 0, j, 0)),
            pl.BlockSpec((1, MLA_HEADS, n_kt, V_HEAD, ATTN_TILE), lambda i, j: (i, 0, j, 0, 0)),
            row(MLA_WIDTH), row(D_MODEL),
        ],
        out_shape=[
            jax.ShapeDtypeStruct((b, MLA_HEADS, QK_DIM, s), BF16),
            jax.ShapeDtypeStruct((b, MLA_HEADS, s, QK_DIM), BF16),
            jax.ShapeDtypeStruct((b, MLA_HEADS, s // ATTN_TILE, V_HEAD, ATTN_TILE), BF16),
            jax.ShapeDtypeStruct((b, s, MLA_WIDTH), F32),
            jax.ShapeDtypeStruct((b, s, D_MODEL), F32),
        ],
        compiler_params=pltpu.CompilerParams(
            dimension_semantics=("parallel", "parallel"), vmem_limit_bytes=VMEM_LIMIT),
        name="mla_pre",
    )(h, g, pos, inv_freq, w_cq_t, w_ckv, w_kr_t, w_mq, w_gate, qg, w_uq_t, kvg, w_uk,
      w_uv_t, kbd, vbd, w_out_mem)


def _sublane_allreduce(x, op):
    for shift in (4, 2, 1):
        x = op(x, pltpu.roll(x, shift, 0))
    return x


def _mla_attn_kernel(q_ref, k_ref, v_ref, sg_ref, hp_ref, w_out_ref, g_ref, o_ref,
                     acc_sc, tok_sc):
    i = pl.program_id(1)
    tq = q_ref.shape[3]
    tk = v_ref.shape[4]
    half = tk // 2

    acc_sc[...] = jnp.zeros(acc_sc.shape, F32)
    heads = range(MLA_HEADS)

    def step(jt, carry, k0, kn, q0, qn, masked):
        m_old, l_old = carry
        start = pl.multiple_of(jt * tk, tk) + k0
        qs = slice(q0, q0 + qn)
        s = [_dot(k_ref[0, hd, pl.ds(start, kn), :], q_ref[0, hd, :, qs]) for hd in heads]
        if masked:
            k_chunk = (k0 + jax.lax.broadcasted_iota(jnp.int32, (kn, qn), 0)) // CHUNK
            q_chunk = (q0 + jax.lax.broadcasted_iota(jnp.int32, (kn, qn), 1)) // CHUNK
            mask = k_chunk <= q_chunk
        m_new, l_new, alpha, p = [], [], [], []
        for hd in heads:
            sh = jnp.where(mask, s[hd], NEG_INF) if masked else s[hd]
            s3 = sh.reshape(kn // SUBLANES, SUBLANES, qn)
            m_prev, l_prev = m_old[hd][:, qs], l_old[hd][:, qs]
            m = jnp.maximum(m_prev, _sublane_allreduce(jnp.max(s3, axis=0), jnp.maximum))
            a = jnp.exp2(m_prev - m)
            p3 = jnp.exp2(s3 - m[None])
            l = a * l_prev + jnp.sum(p3, axis=0)
            if qn != tq:
                m, l = (jnp.concatenate(
                    ([old[:, :q0]] if q0 else []) + [new]
                    + ([old[:, q0 + qn:]] if q0 + qn < tq else []), axis=1)
                    for old, new in ((m_old[hd], m), (l_old[hd], l)))
            m_new.append(m)
            l_new.append(l)
            alpha.append(a)
            p.append(p3.reshape(kn, qn).astype(BF16))
        pv = [_dot(v_ref[0, hd, jt, :, k0:k0 + kn], p[hd]) for hd in heads]
        for hd in heads:
            acc = acc_sc[hd, :, qs].reshape(V_HEAD // SUBLANES, SUBLANES, qn)
            acc_sc[hd, :, qs] = (alpha[hd][None] * acc).reshape(V_HEAD, qn) + pv[hd]
        return tuple(m_new), tuple(l_new)

    init = (tuple(jnp.full((SUBLANES, tq), NEG_INF, F32) for _ in heads),
            tuple(jnp.zeros((SUBLANES, tq), F32) for _ in heads))
    carry = jax.lax.fori_loop(0, i, lambda jt, c: step(jt, c, 0, tk, 0, tq, False), init)
    carry = step(i, carry, 0, half, 0, tq, True)
    _, l_fin = step(i, carry, half, tk - half, half, tq - half, True)

    for hd in heads:
        l_tot = _sublane_allreduce(l_fin[hd], jnp.add)
        o_t = acc_sc[hd].reshape(V_HEAD // SUBLANES, SUBLANES, tq) / l_tot[None]
        tok_sc[:, hd * V_HEAD:(hd + 1) * V_HEAD] = o_t.reshape(V_HEAD, tq).T

    rows = [slice(c * ROW_CHUNK, (c + 1) * ROW_CHUNK) for c in range(tq // ROW_CHUNK)]
    branches = [(tok_sc[r, :] * sg_ref[0, r, :]).astype(BF16) for r in rows]
    outs = [hp_ref[0, r, :] + _dot(br, w_out_ref[...]) for r, br in zip(rows, branches)]
    for r, out in zip(rows, outs):
        o_ref[0, r, :] = _rms_norm(out, g_ref[...])


def _mla_attn(q_t, k, v_t, sg, hp, w_out_tok, final_g):
    b, _, s, _ = k.shape
    t = ATTN_TILE
    const = lambda shape: pl.BlockSpec(shape, lambda i, j: (0,) * len(shape))
    row = lambda width: pl.BlockSpec((1, t, width), lambda i, j: (i, j, 0))
    return pl.pallas_call(
        _mla_attn_kernel,
        grid=(b, s // t),
        in_specs=[
            pl.BlockSpec((1, MLA_HEADS, QK_DIM, t), lambda i, j: (i, 0, 0, j)),
            pl.BlockSpec((1, MLA_HEADS, s, QK_DIM), lambda i, j: (i, 0, 0, 0)),
            pl.BlockSpec((1,) + v_t.shape[1:], lambda i, j: (i, 0, 0, 0, 0)),
            row(MLA_WIDTH), row(D_MODEL), const(w_out_tok.shape), const(final_g.shape),
        ],
        out_specs=row(D_MODEL),
        out_shape=jax.ShapeDtypeStruct((b, s, D_MODEL), F32),
        scratch_shapes=[
            pltpu.VMEM((MLA_HEADS, V_HEAD, t), F32),
            pltpu.VMEM((t, MLA_WIDTH), F32),
        ],
        compiler_params=pltpu.CompilerParams(
            dimension_semantics=("parallel", "arbitrary"), vmem_limit_bytes=VMEM_LIMIT),
        name="mla_attn",
    )(q_t, k, v_t, sg, hp, w_out_tok, final_g)


def _block_diag(w_group):
    g, c, d = w_group.shape
    out = jnp.zeros((g * c, g * d), w_group.dtype)
    for i in range(g):
        out = out.at[i * c:(i + 1) * c, i * d:(i + 1) * d].set(w_group[i])
    return out


def kernel(x, mem, positions, norm_g, mem_norm_g, w_mem_kv, w_out, pool_w_in, pool_w_group,
           pool_scale, mla_w_in, mla_q_norm_g, mla_w_uq, mla_kv_norm_g, mla_w_ukv, final_norm_g):
    depth = w_mem_kv.shape[0]
    assert depth == 2 and pool_w_in.shape[0] == 1 and mla_w_in.shape[0] == 1
    bf = lambda a: a.astype(BF16)
    row = lambda a: a.reshape(1, -1)

    inv_freq = ROPE_THETA ** (-(jnp.arange(0, QK_ROPE, 2, dtype=F32) / QK_ROPE))

    w_mem_cat = bf(jnp.concatenate([w_mem_kv[i] for i in range(depth)], axis=1))
    kbd, vbd = _mem_kv(mem, row(mem_norm_g), w_mem_cat)

    pw = pool_w_in[0]
    h1 = _pool_layer(
        x, row(norm_g[0]),
        _pool_fold(pw[:, :POOL_WIDTH], _block_diag(pool_w_group[0])),
        bf(pw[:, POOL_WIDTH:POOL_WIDTH + MEM_WIDTH]), bf(pw[:, POOL_WIDTH + MEM_WIDTH:]),
        row(pool_scale[0]), kbd, vbd, bf(w_out[0]))

    mw = mla_w_in[0]
    o_ckv = Q_LORA
    o_kr = o_ckv + KV_LORA
    o_mq = o_kr + QK_ROPE
    o_gate = o_mq + MEM_WIDTH
    w_ukv = mla_w_ukv[0].reshape(KV_LORA, MLA_HEADS, QK_NOPE + V_HEAD)
    w_uk = w_ukv[:, :, :QK_NOPE].reshape(KV_LORA, MLA_HEADS * QK_NOPE)
    w_uv_t = w_ukv[:, :, QK_NOPE:].reshape(KV_LORA, MLA_WIDTH).T
    q_t, k, v_t, sg, hp = _mla_pre(
        h1, row(norm_g[1]), positions[:, None, :], inv_freq[:, None],
        bf(mw[:, :o_ckv].T), bf(mw[:, o_ckv:o_kr]), bf(mw[:, o_kr:o_mq].T), bf(mw[:, o_mq:o_gate]),
        bf(mw[:, o_gate:]), mla_q_norm_g[0][:, None], bf(mla_w_uq[0].T), row(mla_kv_norm_g[0]),
        bf(w_uk), bf(w_uv_t), kbd, vbd, bf(w_out[1, MLA_WIDTH:]))

    return _mla_attn(q_t, k, v_t, sg, hp, bf(w_out[1, :MLA_WIDTH]), row(final_norm_g))
```

```python
import functools
import math

import jax
import jax.numpy as jnp
from jax.experimental import pallas as pl
from jax.experimental.pallas import tpu as pltpu

D_MODEL = 1024
CHUNK = 64
EPS = 1e-6
NEG_INF = -1e30

N_MEM = 256
MEM_HEADS = 4
MEM_HEAD_DIM = 64
MEM_WIDTH = MEM_HEADS * MEM_HEAD_DIM

POOL_WINDOWS = (2, 4, 8, 16)
POOL_GROUP_DIM = 192
POOL_WIDTH = len(POOL_WINDOWS) * POOL_GROUP_DIM
POOL_HALO = 16

MLA_HEADS = 6
QK_NOPE = 128
QK_ROPE = 64
QK_DIM = QK_NOPE + QK_ROPE
V_HEAD = 128
V_EXT = V_HEAD + 16
Q_LORA = 384
KV_LORA = 256
MLA_WIDTH = MLA_HEADS * V_HEAD
ROPE_THETA = 10000.0
LOG2_E = math.log2(math.e)
Q_SCALE = QK_DIM ** -0.5 * LOG2_E
MEM_Q_SCALE = MEM_HEAD_DIM ** -0.5 * LOG2_E

MIX_WIDTH = POOL_WIDTH + MEM_WIDTH

LANES = 128
SUBLANES = 8
SEQ_TILE = 1024
ROW_CHUNK = 256
ATTN_TILE = 512
VMEM_LIMIT = 56 * 1024 * 1024

BF16 = jnp.bfloat16
F32 = jnp.float32


def _rms_scale(x):
    ms = jnp.mean(x * x, axis=-1, keepdims=True)
    return x * jax.lax.rsqrt(ms + EPS)


def _rms_norm(x, g):
    return _rms_scale(x) * g


def _silu(x):
    return x / (1.0 + jnp.exp2(x * -LOG2_E))


def _dot(a, b):
    return jnp.dot(a, b, preferred_element_type=F32)


def _dot_nt(a, b):
    return jax.lax.dot_general(a, b, (((1,), (1,)), ((), ())), preferred_element_type=F32)


def _memory_attention(mq, kbd, vbd):
    s = _dot(mq.astype(BF16), kbd)
    probs = []
    for h in range(MEM_HEADS):
        sh = s[:, h * N_MEM:(h + 1) * N_MEM]
        e = jnp.exp2(sh - jnp.max(sh, axis=-1, keepdims=True))
        probs.append((e / jnp.sum(e, axis=-1, keepdims=True)).astype(BF16))
    return _dot(jnp.concatenate(probs, axis=-1), vbd)


def _mem_kv_kernel(mem_ref, g_ref, w_ref, kbd_ref, vbd_ref):
    mem_n = _rms_norm(mem_ref[0], g_ref[...]).astype(BF16)
    kv = _dot(mem_n, w_ref[...])
    n_layers = kv.shape[1] // (2 * MEM_WIDTH)
    row_head = jax.lax.broadcasted_iota(jnp.int32, (MEM_WIDTH, N_MEM), 0) // MEM_HEAD_DIM
    col_head = jax.lax.broadcasted_iota(jnp.int32, (N_MEM, MEM_WIDTH), 1) // MEM_HEAD_DIM
    for layer in range(n_layers):
        base = layer * 2 * MEM_WIDTH
        k_t = kv[:, base:base + MEM_WIDTH].T
        v = kv[:, base + MEM_WIDTH:base + 2 * MEM_WIDTH]
        for h in range(MEM_HEADS):
            kbd_ref[0, layer, :, h * N_MEM:(h + 1) * N_MEM] = jnp.where(
                row_head == h, k_t, 0.0).astype(BF16)
            vbd_ref[0, layer, h * N_MEM:(h + 1) * N_MEM, :] = jnp.where(
                col_head == h, v, 0.0).astype(BF16)


def _mem_kv(mem, mem_norm_g, w_cat):
    b = mem.shape[0]
    n_layers = w_cat.shape[1] // (2 * MEM_WIDTH)
    return pl.pallas_call(
        _mem_kv_kernel,
        grid=(b,),
        in_specs=[
            pl.BlockSpec((1, N_MEM, D_MODEL), lambda i: (i, 0, 0)),
            pl.BlockSpec((1, D_MODEL), lambda i: (0, 0)),
            pl.BlockSpec(w_cat.shape, lambda i: (0, 0)),
        ],
        out_specs=[
            pl.BlockSpec((1, n_layers, MEM_WIDTH, MEM_HEADS * N_MEM), lambda i: (i, 0, 0, 0)),
            pl.BlockSpec((1, n_layers, MEM_HEADS * N_MEM, MEM_WIDTH), lambda i: (i, 0, 0, 0)),
        ],
        out_shape=[
            jax.ShapeDtypeStruct((b, n_layers, MEM_WIDTH, MEM_HEADS * N_MEM), BF16),
            jax.ShapeDtypeStruct((b, n_layers, MEM_HEADS * N_MEM, MEM_WIDTH), BF16),
        ],
        compiler_params=pltpu.CompilerParams(
            dimension_semantics=("parallel",), vmem_limit_bytes=VMEM_LIMIT),
        name="mem_kv",
    )(mem, mem_norm_g, w_cat)


def _split_bf16(a):
    hi = a.astype(BF16)
    return hi, (a - hi.astype(F32)).astype(BF16)


def _pool_fold_kernel(w_u_ref, w_bd_ref, o_ref):
    b_hi, b_lo = _split_bf16(w_bd_ref[...])
    n = ROW_CHUNK
    for c in range(w_u_ref.shape[0] // n):
        a_hi, a_lo = _split_bf16(w_u_ref[c * n:(c + 1) * n, :])
        o_ref[c * n:(c + 1) * n, :] = (
            _dot(a_hi, b_hi) + _dot(a_hi, b_lo) + _dot(a_lo, b_hi)).astype(BF16)


def _pool_fold(w_u, w_bd):
    return pl.pallas_call(
        _pool_fold_kernel,
        out_shape=jax.ShapeDtypeStruct(w_u.shape, BF16),
        compiler_params=pltpu.CompilerParams(vmem_limit_bytes=VMEM_LIMIT),
        name="pool_fold",
    )(w_u, w_bd)


def _window_sums(a):
    s2 = a + pltpu.roll(a, 1, 0)
    s4 = s2 + pltpu.roll(s2, 2, 0)
    s8 = s4 + pltpu.roll(s4, 4, 0)
    s16 = s8 + pltpu.roll(s8, 8, 0)
    return {2: s2, 4: s4, 8: s8, 16: s16}


def _pooled_mix(ubuf, row0, n, pos0):
    pos = pos0 + jax.lax.broadcasted_iota(jnp.int32, (n, 1), 0)
    inv_cnt = {w: 1.0 / jnp.minimum(pos + 1, w).astype(F32) for w in POOL_WINDOWS}
    lane = jax.lax.broadcasted_iota(jnp.int32, (n, LANES), 1)
    mixed = []
    for c in range(POOL_WIDTH // LANES):
        cols = slice(c * LANES, (c + 1) * LANES)
        sums = _window_sums(ubuf[row0:row0 + POOL_HALO + n, cols])
        lo_grp = (c * LANES) // POOL_GROUP_DIM
        hi_grp = ((c + 1) * LANES - 1) // POOL_GROUP_DIM
        w_lo, w_hi = POOL_WINDOWS[lo_grp], POOL_WINDOWS[hi_grp]
        pooled = sums[w_lo][POOL_HALO:] * inv_cnt[w_lo]
        if hi_grp != lo_grp:
            split = hi_grp * POOL_GROUP_DIM - c * LANES
            pooled = jnp.where(lane < split, pooled, sums[w_hi][POOL_HALO:] * inv_cnt[w_hi])
        mixed.append(pooled - ubuf[POOL_HALO + row0:POOL_HALO + row0 + n, cols])
    return jnp.concatenate(mixed, axis=-1)


def _pool_layer_kernel(x_ref, w_u_ref, w_mq_ref, w_gate_ref, scale_ref,
                       kbd_ref, vbd_ref, w_out_ref, o_ref, ubuf):
    j = pl.program_id(1)
    t = x_ref.shape[1]
    n = ROW_CHUNK
    chunks = range(t // n)

    @pl.when(j == 0)
    def _():
        ubuf[0:POOL_HALO, :] = jnp.zeros((POOL_HALO, POOL_WIDTH), F32)

    @pl.when(j > 0)
    def _():
        ubuf[0:POOL_HALO, :] = ubuf[t:t + POOL_HALO, :]

    xs = [x_ref[0, c * n:(c + 1) * n, :] for c in chunks]
    hns = [_rms_scale(x).astype(BF16) for x in xs]
    for c in chunks:
        ubuf[POOL_HALO + c * n:POOL_HALO + (c + 1) * n, :] = _dot(hns[c], w_u_ref[...])
    mqs = [_dot(hn, w_mq_ref[...]) for hn in hns]
    gates = [_silu(_dot(hn, w_gate_ref[...])) for hn in hns]
    toks = [_pooled_mix(ubuf, c * n, n, j * t + c * n) * scale_ref[...] for c in chunks]
    mem_os = [_memory_attention(mq, kbd_ref[0, 0], vbd_ref[0, 0]) for mq in mqs]
    for c in chunks:
        y = _dot((toks[c] * gates[c][:, :POOL_WIDTH]).astype(BF16), w_out_ref[:POOL_WIDTH, :])
        y += _dot((mem_os[c] * gates[c][:, POOL_WIDTH:]).astype(BF16), w_out_ref[POOL_WIDTH:, :])
        o_ref[0, c * n:(c + 1) * n, :] = xs[c] + y


def _pool_layer(x, w_u, w_mq, w_gate, scale, kbd, vbd, w_out):
    b, s, _ = x.shape
    t = SEQ_TILE
    const = lambda shape: pl.BlockSpec(shape, lambda i, j: (0,) * len(shape))
    return pl.pallas_call(
        _pool_layer_kernel,
        grid=(b, s // t),
        in_specs=[
            pl.BlockSpec((1, t, D_MODEL), lambda i, j: (i, j, 0)),
            const(w_u.shape), const(w_mq.shape), const(w_gate.shape), const(scale.shape),
            pl.BlockSpec((1, 1) + kbd.shape[2:], lambda i, j: (i, 0, 0, 0)),
            pl.BlockSpec((1, 1) + vbd.shape[2:], lambda i, j: (i, 0, 0, 0)),
            const(w_out.shape),
        ],
        out_specs=pl.BlockSpec((1, t, D_MODEL), lambda i, j: (i, j, 0)),
        out_shape=jax.ShapeDtypeStruct(x.shape, F32),
        scratch_shapes=[pltpu.VMEM((POOL_HALO + t, POOL_WIDTH), F32)],
        compiler_params=pltpu.CompilerParams(
            dimension_semantics=("parallel", "arbitrary"), vmem_limit_bytes=VMEM_LIMIT),
        name="pool_layer",
    )(x, w_u, w_mq, w_gate, scale, kbd, vbd, w_out)


def _rope_t(x, cos_t, sin_t):
    half = QK_ROPE // 2
    x1, x2 = x[:half], x[half:]
    return x1 * cos_t - x2 * sin_t, x2 * cos_t + x1 * sin_t


def _mla_pre_kernel(h_ref, pos_ref, inv_freq_ref, w_cq_t_ref, w_ckv_ref,
                    w_kr_t_ref, w_mq_ref, w_gate_ref, w_uq_t_ref, w_uk_ref,
                    w_uv_t_ref, kbd_ref, vbd_ref, w_out_mem_ref,
                    q_ref, k_ref, v_ref, sg_ref, hp_ref):
    t = h_ref.shape[1]
    n = ROW_CHUNK
    chunks = range(t // n)
    heads = range(MLA_HEADS)
    half = QK_ROPE // 2
    rows = [slice(c * n, (c + 1) * n) for c in chunks]

    hs = [h_ref[0, rows[c], :] for c in chunks]
    xns = [_rms_scale(h) for h in hs]
    hns = [xn.astype(BF16) for xn in xns]
    mqs = [_dot(hn, w_mq_ref[...]) for hn in hns]
    mem_os = [_memory_attention(mq, kbd_ref[0, 0], vbd_ref[0, 0]) for mq in mqs]
    gates = [_silu(_dot(hn, w_gate_ref[...])) for hn in hns]
    for c in chunks:
        sg_ref[0, rows[c], :] = gates[c][:, :MLA_WIDTH]
        hp_ref[0, rows[c], :] = hs[c] + _dot(
            (mem_os[c] * gates[c][:, MLA_WIDTH:]).astype(BF16), w_out_mem_ref[...])

    hn_ts = [xn.T.astype(BF16) for xn in xns]
    cq_ts = [_dot(w_cq_t_ref[...], hn_t) for hn_t in hn_ts]
    kr_ts = [_dot(w_kr_t_ref[...], hn_t) for hn_t in hn_ts]
    ckvns = [_rms_scale(_dot(hn, w_ckv_ref[...])).astype(BF16) for hn in hns]
    angs =[pos_ref[0, :, rows[c]].astype(F32) * inv_freq_ref[...] for c in chunks]
    cos_ts, sin_ts = [jnp.cos(a) for a in angs], [jnp.sin(a) for a in angs]
    cqn_ts = [(cq * jax.lax.rsqrt(jnp.mean(cq * cq, axis=0, keepdims=True) + EPS)
               ).astype(BF16) for cq in cq_ts]

    q_ts = [_dot(w_uq_t_ref[...], cqn_t) for cqn_t in cqn_ts]
    for c in chunks:
        for hd in heads:
            base = hd * QK_DIM
            r1, r2 = _rope_t(q_ts[c][base + QK_NOPE:base + QK_DIM], cos_ts[c], sin_ts[c])
            q_ref[0, hd, 0:QK_NOPE, rows[c]] = q_ts[c][base:base + QK_NOPE].astype(BF16)
            q_ref[0, hd, QK_NOPE:QK_NOPE + half, rows[c]] = r1.astype(BF16)
            q_ref[0, hd, QK_NOPE + half:QK_DIM, rows[c]] = r2.astype(BF16)

    k_nopes = [_dot(ckvn, w_uk_ref[...]) for ckvn in ckvns]
    v_ts = [_dot_nt(w_uv_t_ref[...], ckvn) for ckvn in ckvns]
    for c in chunks:
        r1, r2 = _rope_t(kr_ts[c], cos_ts[c], sin_ts[c])
        k_rope = jnp.concatenate([r1, r2, jnp.zeros((LANES - QK_ROPE, n), F32)], axis=0).T
        k_rope = k_rope[:, :QK_ROPE]
        kt, off = divmod(c * n, ATTN_TILE)
        for hd in heads:
            k_ref[0, hd, rows[c], :] = jnp.concatenate(
                [k_nopes[c][:, hd * QK_NOPE:(hd + 1) * QK_NOPE], k_rope], axis=-1).astype(BF16)
            v_ref[0, hd, kt, 0:V_HEAD, off:off + n] = (
                v_ts[c][hd * V_HEAD:(hd + 1) * V_HEAD].astype(BF16))
            v_ref[0, hd, kt, V_HEAD:V_EXT, off:off + n] = jnp.ones((V_EXT - V_HEAD, n), BF16)


def _mla_pre(h, pos, inv_freq, w_cq_t, w_ckv, w_kr_t, w_mq, w_gate, w_uq_t, w_uk, w_uv_t,
             kbd, vbd, w_out_mem):
    b, s, _ = h.shape
    t = SEQ_TILE
    n_kt = t // ATTN_TILE
    const = lambda shape: pl.BlockSpec(shape, lambda i, j: (0,) * len(shape))
    row = lambda width: pl.BlockSpec((1, t, width), lambda i, j: (i, j, 0))
    return pl.pallas_call(
        _mla_pre_kernel,
        grid=(b, s // t),
        in_specs=[
            row(D_MODEL),
            pl.BlockSpec((1, 1, t), lambda i, j: (i, 0, j)), const(inv_freq.shape),
            const(w_cq_t.shape), const(w_ckv.shape), const(w_kr_t.shape), const(w_mq.shape),
            const(w_gate.shape), const(w_uq_t.shape), const(w_uk.shape), const(w_uv_t.shape),
            pl.BlockSpec((1, 1) + kbd.shape[2:], lambda i, j: (i, 1, 0, 0)),
            pl.BlockSpec((1, 1) + vbd.shape[2:], lambda i, j: (i, 1, 0, 0)),
            const(w_out_mem.shape),
        ],
        out_specs=[
            pl.BlockSpec((1, MLA_HEADS, QK_DIM, t), lambda i, j: (i, 0, 0, j)),
            pl.BlockSpec((1, MLA_HEADS, t, QK_DIM), lambda i, j: (i, 0, j, 0)),
            pl.BlockSpec((1, MLA_HEADS, n_kt, V_EXT, ATTN_TILE), lambda i, j: (i, 0, j, 0, 0)),
            row(MLA_WIDTH), row(D_MODEL),
        ],
        out_shape=[
            jax.ShapeDtypeStruct((b, MLA_HEADS, QK_DIM, s), BF16),
            jax.ShapeDtypeStruct((b, MLA_HEADS, s, QK_DIM), BF16),
            jax.ShapeDtypeStruct((b, MLA_HEADS, s // ATTN_TILE, V_EXT, ATTN_TILE), BF16),
            jax.ShapeDtypeStruct((b, s, MLA_WIDTH), F32),
            jax.ShapeDtypeStruct((b, s, D_MODEL), F32),
        ],
        compiler_params=pltpu.CompilerParams(
            dimension_semantics=("parallel", "parallel"), vmem_limit_bytes=VMEM_LIMIT),
        name="mla_pre",
    )(h, pos, inv_freq, w_cq_t, w_ckv, w_kr_t, w_mq, w_gate, w_uq_t, w_uk, w_uv_t, kbd, vbd,
      w_out_mem)


def _sublane_allreduce(x, op):
    for shift in (4, 2, 1):
        x = op(x, pltpu.roll(x, shift, 0))
    return x


def _mla_attn_kernel(q_ref, k_ref, v_ref, sg_ref, hp_ref, w_out_ref, g_ref, o_ref,
                     acc_sc, tok_sc):
    i = pl.program_id(1)
    tq = q_ref.shape[3]
    tk = v_ref.shape[4]
    half = tk // 2
    heads = range(MLA_HEADS)

    def step(jt, carry, k0, kn, q0, qn, masked):
        m_old = carry
        ks = slice(jt * tk + k0, jt * tk + k0 + kn)
        qs = slice(q0, q0 + qn)
        s = [_dot(k_ref[0, hd, ks, :], q_ref[0, hd, :, qs]) for hd in heads]
        if masked:
            k_chunk = (k0 + jax.lax.broadcasted_iota(jnp.int32, (kn, qn), 0)) // CHUNK
            q_chunk = (q0 + jax.lax.broadcasted_iota(jnp.int32, (kn, qn), 1)) // CHUNK
            mask = k_chunk <= q_chunk
        m_new, alpha, p = [], [], []
        for hd in heads:
            sh = jnp.where(mask, s[hd], NEG_INF) if masked else s[hd]
            s3 = sh.reshape(kn // SUBLANES, SUBLANES, qn)
            m_prev = m_old[hd][:, qs]
            m = jnp.maximum(m_prev, _sublane_allreduce(jnp.max(s3, axis=0), jnp.maximum))
            alpha.append(jnp.exp2(m_prev - m))
            p.append(jnp.exp2(s3 - m[None]).reshape(kn, qn).astype(BF16))
            if qn != tq:
                m = jnp.concatenate(
                    ([m_old[hd][:, :q0]] if q0 else []) + [m]
                    + ([m_old[hd][:, q0 + qn:]] if q0 + qn < tq else []), axis=1)
            m_new.append(m)
        pv = [_dot(v_ref[0, hd, jt, :, k0:k0 + kn], p[hd]) for hd in heads]
        for hd in heads:
            acc = acc_sc[hd, :, qs].reshape(V_EXT // SUBLANES, SUBLANES, qn)
            acc_sc[hd, :, qs] = (alpha[hd][None] * acc).reshape(V_EXT, qn) + pv[hd]
        return tuple(m_new)

    def query_tile(n_full):
        acc_sc[...] = jnp.zeros(acc_sc.shape, F32)
        carry = tuple(jnp.full((SUBLANES, tq), NEG_INF, F32) for _ in heads)
        for jt in range(n_full):
            carry = step(jt, carry, 0, tk, 0, tq, False)
        carry = step(n_full, carry, 0, half, 0, tq, True)
        step(n_full, carry, half, tk - half, half, tq - half, True)

        for hd in heads:
            l_tot = acc_sc[hd, V_HEAD:V_HEAD + SUBLANES, :]
            o_t = acc_sc[hd, 0:V_HEAD, :].reshape(V_HEAD // SUBLANES, SUBLANES, tq) / l_tot[None]
            tok_sc[:, hd * V_HEAD:(hd + 1) * V_HEAD] = o_t.reshape(V_HEAD, tq).T

        rows = [slice(c * ROW_CHUNK, (c + 1) * ROW_CHUNK) for c in range(tq // ROW_CHUNK)]
        branches = [(tok_sc[r, :] * sg_ref[0, r, :]).astype(BF16) for r in rows]
        outs = [hp_ref[0, r, :] + _dot(br, w_out_ref[...]) for r, br in zip(rows, branches)]
        for r, out in zip(rows, outs):
            o_ref[0, r, :] = _rms_norm(out, g_ref[...])

    for n_full in range(k_ref.shape[2] // tk):
        pl.when(i == n_full)(functools.partial(query_tile, n_full))


def _mla_attn(q_t, k, v_t, sg, hp, w_out_tok, final_g):
    b, _, s, _ = k.shape
    t = ATTN_TILE
    const = lambda shape: pl.BlockSpec(shape, lambda i, j: (0,) * len(shape))
    row = lambda width: pl.BlockSpec((1, t, width), lambda i, j: (i, j, 0))
    return pl.pallas_call(
        _mla_attn_kernel,
        grid=(b, s // t),
        in_specs=[
            pl.BlockSpec((1, MLA_HEADS, QK_DIM, t), lambda i, j: (i, 0, 0, j)),
            pl.BlockSpec((1, MLA_HEADS, s, QK_DIM), lambda i, j: (i, 0, 0, 0)),
            pl.BlockSpec((1,) + v_t.shape[1:], lambda i, j: (i, 0, 0, 0, 0)),
            row(MLA_WIDTH), row(D_MODEL), const(w_out_tok.shape), const(final_g.shape),
        ],
        out_specs=row(D_MODEL),
        out_shape=jax.ShapeDtypeStruct((b, s, D_MODEL), F32),
        scratch_shapes=[
            pltpu.VMEM((MLA_HEADS, V_EXT, t), F32),
            pltpu.VMEM((t, MLA_WIDTH), F32),
        ],
        compiler_params=pltpu.CompilerParams(
            dimension_semantics=("parallel", "arbitrary"), vmem_limit_bytes=VMEM_LIMIT),
        name="mla_attn",
    )(q_t, k, v_t, sg, hp, w_out_tok, final_g)


def _block_diag(w_group):
    g, c, d = w_group.shape
    out = jnp.zeros((g * c, g * d), w_group.dtype)
    for i in range(g):
        out = out.at[i * c:(i + 1) * c, i * d:(i + 1) * d].set(w_group[i])
    return out


def kernel(x, mem, positions, norm_g, mem_norm_g, w_mem_kv, w_out, pool_w_in, pool_w_group,
           pool_scale, mla_w_in, mla_q_norm_g, mla_w_uq, mla_kv_norm_g, mla_w_ukv, final_norm_g):
    depth = w_mem_kv.shape[0]
    assert depth == 2 and pool_w_in.shape[0] == 1 and mla_w_in.shape[0] == 1
    bf = lambda a: a.astype(BF16)
    row = lambda a: a.reshape(1, -1)

    inv_freq = ROPE_THETA ** (-(jnp.arange(0, QK_ROPE, 2, dtype=F32) / QK_ROPE))

    w_mem_cat = bf(jnp.concatenate([w_mem_kv[i] for i in range(depth)], axis=1))
    kbd, vbd = _mem_kv(mem, row(mem_norm_g), w_mem_cat)

    pw = pool_w_in[0] * norm_g[0][:, None]
    h1 = _pool_layer(
        x,
        _pool_fold(pw[:, :POOL_WIDTH], _block_diag(pool_w_group[0])),
        bf(pw[:, POOL_WIDTH:POOL_WIDTH + MEM_WIDTH] * MEM_Q_SCALE),
        bf(pw[:, POOL_WIDTH + MEM_WIDTH:]),
        row(pool_scale[0]), kbd, vbd, bf(w_out[0]))

    mw = mla_w_in[0] * norm_g[1][:, None]
    o_ckv = Q_LORA
    o_kr = o_ckv + KV_LORA
    o_mq = o_kr + QK_ROPE
    o_gate = o_mq + MEM_WIDTH
    w_ukv = (mla_w_ukv[0] * mla_kv_norm_g[0][:, None]).reshape(
        KV_LORA, MLA_HEADS, QK_NOPE + V_HEAD)
    w_uk = w_ukv[:, :, :QK_NOPE].reshape(KV_LORA, MLA_HEADS * QK_NOPE)
    w_uv_t = w_ukv[:, :, QK_NOPE:].reshape(KV_LORA, MLA_WIDTH).T
    w_uq_t = (mla_w_uq[0] * (mla_q_norm_g[0][:, None] * Q_SCALE)).T
    q_t, k, v_t, sg, hp = _mla_pre(
        h1, positions[:, None, :], inv_freq[:, None],
        bf(mw[:, :o_ckv].T), bf(mw[:, o_ckv:o_kr]), bf(mw[:, o_kr:o_mq].T),
        bf(mw[:, o_mq:o_gate] * MEM_Q_SCALE), bf(mw[:, o_gate:]), bf(w_uq_t),
        bf(w_uk), bf(w_uv_t), kbd, vbd, bf(w_out[1, MLA_WIDTH:]))

    return _mla_attn(q_t, k, v_t, sg, hp, bf(w_out[1, :MLA_WIDTH]), row(final_norm_g))
```

```python
import functools
import math

import jax
import jax.numpy as jnp
from jax.experimental import pallas as pl
from jax.experimental.pallas import tpu as pltpu

D_MODEL = 1024
CHUNK = 64
EPS = 1e-6
NEG_INF = -1e30

N_MEM = 256
MEM_HEADS = 4
MEM_HEAD_DIM = 64
MEM_WIDTH = MEM_HEADS * MEM_HEAD_DIM

POOL_WINDOWS = (2, 4, 8, 16)
POOL_GROUP_DIM = 192
POOL_WIDTH = len(POOL_WINDOWS) * POOL_GROUP_DIM
POOL_HALO = 16

MLA_HEADS = 6
QK_NOPE = 128
QK_ROPE = 64
QK_DIM = QK_NOPE + QK_ROPE
V_HEAD = 128
V_EXT = V_HEAD + 16
Q_LORA = 384
KV_LORA = 256
MLA_WIDTH = MLA_HEADS * V_HEAD
ROPE_THETA = 10000.0
LOG2_E = math.log2(math.e)
Q_SCALE = QK_DIM ** -0.5 * LOG2_E
MEM_Q_SCALE = MEM_HEAD_DIM ** -0.5 * LOG2_E

MIX_WIDTH = POOL_WIDTH + MEM_WIDTH
POOL_IN = POOL_WIDTH + MEM_WIDTH + MIX_WIDTH

LANES = 128
SUBLANES = 8
SEQ_TILE = 1024
ROW_CHUNK = 256
ATTN_TILE = 512
VMEM_LIMIT = 56 * 1024 * 1024

BF16 = jnp.bfloat16
F32 = jnp.float32


def _rms_scale(x):
    ms = jnp.mean(x * x, axis=-1, keepdims=True)
    return x * jax.lax.rsqrt(ms + EPS)


def _rms_norm(x, g):
    return _rms_scale(x) * g


def _silu(x):
    return x / (1.0 + jnp.exp2(x * -LOG2_E))


def _dot(a, b):
    return jnp.dot(a, b, preferred_element_type=F32)


def _dot_nt(a, b):
    return jax.lax.dot_general(a, b, (((1,), (1,)), ((), ())), preferred_element_type=F32)


def _memory_attention(mq, kbd, vbd):
    s = _dot(mq.astype(BF16), kbd)
    probs = []
    for h in range(MEM_HEADS):
        sh = s[:, h * N_MEM:(h + 1) * N_MEM]
        e = jnp.exp2(sh - jnp.max(sh, axis=-1, keepdims=True))
        probs.append((e / jnp.sum(e, axis=-1, keepdims=True)).astype(BF16))
    return _dot(jnp.concatenate(probs, axis=-1), vbd)


def _mem_kv_kernel(mem_ref, g_ref, w_ref, kbd_ref, vbd_ref):
    mem_n = _rms_norm(mem_ref[0], g_ref[...]).astype(BF16)
    row_head = jax.lax.broadcasted_iota(jnp.int32, (MEM_WIDTH, N_MEM), 0) // MEM_HEAD_DIM
    col_head = jax.lax.broadcasted_iota(jnp.int32, (N_MEM, MEM_WIDTH), 1) // MEM_HEAD_DIM
    for layer in range(w_ref.shape[0]):
        kv = _dot(mem_n, w_ref[layer])
        k_t = kv[:, :MEM_WIDTH].T
        v = kv[:, MEM_WIDTH:]
        for h in range(MEM_HEADS):
            kbd_ref[0, layer, :, h * N_MEM:(h + 1) * N_MEM] = jnp.where(
                row_head == h, k_t, 0.0).astype(BF16)
            vbd_ref[0, layer, h * N_MEM:(h + 1) * N_MEM, :] = jnp.where(
                col_head == h, v, 0.0).astype(BF16)


def _mem_kv(mem, mem_norm_g, w):
    b = mem.shape[0]
    n_layers = w.shape[0]
    return pl.pallas_call(
        _mem_kv_kernel,
        grid=(b,),
        in_specs=[
            pl.BlockSpec((1, N_MEM, D_MODEL), lambda i: (i, 0, 0)),
            pl.BlockSpec((1, D_MODEL), lambda i: (0, 0)),
            pl.BlockSpec(w.shape, lambda i: (0, 0, 0)),
        ],
        out_specs=[
            pl.BlockSpec((1, n_layers, MEM_WIDTH, MEM_HEADS * N_MEM), lambda i: (i, 0, 0, 0)),
            pl.BlockSpec((1, n_layers, MEM_HEADS * N_MEM, MEM_WIDTH), lambda i: (i, 0, 0, 0)),
        ],
        out_shape=[
            jax.ShapeDtypeStruct((b, n_layers, MEM_WIDTH, MEM_HEADS * N_MEM), BF16),
            jax.ShapeDtypeStruct((b, n_layers, MEM_HEADS * N_MEM, MEM_WIDTH), BF16),
        ],
        compiler_params=pltpu.CompilerParams(
            dimension_semantics=("parallel",), vmem_limit_bytes=VMEM_LIMIT),
        name="mem_kv",
    )(mem, mem_norm_g, w)


def _split_bf16(a):
    hi = a.astype(BF16)
    return hi, (a - hi.astype(F32)).astype(BF16)


def _pool_fold_kernel(w_u_ref, g_ref, w_grp_ref, o_ref):
    grp = [_split_bf16(w_grp_ref[0, gi]) for gi in range(len(POOL_WINDOWS))]
    n = ROW_CHUNK
    for c in range(w_u_ref.shape[1] // n):
        a = w_u_ref[0, c * n:(c + 1) * n, :] * g_ref[c * n:(c + 1) * n, :]
        cols = []
        for gi, (b_hi, b_lo) in enumerate(grp):
            a_hi, a_lo = _split_bf16(a[:, gi * POOL_GROUP_DIM:(gi + 1) * POOL_GROUP_DIM])
            cols.append(_dot(a_hi, b_hi) + _dot(a_hi, b_lo) + _dot(a_lo, b_hi))
        o_ref[c * n:(c + 1) * n, :] = jnp.concatenate(cols, axis=-1).astype(BF16)


def _pool_fold(pool_w_in, g_col, pool_w_group):
    return pl.pallas_call(
        _pool_fold_kernel,
        grid=(1,),
        in_specs=[
            pl.BlockSpec((1, D_MODEL, POOL_WIDTH), lambda i: (0, 0, 0)),
            pl.BlockSpec(g_col.shape, lambda i: (0, 0)),
            pl.BlockSpec(pool_w_group.shape, lambda i: (0, 0, 0, 0)),
        ],
        out_specs=pl.BlockSpec((D_MODEL, POOL_WIDTH), lambda i: (0, 0)),
        out_shape=jax.ShapeDtypeStruct((D_MODEL, POOL_WIDTH), BF16),
        compiler_params=pltpu.CompilerParams(vmem_limit_bytes=VMEM_LIMIT),
        name="pool_fold",
    )(pool_w_in, g_col, pool_w_group)


def _window_sums(a):
    s2 = a + pltpu.roll(a, 1, 0)
    s4 = s2 + pltpu.roll(s2, 2, 0)
    s8 = s4 + pltpu.roll(s4, 4, 0)
    s16 = s8 + pltpu.roll(s8, 8, 0)
    return {2: s2, 4: s4, 8: s8, 16: s16}


def _pooled_mix(ubuf, row0, n, pos0):
    pos = pos0 + jax.lax.broadcasted_iota(jnp.int32, (n, 1), 0)
    inv_cnt = {w: 1.0 / jnp.minimum(pos + 1, w).astype(F32) for w in POOL_WINDOWS}
    lane = jax.lax.broadcasted_iota(jnp.int32, (n, LANES), 1)
    mixed = []
    for c in range(POOL_WIDTH // LANES):
        cols = slice(c * LANES, (c + 1) * LANES)
        sums = _window_sums(ubuf[row0:row0 + POOL_HALO + n, cols])
        lo_grp = (c * LANES) // POOL_GROUP_DIM
        hi_grp = ((c + 1) * LANES - 1) // POOL_GROUP_DIM
        w_lo, w_hi = POOL_WINDOWS[lo_grp], POOL_WINDOWS[hi_grp]
        pooled = sums[w_lo][POOL_HALO:] * inv_cnt[w_lo]
        if hi_grp != lo_grp:
            split = hi_grp * POOL_GROUP_DIM - c * LANES
            pooled = jnp.where(lane < split, pooled, sums[w_hi][POOL_HALO:] * inv_cnt[w_hi])
        mixed.append(pooled - ubuf[POOL_HALO + row0:POOL_HALO + row0 + n, cols])
    return jnp.concatenate(mixed, axis=-1)


def _pool_layer_kernel(x_ref, w_u_ref, w_mq_ref, w_gate_ref, scale_ref,
                       kbd_ref, vbd_ref, w_out_ref, o_ref, ubuf):
    j = pl.program_id(1)
    t = x_ref.shape[1]
    n = ROW_CHUNK
    chunks = range(t // n)

    @pl.when(j == 0)
    def _():
        ubuf[0:POOL_HALO, :] = jnp.zeros((POOL_HALO, POOL_WIDTH), F32)

    @pl.when(j > 0)
    def _():
        ubuf[0:POOL_HALO, :] = ubuf[t:t + POOL_HALO, :]

    xs = [x_ref[0, c * n:(c + 1) * n, :] for c in chunks]
    hns = [_rms_scale(x).astype(BF16) for x in xs]
    for c in chunks:
        ubuf[POOL_HALO + c * n:POOL_HALO + (c + 1) * n, :] = _dot(hns[c], w_u_ref[...])
    mqs = [_dot(hn, w_mq_ref[...]) for hn in hns]
    gates = [_silu(_dot(hn, w_gate_ref[...])) for hn in hns]
    toks = [_pooled_mix(ubuf, c * n, n, j * t + c * n) * scale_ref[...] for c in chunks]
    mem_os = [_memory_attention(mq, kbd_ref[0, 0], vbd_ref[0, 0]) for mq in mqs]
    for c in chunks:
        y = _dot((toks[c] * gates[c][:, :POOL_WIDTH]).astype(BF16), w_out_ref[0, :POOL_WIDTH, :])
        y += _dot((mem_os[c] * gates[c][:, POOL_WIDTH:]).astype(BF16),
                  w_out_ref[0, POOL_WIDTH:, :])
        o_ref[0, c * n:(c + 1) * n, :] = xs[c] + y


def _pool_layer(x, w_u, w_in, scale, kbd, vbd, w_out):
    b, s, _ = x.shape
    t = SEQ_TILE
    const = lambda shape: pl.BlockSpec(shape, lambda i, j: (0,) * len(shape))
    return pl.pallas_call(
        _pool_layer_kernel,
        grid=(b, s // t),
        in_specs=[
            pl.BlockSpec((1, t, D_MODEL), lambda i, j: (i, j, 0)),
            const(w_u.shape),
            pl.BlockSpec((D_MODEL, MEM_WIDTH), lambda i, j: (0, POOL_WIDTH // MEM_WIDTH)),
            pl.BlockSpec((D_MODEL, MIX_WIDTH), lambda i, j: (0, 1)),
            const(scale.shape),
            pl.BlockSpec((1, 1) + kbd.shape[2:], lambda i, j: (i, 0, 0, 0)),
            pl.BlockSpec((1, 1) + vbd.shape[2:], lambda i, j: (i, 0, 0, 0)),
            pl.BlockSpec((1,) + w_out.shape[1:], lambda i, j: (0, 0, 0)),
        ],
        out_specs=pl.BlockSpec((1, t, D_MODEL), lambda i, j: (i, j, 0)),
        out_shape=jax.ShapeDtypeStruct(x.shape, F32),
        scratch_shapes=[pltpu.VMEM((POOL_HALO + t, POOL_WIDTH), F32)],
        compiler_params=pltpu.CompilerParams(
            dimension_semantics=("parallel", "arbitrary"), vmem_limit_bytes=VMEM_LIMIT),
        name="pool_layer",
    )(x, w_u, w_in, w_in, scale, kbd, vbd, w_out)


def _rope_t(x, cos_t, sin_t):
    half = QK_ROPE // 2
    x1, x2 = x[:half], x[half:]
    return x1 * cos_t - x2 * sin_t, x2 * cos_t + x1 * sin_t


def _mla_pre_kernel(h_ref, pos_ref, inv_freq_ref, w_fm_ref, w_pm_ref, w_uq_t_ref, w_uk_ref,
                    w_uv_t_ref, kbd_ref, vbd_ref, w_out_ref,
                    q_ref, k_ref, v_ref, sg_ref, hp_ref):
    w_cq_t_ref, w_kr_t_ref = w_fm_ref.at[0:Q_LORA], w_fm_ref.at[Q_LORA:Q_LORA + QK_ROPE]
    w_gate_ref = w_pm_ref.at[:, 0:MIX_WIDTH]
    w_ckv_ref = w_pm_ref.at[:, MIX_WIDTH:MIX_WIDTH + KV_LORA]
    w_mq_ref = w_pm_ref.at[:, MIX_WIDTH + KV_LORA:MIX_WIDTH + KV_LORA + MEM_WIDTH]
    w_out_mem_ref = w_out_ref.at[0]
    t = h_ref.shape[1]
    n = ROW_CHUNK
    chunks = range(t // n)
    heads = range(MLA_HEADS)
    half = QK_ROPE // 2
    rows = [slice(c * n, (c + 1) * n) for c in chunks]

    hs = [h_ref[0, rows[c], :] for c in chunks]
    xns = [_rms_scale(h) for h in hs]
    hns = [xn.astype(BF16) for xn in xns]
    mqs = [_dot(hn, w_mq_ref[...]) for hn in hns]
    mem_os = [_memory_attention(mq, kbd_ref[0, 0], vbd_ref[0, 0]) for mq in mqs]
    gates = [_silu(_dot(hn, w_gate_ref[...])) for hn in hns]
    for c in chunks:
        sg_ref[0, rows[c], :] = gates[c][:, :MLA_WIDTH]
        hp_ref[0, rows[c], :] = hs[c] + _dot(
            (mem_os[c] * gates[c][:, MLA_WIDTH:]).astype(BF16), w_out_mem_ref[...])

    hn_ts = [xn.T.astype(BF16) for xn in xns]
    cq_ts = [_dot(w_cq_t_ref[...], hn_t) for hn_t in hn_ts]
    kr_ts = [_dot(w_kr_t_ref[...], hn_t) for hn_t in hn_ts]
    ckvns = [_rms_scale(_dot(hn, w_ckv_ref[...])).astype(BF16) for hn in hns]
    angs = [pos_ref[0, :, rows[c]].astype(F32) * inv_freq_ref[...] for c in chunks]
    cos_ts, sin_ts = [jnp.cos(a) for a in angs], [jnp.sin(a) for a in angs]
    cqn_ts = [(cq * jax.lax.rsqrt(jnp.mean(cq * cq, axis=0, keepdims=True) + EPS)
               ).astype(BF16) for cq in cq_ts]

    q_ts = [_dot(w_uq_t_ref[...], cqn_t) for cqn_t in cqn_ts]
    for c in chunks:
        for hd in heads:
            base = hd * QK_DIM
            r1, r2 = _rope_t(q_ts[c][base + QK_NOPE:base + QK_DIM], cos_ts[c], sin_ts[c])
            q_ref[0, hd, 0:QK_NOPE, rows[c]] = q_ts[c][base:base + QK_NOPE].astype(BF16)
            q_ref[0, hd, QK_NOPE:QK_NOPE + half, rows[c]] = r1.astype(BF16)
            q_ref[0, hd, QK_NOPE + half:QK_DIM, rows[c]] = r2.astype(BF16)

    k_nopes = [_dot(ckvn, w_uk_ref[...]) for ckvn in ckvns]
    v_ts = [_dot_nt(w_uv_t_ref[...], ckvn) for ckvn in ckvns]
    for c in chunks:
        r1, r2 = _rope_t(kr_ts[c], cos_ts[c], sin_ts[c])
        k_rope = jnp.concatenate([r1, r2, jnp.zeros((LANES - QK_ROPE, n), F32)], axis=0).T
        k_rope = k_rope[:, :QK_ROPE]
        kt, off = divmod(c * n, ATTN_TILE)
        for hd in heads:
            k_ref[0, hd, rows[c], :] = jnp.concatenate(
                [k_nopes[c][:, hd * QK_NOPE:(hd + 1) * QK_NOPE], k_rope], axis=-1).astype(BF16)
            v_ref[0, hd, kt, 0:V_HEAD, off:off + n] = (
                v_ts[c][hd * V_HEAD:(hd + 1) * V_HEAD].astype(BF16))
            v_ref[0, hd, kt, V_HEAD:V_EXT, off:off + n] = jnp.ones((V_EXT - V_HEAD, n), BF16)


def _mla_pre(h, pos, inv_freq, w_fm, w_pm, w_uq_t, w_uk, w_uv_t, kbd, vbd, w_out):
    b, s, _ = h.shape
    t = SEQ_TILE
    n_kt = t // ATTN_TILE
    const = lambda shape: pl.BlockSpec(shape, lambda i, j: (0,) * len(shape))
    row = lambda width: pl.BlockSpec((1, t, width), lambda i, j: (i, j, 0))
    return pl.pallas_call(
        _mla_pre_kernel,
        grid=(b, s // t),
        in_specs=[
            row(D_MODEL),
            pl.BlockSpec((1, 1, t), lambda i, j: (i, 0, j)), const(inv_freq.shape),
            const(w_fm.shape), const(w_pm.shape),
            const(w_uq_t.shape), const(w_uk.shape), const(w_uv_t.shape),
            pl.BlockSpec((1, 1) + kbd.shape[2:], lambda i, j: (i, 1, 0, 0)),
            pl.BlockSpec((1, 1) + vbd.shape[2:], lambda i, j: (i, 1, 0, 0)),
            pl.BlockSpec((1, MEM_WIDTH, D_MODEL), lambda i, j: (1, MLA_WIDTH // MEM_WIDTH, 0)),
        ],
        out_specs=[
            pl.BlockSpec((1, MLA_HEADS, QK_DIM, t), lambda i, j: (i, 0, 0, j)),
            pl.BlockSpec((1, MLA_HEADS, t, QK_DIM), lambda i, j: (i, 0, j, 0)),
            pl.BlockSpec((1, MLA_HEADS, n_kt, V_EXT, ATTN_TILE), lambda i, j: (i, 0, j, 0, 0)),
            row(MLA_WIDTH), row(D_MODEL),
        ],
        out_shape=[
            jax.ShapeDtypeStruct((b, MLA_HEADS, QK_DIM, s), BF16),
            jax.ShapeDtypeStruct((b, MLA_HEADS, s, QK_DIM), BF16),
            jax.ShapeDtypeStruct((b, MLA_HEADS, s // ATTN_TILE, V_EXT, ATTN_TILE), BF16),
            jax.ShapeDtypeStruct((b, s, MLA_WIDTH), F32),
            jax.ShapeDtypeStruct((b, s, D_MODEL), F32),
        ],
        compiler_params=pltpu.CompilerParams(
            dimension_semantics=("parallel", "parallel"), vmem_limit_bytes=VMEM_LIMIT),
        name="mla_pre",
    )(h, pos, inv_freq, w_fm, w_pm, w_uq_t, w_uk, w_uv_t, kbd, vbd, w_out)


def _sublane_allreduce(x, op):
    for shift in (4, 2, 1):
        x = op(x, pltpu.roll(x, shift, 0))
    return x


def _mla_attn_kernel(q_ref, k_ref, v_ref, sg_ref, hp_ref, w_out_ref, g_ref, o_ref,
                     acc_sc, tok_sc):
    i = pl.program_id(1)
    tq = q_ref.shape[3]
    tk = v_ref.shape[4]
    half = tk // 2
    heads = range(MLA_HEADS)

    def step(jt, carry, k0, kn, q0, qn, masked):
        m_old = carry
        ks = slice(jt * tk + k0, jt * tk + k0 + kn)
        qs = slice(q0, q0 + qn)
        s = [_dot(k_ref[0, hd, ks, :], q_ref[0, hd, :, qs]) for hd in heads]
        if masked:
            k_chunk = (k0 + jax.lax.broadcasted_iota(jnp.int32, (kn, qn), 0)) // CHUNK
            q_chunk = (q0 + jax.lax.broadcasted_iota(jnp.int32, (kn, qn), 1)) // CHUNK
            mask = k_chunk <= q_chunk
        m_new, alpha, p = [], [], []
        for hd in heads:
            sh = jnp.where(mask, s[hd], NEG_INF) if masked else s[hd]
            s3 = sh.reshape(kn // SUBLANES, SUBLANES, qn)
            m_prev = m_old[hd][:, qs]
            m = jnp.maximum(m_prev, _sublane_allreduce(jnp.max(s3, axis=0), jnp.maximum))
            alpha.append(jnp.exp2(m_prev - m))
            p.append(jnp.exp2(s3 - m[None]).reshape(kn, qn).astype(BF16))
            if qn != tq:
                m = jnp.concatenate(
                    ([m_old[hd][:, :q0]] if q0 else []) + [m]
                    + ([m_old[hd][:, q0 + qn:]] if q0 + qn < tq else []), axis=1)
            m_new.append(m)
        pv = [_dot(v_ref[0, hd, jt, :, k0:k0 + kn], p[hd]) for hd in heads]
        for hd in heads:
            acc = acc_sc[hd, :, qs].reshape(V_EXT // SUBLANES, SUBLANES, qn)
            acc_sc[hd, :, qs] = (alpha[hd][None] * acc).reshape(V_EXT, qn) + pv[hd]
        return tuple(m_new)

    def query_tile(n_full):
        acc_sc[...] = jnp.zeros(acc_sc.shape, F32)
        carry = tuple(jnp.full((SUBLANES, tq), NEG_INF, F32) for _ in heads)
        for jt in range(n_full):
            carry = step(jt, carry, 0, tk, 0, tq, False)
        carry = step(n_full, carry, 0, half, 0, tq, True)
        step(n_full, carry, half, tk - half, half, tq - half, True)

        for hd in heads:
            l_tot = acc_sc[hd, V_HEAD:V_HEAD + SUBLANES, :]
            o_t = acc_sc[hd, 0:V_HEAD, :].reshape(V_HEAD // SUBLANES, SUBLANES, tq) / l_tot[None]
            tok_sc[:, hd * V_HEAD:(hd + 1) * V_HEAD] = o_t.reshape(V_HEAD, tq).T

        n = tq // 2
        rows = [slice(c * n, (c + 1) * n) for c in range(tq // n)]
        branches = [(tok_sc[r, :] * sg_ref[0, r, :]).astype(BF16) for r in rows]
        outs = [hp_ref[0, r, :] + _dot(br, w_out_ref[0]) for r, br in zip(rows, branches)]
        for r, out in zip(rows, outs):
            o_ref[0, r, :] = _rms_norm(out, g_ref[...])

    for n_full in range(k_ref.shape[2] // tk):
        pl.when(i == n_full)(functools.partial(query_tile, n_full))


def _mla_attn(q_t, k, v_t, sg, hp, w_out, final_g):
    b, _, s, _ = k.shape
    t = ATTN_TILE
    const = lambda shape: pl.BlockSpec(shape, lambda i, j: (0,) * len(shape))
    row = lambda width: pl.BlockSpec((1, t, width), lambda i, j: (i, j, 0))
    return pl.pallas_call(
        _mla_attn_kernel,
        grid=(b, s // t),
        in_specs=[
            pl.BlockSpec((1, MLA_HEADS, QK_DIM, t), lambda i, j: (i, 0, 0, j)),
            pl.BlockSpec((1, MLA_HEADS, s, QK_DIM), lambda i, j: (i, 0, 0, 0)),
            pl.BlockSpec((1,) + v_t.shape[1:], lambda i, j: (i, 0, 0, 0, 0)),
            row(MLA_WIDTH), row(D_MODEL),
            pl.BlockSpec((1, MLA_WIDTH, D_MODEL), lambda i, j: (1, 0, 0)), const(final_g.shape),
        ],
        out_specs=row(D_MODEL),
        out_shape=jax.ShapeDtypeStruct((b, s, D_MODEL), F32),
        scratch_shapes=[
            pltpu.VMEM((MLA_HEADS, V_EXT, t), F32),
            pltpu.VMEM((t, MLA_WIDTH), F32),
        ],
        compiler_params=pltpu.CompilerParams(
            dimension_semantics=("parallel", "arbitrary"), vmem_limit_bytes=VMEM_LIMIT),
        name="mla_attn",
    )(q_t, k, v_t, sg, hp, w_out, final_g)


def kernel(x, mem, positions, norm_g, mem_norm_g, w_mem_kv, w_out, pool_w_in, pool_w_group,
           pool_scale, mla_w_in, mla_q_norm_g, mla_w_uq, mla_kv_norm_g, mla_w_ukv, final_norm_g):
    depth = w_mem_kv.shape[0]
    assert depth == 2 and pool_w_in.shape[0] == 1 and mla_w_in.shape[0] == 1
    bf = lambda a: a.astype(BF16)
    row = lambda a: a.reshape(1, -1)

    inv_freq = ROPE_THETA ** (-(jnp.arange(0, QK_ROPE, 2, dtype=F32) / QK_ROPE))
    kbd, vbd = _mem_kv(mem, row(mem_norm_g), bf(w_mem_kv))
    w_out_bf = bf(w_out)

    g0 = norm_g[0][:, None]
    mq_cols = (jnp.arange(POOL_IN) >= POOL_WIDTH) & (jnp.arange(POOL_IN) < POOL_WIDTH + MEM_WIDTH)
    pool_w = bf(pool_w_in[0] * g0 * jnp.where(mq_cols, MEM_Q_SCALE, 1.0))
    h1 = _pool_layer(x, _pool_fold(pool_w_in, g0, pool_w_group), pool_w, row(pool_scale[0]),
                     kbd, vbd, w_out_bf)

    mw = mla_w_in[0] * norm_g[1][:, None]
    o_ckv = Q_LORA
    o_kr = o_ckv + KV_LORA
    o_mq = o_kr + QK_ROPE
    o_gate = o_mq + MEM_WIDTH
    w_fm = bf(jnp.concatenate([mw[:, :o_ckv], mw[:, o_kr:o_mq]], axis=1).T)
    w_pm = bf(jnp.concatenate(
        [mw[:, o_gate:], mw[:, o_ckv:o_kr], mw[:, o_mq:o_gate] * MEM_Q_SCALE], axis=1))
    w_ukv = (mla_w_ukv[0] * mla_kv_norm_g[0][:, None]).reshape(
        KV_LORA, MLA_HEADS, QK_NOPE + V_HEAD)
    w_uk = bf(w_ukv[:, :, :QK_NOPE].reshape(KV_LORA, MLA_HEADS * QK_NOPE))
    w_uv_t = bf(w_ukv[:, :, QK_NOPE:].reshape(KV_LORA, MLA_WIDTH).T)
    w_uq_t = bf((mla_w_uq[0] * (mla_q_norm_g[0][:, None] * Q_SCALE)).T)
    q_t, k, v_t, sg, hp = _mla_pre(
        h1, positions[:, None, :], inv_freq[:, None], w_fm, w_pm, w_uq_t, w_uk, w_uv_t,
        kbd, vbd, w_out_bf)

    return _mla_attn(q_t, k, v_t, sg, hp, w_out_bf, row(final_norm_g))
```

```python
import functools
import math

import jax
import jax.numpy as jnp
from jax.experimental import pallas as pl
from jax.experimental.pallas import tpu as pltpu

D_MODEL = 1024
CHUNK = 64
EPS = 1e-6
NEG_INF = -1e30

N_MEM = 256
MEM_HEADS = 4
MEM_HEAD_DIM = 64
MEM_WIDTH = MEM_HEADS * MEM_HEAD_DIM

POOL_WINDOWS = (2, 4, 8, 16)
POOL_GROUP_DIM = 192
POOL_WIDTH = len(POOL_WINDOWS) * POOL_GROUP_DIM
POOL_HALO = 16

MLA_HEADS = 6
QK_NOPE = 128
QK_ROPE = 64
QK_DIM = QK_NOPE + QK_ROPE
V_HEAD = 128
V_EXT = V_HEAD + 16
Q_LORA = 384
KV_LORA = 256
MLA_WIDTH = MLA_HEADS * V_HEAD
ROPE_THETA = 10000.0
LOG2_E = math.log2(math.e)
Q_SCALE = QK_DIM ** -0.5 * LOG2_E
MEM_Q_SCALE = MEM_HEAD_DIM ** -0.5 * LOG2_E

MIX_WIDTH = POOL_WIDTH + MEM_WIDTH
POOL_IN = POOL_WIDTH + MEM_WIDTH + MIX_WIDTH

LANES = 128
SUBLANES = 8
SEQ_TILE = 1024
ROW_CHUNK = 256
ATTN_TILE = 512
VMEM_LIMIT = 56 * 1024 * 1024

BF16 = jnp.bfloat16
F32 = jnp.float32


def _rms_scale(x):
    ms = jnp.mean(x * x, axis=-1, keepdims=True)
    return x * jax.lax.rsqrt(ms + EPS)


def _rms_norm(x, g):
    return _rms_scale(x) * g


def _silu(x):
    return x / (1.0 + jnp.exp2(x * -LOG2_E))


def _dot(a, b):
    return jnp.dot(a, b, preferred_element_type=F32)


def _dot_nt(a, b):
    return jax.lax.dot_general(a, b, (((1,), (1,)), ((), ())), preferred_element_type=F32)


def _memory_attention(mq, kbd, vbd):
    s = _dot(mq.astype(BF16), kbd)
    probs = []
    for h in range(MEM_HEADS):
        sh = s[:, h * N_MEM:(h + 1) * N_MEM]
        e = jnp.exp2(sh - jnp.max(sh, axis=-1, keepdims=True))
        probs.append((e / jnp.sum(e, axis=-1, keepdims=True)).astype(BF16))
    return _dot(jnp.concatenate(probs, axis=-1), vbd)


def _mem_kv_kernel(mem_ref, g_ref, w_ref, kbd_ref, vbd_ref):
    mem_n = _rms_norm(mem_ref[0], g_ref[...]).astype(BF16)
    row_head = jax.lax.broadcasted_iota(jnp.int32, (MEM_WIDTH, N_MEM), 0) // MEM_HEAD_DIM
    col_head = jax.lax.broadcasted_iota(jnp.int32, (N_MEM, MEM_WIDTH), 1) // MEM_HEAD_DIM
    for layer in range(w_ref.shape[0]):
        kv = _dot(mem_n, w_ref[layer])
        k_t = kv[:, :MEM_WIDTH].T
        v = kv[:, MEM_WIDTH:]
        for h in range(MEM_HEADS):
            kbd_ref[0, layer, :, h * N_MEM:(h + 1) * N_MEM] = jnp.where(
                row_head == h, k_t, 0.0).astype(BF16)
            vbd_ref[0, layer, h * N_MEM:(h + 1) * N_MEM, :] = jnp.where(
                col_head == h, v, 0.0).astype(BF16)


def _mem_kv(mem, mem_norm_g, w):
    b = mem.shape[0]
    n_layers = w.shape[0]
    return pl.pallas_call(
        _mem_kv_kernel,
        grid=(b,),
        in_specs=[
            pl.BlockSpec((1, N_MEM, D_MODEL), lambda i: (i, 0, 0)),
            pl.BlockSpec((1, D_MODEL), lambda i: (0, 0)),
            pl.BlockSpec(w.shape, lambda i: (0, 0, 0)),
        ],
        out_specs=[
            pl.BlockSpec((1, n_layers, MEM_WIDTH, MEM_HEADS * N_MEM), lambda i: (i, 0, 0, 0)),
            pl.BlockSpec((1, n_layers, MEM_HEADS * N_MEM, MEM_WIDTH), lambda i: (i, 0, 0, 0)),
        ],
        out_shape=[
            jax.ShapeDtypeStruct((b, n_layers, MEM_WIDTH, MEM_HEADS * N_MEM), BF16),
            jax.ShapeDtypeStruct((b, n_layers, MEM_HEADS * N_MEM, MEM_WIDTH), BF16),
        ],
        compiler_params=pltpu.CompilerParams(
            dimension_semantics=("parallel",), vmem_limit_bytes=VMEM_LIMIT),
        name="mem_kv",
    )(mem, mem_norm_g, w)


def _split_bf16(a):
    hi = a.astype(BF16)
    return hi, (a - hi.astype(F32)).astype(BF16)


def _pool_fold_kernel(w_u_ref, g_ref, w_grp_ref, o_ref):
    grp = [_split_bf16(w_grp_ref[0, gi]) for gi in range(len(POOL_WINDOWS))]
    n = ROW_CHUNK
    for c in range(w_u_ref.shape[1] // n):
        a = w_u_ref[0, c * n:(c + 1) * n, :] * g_ref[c * n:(c + 1) * n, :]
        cols = []
        for gi, (b_hi, b_lo) in enumerate(grp):
            a_hi, a_lo = _split_bf16(a[:, gi * POOL_GROUP_DIM:(gi + 1) * POOL_GROUP_DIM])
            cols.append(_dot(a_hi, b_hi) + _dot(a_hi, b_lo) + _dot(a_lo, b_hi))
        o_ref[c * n:(c + 1) * n, :] = jnp.concatenate(cols, axis=-1).astype(BF16)


def _pool_fold(pool_w_in, g_col, pool_w_group):
    return pl.pallas_call(
        _pool_fold_kernel,
        grid=(1,),
        in_specs=[
            pl.BlockSpec((1, D_MODEL, POOL_WIDTH), lambda i: (0, 0, 0)),
            pl.BlockSpec(g_col.shape, lambda i: (0, 0)),
            pl.BlockSpec(pool_w_group.shape, lambda i: (0, 0, 0, 0)),
        ],
        out_specs=pl.BlockSpec((D_MODEL, POOL_WIDTH), lambda i: (0, 0)),
        out_shape=jax.ShapeDtypeStruct((D_MODEL, POOL_WIDTH), BF16),
        compiler_params=pltpu.CompilerParams(vmem_limit_bytes=VMEM_LIMIT),
        name="pool_fold",
    )(pool_w_in, g_col, pool_w_group)


def _window_sums(a):
    s2 = a + pltpu.roll(a, 1, 0)
    s4 = s2 + pltpu.roll(s2, 2, 0)
    s8 = s4 + pltpu.roll(s4, 4, 0)
    s16 = s8 + pltpu.roll(s8, 8, 0)
    return {2: s2, 4: s4, 8: s8, 16: s16}


def _pooled_mix(ubuf, row0, n, pos0):
    pos = pos0 + jax.lax.broadcasted_iota(jnp.int32, (n, 1), 0)
    inv_cnt = {w: 1.0 / jnp.minimum(pos + 1, w).astype(F32) for w in POOL_WINDOWS}
    lane = jax.lax.broadcasted_iota(jnp.int32, (n, LANES), 1)
    mixed = []
    for c in range(POOL_WIDTH // LANES):
        cols = slice(c * LANES, (c + 1) * LANES)
        sums = _window_sums(ubuf[row0:row0 + POOL_HALO + n, cols])
        lo_grp = (c * LANES) // POOL_GROUP_DIM
        hi_grp = ((c + 1) * LANES - 1) // POOL_GROUP_DIM
        w_lo, w_hi = POOL_WINDOWS[lo_grp], POOL_WINDOWS[hi_grp]
        pooled = sums[w_lo][POOL_HALO:] * inv_cnt[w_lo]
        if hi_grp != lo_grp:
            split = hi_grp * POOL_GROUP_DIM - c * LANES
            pooled = jnp.where(lane < split, pooled, sums[w_hi][POOL_HALO:] * inv_cnt[w_hi])
        mixed.append(pooled - ubuf[POOL_HALO + row0:POOL_HALO + row0 + n, cols])
    return jnp.concatenate(mixed, axis=-1)


def _pool_layer_kernel(x_ref, w_u_ref, w_mq_ref, w_gate_ref, scale_ref,
                       kbd_ref, vbd_ref, w_out_ref, o_ref, ubuf):
    j = pl.program_id(1)
    t = x_ref.shape[1]
    n = ROW_CHUNK
    chunks = range(t // n)

    @pl.when(j == 0)
    def _():
        ubuf[0:POOL_HALO, :] = jnp.zeros((POOL_HALO, POOL_WIDTH), F32)

    @pl.when(j > 0)
    def _():
        ubuf[0:POOL_HALO, :] = ubuf[t:t + POOL_HALO, :]

    xs = [x_ref[0, c * n:(c + 1) * n, :] for c in chunks]
    hns = [_rms_scale(x).astype(BF16) for x in xs]
    for c in chunks:
        ubuf[POOL_HALO + c * n:POOL_HALO + (c + 1) * n, :] = _dot(hns[c], w_u_ref[...])
    mqs = [_dot(hn, w_mq_ref[...]) for hn in hns]
    gates = [_silu(_dot(hn, w_gate_ref[...])) for hn in hns]
    toks = [_pooled_mix(ubuf, c * n, n, j * t + c * n) * scale_ref[...] for c in chunks]
    mem_os = [_memory_attention(mq, kbd_ref[0, 0], vbd_ref[0, 0]) for mq in mqs]
    for c in chunks:
        y = _dot((toks[c] * gates[c][:, :POOL_WIDTH]).astype(BF16), w_out_ref[0, :POOL_WIDTH, :])
        y += _dot((mem_os[c] * gates[c][:, POOL_WIDTH:]).astype(BF16),
                  w_out_ref[0, POOL_WIDTH:, :])
        o_ref[0, c * n:(c + 1) * n, :] = xs[c] + y


def _pool_layer(x, w_u, w_in, scale, kbd, vbd, w_out):
    b, s, _ = x.shape
    t = SEQ_TILE
    const = lambda shape: pl.BlockSpec(shape, lambda i, j: (0,) * len(shape))
    return pl.pallas_call(
        _pool_layer_kernel,
        grid=(b, s // t),
        in_specs=[
            pl.BlockSpec((1, t, D_MODEL), lambda i, j: (i, j, 0)),
            const(w_u.shape),
            pl.BlockSpec((D_MODEL, MEM_WIDTH), lambda i, j: (0, POOL_WIDTH // MEM_WIDTH)),
            pl.BlockSpec((D_MODEL, MIX_WIDTH), lambda i, j: (0, 1)),
            const(scale.shape),
            pl.BlockSpec((1, 1) + kbd.shape[2:], lambda i, j: (i, 0, 0, 0)),
            pl.BlockSpec((1, 1) + vbd.shape[2:], lambda i, j: (i, 0, 0, 0)),
            pl.BlockSpec((1,) + w_out.shape[1:], lambda i, j: (0, 0, 0)),
        ],
        out_specs=pl.BlockSpec((1, t, D_MODEL), lambda i, j: (i, j, 0)),
        out_shape=jax.ShapeDtypeStruct(x.shape, F32),
        scratch_shapes=[pltpu.VMEM((POOL_HALO + t, POOL_WIDTH), F32)],
        compiler_params=pltpu.CompilerParams(
            dimension_semantics=("parallel", "arbitrary"), vmem_limit_bytes=VMEM_LIMIT),
        name="pool_layer",
    )(x, w_u, w_in, w_in, scale, kbd, vbd, w_out)


def _rope_t(x, cos_t, sin_t):
    half = QK_ROPE // 2
    x1, x2 = x[:half], x[half:]
    return x1 * cos_t - x2 * sin_t, x2 * cos_t + x1 * sin_t


def _mla_project(tile, h_ref, pos_ref, inv_freq_ref, w_fm_ref, w_pm_ref, w_uq_t_ref, w_uk_ref,
                 w_uv_t_ref, kbd_ref, vbd_ref, w_out_ref, q_sc, k_sc, v_sc, sg_sc, hp_sc):
    w_cq_t_ref, w_kr_t_ref = w_fm_ref.at[0:Q_LORA], w_fm_ref.at[Q_LORA:Q_LORA + QK_ROPE]
    w_gate_ref = w_pm_ref.at[:, 0:MIX_WIDTH]
    w_ckv_ref = w_pm_ref.at[:, MIX_WIDTH:MIX_WIDTH + KV_LORA]
    w_mq_ref = w_pm_ref.at[:, MIX_WIDTH + KV_LORA:MIX_WIDTH + KV_LORA + MEM_WIDTH]
    w_out_mem_ref = w_out_ref.at[0, MLA_WIDTH:MIX_WIDTH]
    t = h_ref.shape[1]
    n = ROW_CHUNK
    chunks = range(t // n)
    heads = range(MLA_HEADS)
    half = QK_ROPE // 2
    rows = [slice(c * n, (c + 1) * n) for c in chunks]

    hs = [h_ref[0, rows[c], :] for c in chunks]
    xns = [_rms_scale(h) for h in hs]
    hns = [xn.astype(BF16) for xn in xns]
    mqs = [_dot(hn, w_mq_ref[...]) for hn in hns]
    mem_os = [_memory_attention(mq, kbd_ref[0, 0], vbd_ref[0, 0]) for mq in mqs]
    gates = [_silu(_dot(hn, w_gate_ref[...])) for hn in hns]
    for c in chunks:
        sg_sc[rows[c], :] = gates[c][:, :MLA_WIDTH]
        hp_sc[rows[c], :] = hs[c] + _dot(
            (mem_os[c] * gates[c][:, MLA_WIDTH:]).astype(BF16), w_out_mem_ref[...])

    hn_ts = [xn.T.astype(BF16) for xn in xns]
    cq_ts = [_dot(w_cq_t_ref[...], hn_t) for hn_t in hn_ts]
    kr_ts = [_dot(w_kr_t_ref[...], hn_t) for hn_t in hn_ts]
    ckvns = [_rms_scale(_dot(hn, w_ckv_ref[...])).astype(BF16) for hn in hns]
    angs = [pos_ref[0, :, rows[c]].astype(F32) * inv_freq_ref[...] for c in chunks]
    cos_ts, sin_ts = [jnp.cos(a) for a in angs], [jnp.sin(a) for a in angs]
    cqn_ts = [(cq * jax.lax.rsqrt(jnp.mean(cq * cq, axis=0, keepdims=True) + EPS)
               ).astype(BF16) for cq in cq_ts]

    q_ts = [_dot(w_uq_t_ref[...], cqn_t) for cqn_t in cqn_ts]
    for c in chunks:
        for hd in heads:
            base = hd * QK_DIM
            r1, r2 = _rope_t(q_ts[c][base + QK_NOPE:base + QK_DIM], cos_ts[c], sin_ts[c])
            q_sc[hd, 0:QK_NOPE, rows[c]] = q_ts[c][base:base + QK_NOPE].astype(BF16)
            q_sc[hd, QK_NOPE:QK_NOPE + half, rows[c]] = r1.astype(BF16)
            q_sc[hd, QK_NOPE + half:QK_DIM, rows[c]] = r2.astype(BF16)

    k_nopes = [_dot(ckvn, w_uk_ref[...]) for ckvn in ckvns]
    v_ts = [_dot_nt(w_uv_t_ref[...], ckvn) for ckvn in ckvns]
    for c in chunks:
        r1, r2 = _rope_t(kr_ts[c], cos_ts[c], sin_ts[c])
        k_rope = jnp.concatenate([r1, r2, jnp.zeros((LANES - QK_ROPE, n), F32)], axis=0).T
        k_rope = k_rope[:, :QK_ROPE]
        for hd in heads:
            k_sc[hd, tile * t + c * n:tile * t + (c + 1) * n, :] = jnp.concatenate(
                [k_nopes[c][:, hd * QK_NOPE:(hd + 1) * QK_NOPE], k_rope], axis=-1).astype(BF16)
            v_sc[hd, tile, 0:V_HEAD, rows[c]] = (
                v_ts[c][hd * V_HEAD:(hd + 1) * V_HEAD].astype(BF16))
            v_sc[hd, tile, V_HEAD:V_EXT, rows[c]] = jnp.ones((V_EXT - V_HEAD, n), BF16)


def _sublane_allreduce(x, op):
    for shift in (4, 2, 1):
        x = op(x, pltpu.roll(x, shift, 0))
    return x


def _mla_layer_kernel(h_ref, pos_ref, inv_freq_ref, w_fm_ref, w_pm_ref, w_uq_t_ref, w_uk_ref,
                      w_uv_t_ref, kbd_ref, vbd_ref, w_out_ref, g_ref, o_ref,
                      q_sc, k_sc, v_sc, sg_sc, hp_sc, acc_sc, tok_sc):
    i = pl.program_id(1)
    tq = tk = h_ref.shape[1]
    half = tk // 2
    heads = range(MLA_HEADS)

    def step(jt, carry, k0, kn, q0, qn, masked):
        m_old = carry
        ks = slice(jt * tk + k0, jt * tk + k0 + kn)
        qs = slice(q0, q0 + qn)
        s = [_dot(k_sc[hd, ks, :], q_sc[hd, :, qs]) for hd in heads]
        if masked:
            k_chunk = (k0 + jax.lax.broadcasted_iota(jnp.int32, (kn, qn), 0)) // CHUNK
            q_chunk = (q0 + jax.lax.broadcasted_iota(jnp.int32, (kn, qn), 1)) // CHUNK
            mask = k_chunk <= q_chunk
        m_new, alpha, p = [], [], []
        for hd in heads:
            sh = jnp.where(mask, s[hd], NEG_INF) if masked else s[hd]
            s3 = sh.reshape(kn // SUBLANES, SUBLANES, qn)
            m_prev = m_old[hd][:, qs]
            m = jnp.maximum(m_prev, _sublane_allreduce(jnp.max(s3, axis=0), jnp.maximum))
            alpha.append(jnp.exp2(m_prev - m))
            p.append(jnp.exp2(s3 - m[None]).reshape(kn, qn).astype(BF16))
            if qn != tq:
                m = jnp.concatenate(
                    ([m_old[hd][:, :q0]] if q0 else []) + [m]
                    + ([m_old[hd][:, q0 + qn:]] if q0 + qn < tq else []), axis=1)
            m_new.append(m)
        pv = [_dot(v_sc[hd, jt, :, k0:k0 + kn], p[hd]) for hd in heads]
        for hd in heads:
            acc = acc_sc[hd, :, qs].reshape(V_EXT // SUBLANES, SUBLANES, qn)
            acc_sc[hd, :, qs] = (alpha[hd][None] * acc).reshape(V_EXT, qn) + pv[hd]
        return tuple(m_new)

    def query_tile(n_full):
        _mla_project(n_full, h_ref, pos_ref, inv_freq_ref, w_fm_ref, w_pm_ref, w_uq_t_ref,
                     w_uk_ref, w_uv_t_ref, kbd_ref, vbd_ref, w_out_ref,
                     q_sc, k_sc, v_sc, sg_sc, hp_sc)
        acc_sc[...] = jnp.zeros(acc_sc.shape, F32)
        carry = tuple(jnp.full((SUBLANES, tq), NEG_INF, F32) for _ in heads)
        for jt in range(n_full):
            carry = step(jt, carry, 0, tk, 0, tq, False)
        carry = step(n_full, carry, 0, half, 0, tq, True)
        step(n_full, carry, half, tk - half, half, tq - half, True)

        for hd in heads:
            l_tot = acc_sc[hd, V_HEAD:V_HEAD + SUBLANES, :]
            o_t = acc_sc[hd, 0:V_HEAD, :].reshape(V_HEAD // SUBLANES, SUBLANES, tq) / l_tot[None]
            tok_sc[:, hd * V_HEAD:(hd + 1) * V_HEAD] = o_t.reshape(V_HEAD, tq).T

        n = tq // 2
        rows = [slice(c * n, (c + 1) * n) for c in range(tq // n)]
        branches = [(tok_sc[r, :] * sg_sc[r, :]).astype(BF16) for r in rows]
        outs = [hp_sc[r, :] + _dot(br, w_out_ref[0, 0:MLA_WIDTH, :])
                for r, br in zip(rows, branches)]
        for r, out in zip(rows, outs):
            o_ref[0, r, :] = _rms_norm(out, g_ref[...])

    for n_full in range(k_sc.shape[1] // tk):
        pl.when(i == n_full)(functools.partial(query_tile, n_full))


def _mla_layer(h, pos, inv_freq, w_fm, w_pm, w_uq_t, w_uk, w_uv_t, kbd, vbd, w_out, final_g):
    b, s, _ = h.shape
    t = ATTN_TILE
    const = lambda shape: pl.BlockSpec(shape, lambda i, j: (0,) * len(shape),
                                       pipeline_mode=pl.Buffered(1))
    row = lambda width: pl.BlockSpec((1, t, width), lambda i, j: (i, j, 0))
    return pl.pallas_call(
        _mla_layer_kernel,
        grid=(b, s // t),
        in_specs=[
            row(D_MODEL),
            pl.BlockSpec((1, 1, t), lambda i, j: (i, 0, j)), const(inv_freq.shape),
            const(w_fm.shape), const(w_pm.shape),
            const(w_uq_t.shape), const(w_uk.shape), const(w_uv_t.shape),
            pl.BlockSpec((1, 1) + kbd.shape[2:], lambda i, j: (i, 1, 0, 0)),
            pl.BlockSpec((1, 1) + vbd.shape[2:], lambda i, j: (i, 1, 0, 0)),
            pl.BlockSpec((1,) + w_out.shape[1:], lambda i, j: (1, 0, 0),
                         pipeline_mode=pl.Buffered(1)),
            const(final_g.shape),
        ],
        out_specs=row(D_MODEL),
        out_shape=jax.ShapeDtypeStruct((b, s, D_MODEL), F32),
        scratch_shapes=[
            pltpu.VMEM((MLA_HEADS, QK_DIM, t), BF16),
            pltpu.VMEM((MLA_HEADS, s, QK_DIM), BF16),
            pltpu.VMEM((MLA_HEADS, s // t, V_EXT, t), BF16),
            pltpu.VMEM((t, MLA_WIDTH), F32),
            pltpu.VMEM((t, D_MODEL), F32),
            pltpu.VMEM((MLA_HEADS, V_EXT, t), F32),
            pltpu.VMEM((t, MLA_WIDTH), F32),
        ],
        compiler_params=pltpu.CompilerParams(
            dimension_semantics=("parallel", "arbitrary"), vmem_limit_bytes=VMEM_LIMIT),
        name="mla_layer",
    )(h, pos, inv_freq, w_fm, w_pm, w_uq_t, w_uk, w_uv_t, kbd, vbd, w_out, final_g)


def kernel(x, mem, positions, norm_g, mem_norm_g, w_mem_kv, w_out, pool_w_in, pool_w_group,
           pool_scale, mla_w_in, mla_q_norm_g, mla_w_uq, mla_kv_norm_g, mla_w_ukv, final_norm_g):
    depth = w_mem_kv.shape[0]
    assert depth == 2 and pool_w_in.shape[0] == 1 and mla_w_in.shape[0] == 1
    bf = lambda a: a.astype(BF16)
    row = lambda a: a.reshape(1, -1)

    inv_freq = ROPE_THETA ** (-(jnp.arange(0, QK_ROPE, 2, dtype=F32) / QK_ROPE))
    kbd, vbd = _mem_kv(mem, row(mem_norm_g), bf(w_mem_kv))
    w_out_bf = bf(w_out)

    g0 = norm_g[0][:, None]
    mq_cols = (jnp.arange(POOL_IN) >= POOL_WIDTH) & (jnp.arange(POOL_IN) < POOL_WIDTH + MEM_WIDTH)
    pool_w = bf(pool_w_in[0] * g0 * jnp.where(mq_cols, MEM_Q_SCALE, 1.0))
    h1 = _pool_layer(x, _pool_fold(pool_w_in, g0, pool_w_group), pool_w, row(pool_scale[0]),
                     kbd, vbd, w_out_bf)

    mw = mla_w_in[0] * norm_g[1][:, None]
    o_ckv = Q_LORA
    o_kr = o_ckv + KV_LORA
    o_mq = o_kr + QK_ROPE
    o_gate = o_mq + MEM_WIDTH
    w_fm = bf(jnp.concatenate([mw[:, :o_ckv], mw[:, o_kr:o_mq]], axis=1).T)
    w_pm = bf(jnp.concatenate(
        [mw[:, o_gate:], mw[:, o_ckv:o_kr], mw[:, o_mq:o_gate] * MEM_Q_SCALE], axis=1))
    w_ukv = (mla_w_ukv[0] * mla_kv_norm_g[0][:, None]).reshape(
        KV_LORA, MLA_HEADS, QK_NOPE + V_HEAD)
    w_uk = bf(w_ukv[:, :, :QK_NOPE].reshape(KV_LORA, MLA_HEADS * QK_NOPE))
    w_uv_t = bf(w_ukv[:, :, QK_NOPE:].reshape(KV_LORA, MLA_WIDTH).T)
    w_uq_t = bf((mla_w_uq[0] * (mla_q_norm_g[0][:, None] * Q_SCALE)).T)
    return _mla_layer(h1, positions[:, None, :], inv_freq[:, None], w_fm, w_pm, w_uq_t, w_uk,
                      w_uv_t, kbd, vbd, w_out_bf, row(final_norm_g))
```

```python
import functools
import math

import jax
import jax.numpy as jnp
from jax.experimental import pallas as pl
from jax.experimental.pallas import tpu as pltpu

D_MODEL = 1024
CHUNK = 64
EPS = 1e-6
NEG_INF = -1e30

N_MEM = 256
MEM_HEADS = 4
MEM_HEAD_DIM = 64
MEM_WIDTH = MEM_HEADS * MEM_HEAD_DIM
MEM_BATCH = 4

POOL_WINDOWS = (2, 4, 8, 16)
POOL_GROUP_DIM = 192
POOL_WIDTH = len(POOL_WINDOWS) * POOL_GROUP_DIM
POOL_HALO = 16

MLA_HEADS = 6
QK_NOPE = 128
QK_ROPE = 64
QK_DIM = QK_NOPE + QK_ROPE
V_HEAD = 128
V_EXT = V_HEAD + 16
Q_LORA = 384
KV_LORA = 256
MLA_WIDTH = MLA_HEADS * V_HEAD
ROPE_THETA = 10000.0
LOG2_E = math.log2(math.e)
Q_SCALE = QK_DIM ** -0.5 * LOG2_E
MEM_Q_SCALE = MEM_HEAD_DIM ** -0.5 * LOG2_E

MIX_WIDTH = POOL_WIDTH + MEM_WIDTH
POOL_IN = POOL_WIDTH + MEM_WIDTH + MIX_WIDTH

LANES = 128
SUBLANES = 8
SEQ_TILE = 1024
ROW_CHUNK = 256
ATTN_TILE = 512
VMEM_LIMIT = 56 * 1024 * 1024

BF16 = jnp.bfloat16
F32 = jnp.float32


def _rms_scale(x):
    ms = jnp.mean(x * x, axis=-1, keepdims=True)
    return x * jax.lax.rsqrt(ms + EPS)


def _rms_norm(x, g):
    return _rms_scale(x) * g


def _silu(x):
    return x / (1.0 + jnp.exp2(x * -LOG2_E))


def _dot(a, b):
    return jnp.dot(a, b, preferred_element_type=F32)


def _dot_nt(a, b):
    return jax.lax.dot_general(a, b, (((1,), (1,)), ((), ())), preferred_element_type=F32)


def _memory_attention(mq, kbd, vbd):
    s = _dot(mq.astype(BF16), kbd)
    probs = []
    for h in range(MEM_HEADS):
        sh = s[:, h * N_MEM:(h + 1) * N_MEM]
        e = jnp.exp2(sh - jnp.max(sh, axis=-1, keepdims=True))
        probs.append((e / jnp.sum(e, axis=-1, keepdims=True)).astype(BF16))
    return _dot(jnp.concatenate(probs, axis=-1), vbd)


def _mem_kv_kernel(mem_ref, g_ref, w_ref, kbd_ref, vbd_ref):
    rows = range(mem_ref.shape[0])
    mem_ns = [_rms_norm(mem_ref[r], g_ref[...]).astype(BF16) for r in rows]
    row_head = jax.lax.broadcasted_iota(jnp.int32, (MEM_WIDTH, N_MEM), 0) // MEM_HEAD_DIM
    col_head = jax.lax.broadcasted_iota(jnp.int32, (N_MEM, MEM_WIDTH), 1) // MEM_HEAD_DIM
    for layer in range(w_ref.shape[0]):
        kvs = [_dot(mem_n, w_ref[layer]) for mem_n in mem_ns]
        for r in rows:
            k_t = kvs[r][:, :MEM_WIDTH].T
            v = kvs[r][:, MEM_WIDTH:]
            for h in range(MEM_HEADS):
                kbd_ref[r, layer, :, h * N_MEM:(h + 1) * N_MEM] = jnp.where(
                    row_head == h, k_t, 0.0).astype(BF16)
                vbd_ref[r, layer, h * N_MEM:(h + 1) * N_MEM, :] = jnp.where(
                    col_head == h, v, 0.0).astype(BF16)


def _mem_kv(mem, mem_norm_g, w):
    b = mem.shape[0]
    n_layers = w.shape[0]
    nb = MEM_BATCH
    return pl.pallas_call(
        _mem_kv_kernel,
        grid=(b // nb,),
        in_specs=[
            pl.BlockSpec((nb, N_MEM, D_MODEL), lambda i: (i, 0, 0)),
            pl.BlockSpec((1, D_MODEL), lambda i: (0, 0)),
            pl.BlockSpec(w.shape, lambda i: (0, 0, 0)),
        ],
        out_specs=[
            pl.BlockSpec((nb, n_layers, MEM_WIDTH, MEM_HEADS * N_MEM), lambda i: (i, 0, 0, 0)),
            pl.BlockSpec((nb, n_layers, MEM_HEADS * N_MEM, MEM_WIDTH), lambda i: (i, 0, 0, 0)),
        ],
        out_shape=[
            jax.ShapeDtypeStruct((b, n_layers, MEM_WIDTH, MEM_HEADS * N_MEM), BF16),
            jax.ShapeDtypeStruct((b, n_layers, MEM_HEADS * N_MEM, MEM_WIDTH), BF16),
        ],
        compiler_params=pltpu.CompilerParams(
            dimension_semantics=("parallel",), vmem_limit_bytes=VMEM_LIMIT),
        name="mem_kv",
    )(mem, mem_norm_g, w)


def _split_bf16(a):
    hi = a.astype(BF16)
    return hi, (a - hi.astype(F32)).astype(BF16)


def _pool_fold_kernel(w_u_ref, g_ref, w_grp_ref, o_ref):
    grp = [_split_bf16(w_grp_ref[0, gi]) for gi in range(len(POOL_WINDOWS))]
    n = ROW_CHUNK
    for c in range(w_u_ref.shape[1] // n):
        a = w_u_ref[0, c * n:(c + 1) * n, :] * g_ref[c * n:(c + 1) * n, :]
        cols = []
        for gi, (b_hi, b_lo) in enumerate(grp):
            a_hi, a_lo = _split_bf16(a[:, gi * POOL_GROUP_DIM:(gi + 1) * POOL_GROUP_DIM])
            cols.append(_dot(a_hi, b_hi) + _dot(a_hi, b_lo) + _dot(a_lo, b_hi))
        o_ref[c * n:(c + 1) * n, :] = jnp.concatenate(cols, axis=-1).astype(BF16)


def _pool_fold(pool_w_in, g_col, pool_w_group):
    return pl.pallas_call(
        _pool_fold_kernel,
        grid=(1,),
        in_specs=[
            pl.BlockSpec((1, D_MODEL, POOL_WIDTH), lambda i: (0, 0, 0)),
            pl.BlockSpec(g_col.shape, lambda i: (0, 0)),
            pl.BlockSpec(pool_w_group.shape, lambda i: (0, 0, 0, 0)),
        ],
        out_specs=pl.BlockSpec((D_MODEL, POOL_WIDTH), lambda i: (0, 0)),
        out_shape=jax.ShapeDtypeStruct((D_MODEL, POOL_WIDTH), BF16),
        compiler_params=pltpu.CompilerParams(vmem_limit_bytes=VMEM_LIMIT),
        name="pool_fold",
    )(pool_w_in, g_col, pool_w_group)


def _window_sums(a):
    s2 = a + pltpu.roll(a, 1, 0)
    s4 = s2 + pltpu.roll(s2, 2, 0)
    s8 = s4 + pltpu.roll(s4, 4, 0)
    s16 = s8 + pltpu.roll(s8, 8, 0)
    return {2: s2, 4: s4, 8: s8, 16: s16}


def _pooled_mix(ubuf, row0, n, pos0):
    pos = pos0 + jax.lax.broadcasted_iota(jnp.int32, (n, 1), 0)
    inv_cnt = {w: 1.0 / jnp.minimum(pos + 1, w).astype(F32) for w in POOL_WINDOWS}
    lane = jax.lax.broadcasted_iota(jnp.int32, (n, LANES), 1)
    mixed = []
    for c in range(POOL_WIDTH // LANES):
        cols = slice(c * LANES, (c + 1) * LANES)
        sums = _window_sums(ubuf[row0:row0 + POOL_HALO + n, cols])
        lo_grp = (c * LANES) // POOL_GROUP_DIM
        hi_grp = ((c + 1) * LANES - 1) // POOL_GROUP_DIM
        w_lo, w_hi = POOL_WINDOWS[lo_grp], POOL_WINDOWS[hi_grp]
        pooled = sums[w_lo][POOL_HALO:] * inv_cnt[w_lo]
        if hi_grp != lo_grp:
            split = hi_grp * POOL_GROUP_DIM - c * LANES
            pooled = jnp.where(lane < split, pooled, sums[w_hi][POOL_HALO:] * inv_cnt[w_hi])
        mixed.append(pooled - ubuf[POOL_HALO + row0:POOL_HALO + row0 + n, cols])
    return jnp.concatenate(mixed, axis=-1)


def _pool_layer_kernel(x_ref, w_u_ref, w_mq_ref, w_gate_ref, scale_ref,
                       kbd_ref, vbd_ref, w_out_ref, o_ref, ubuf):
    j = pl.program_id(1)
    t = x_ref.shape[1]
    n = ROW_CHUNK
    chunks = range(t // n)

    @pl.when(j == 0)
    def _():
        ubuf[0:POOL_HALO, :] = jnp.zeros((POOL_HALO, POOL_WIDTH), F32)

    @pl.when(j > 0)
    def _():
        ubuf[0:POOL_HALO, :] = ubuf[t:t + POOL_HALO, :]

    xs = [x_ref[0, c * n:(c + 1) * n, :] for c in chunks]
    hns = [_rms_scale(x).astype(BF16) for x in xs]
    for c in chunks:
        ubuf[POOL_HALO + c * n:POOL_HALO + (c + 1) * n, :] = _dot(hns[c], w_u_ref[...])
    mqs = [_dot(hn, w_mq_ref[...]) for hn in hns]
    gates = [_silu(_dot(hn, w_gate_ref[...])) for hn in hns]
    toks = [_pooled_mix(ubuf, c * n, n, j * t + c * n) * scale_ref[...] for c in chunks]
    mem_os = [_memory_attention(mq, kbd_ref[0, 0], vbd_ref[0, 0]) for mq in mqs]
    for c in chunks:
        y = _dot((toks[c] * gates[c][:, :POOL_WIDTH]).astype(BF16), w_out_ref[0, :POOL_WIDTH, :])
        y += _dot((mem_os[c] * gates[c][:, POOL_WIDTH:]).astype(BF16),
                  w_out_ref[0, POOL_WIDTH:, :])
        o_ref[0, c * n:(c + 1) * n, :] = xs[c] + y


def _pool_layer(x, w_u, w_in, scale, kbd, vbd, w_out):
    b, s, _ = x.shape
    t = SEQ_TILE
    const = lambda shape: pl.BlockSpec(shape, lambda i, j: (0,) * len(shape))
    return pl.pallas_call(
        _pool_layer_kernel,
        grid=(b, s // t),
        in_specs=[
            pl.BlockSpec((1, t, D_MODEL), lambda i, j: (i, j, 0)),
            const(w_u.shape),
            pl.BlockSpec((D_MODEL, MEM_WIDTH), lambda i, j: (0, POOL_WIDTH // MEM_WIDTH)),
            pl.BlockSpec((D_MODEL, MIX_WIDTH), lambda i, j: (0, 1)),
            const(scale.shape),
            pl.BlockSpec((1, 1) + kbd.shape[2:], lambda i, j: (i, 0, 0, 0)),
            pl.BlockSpec((1, 1) + vbd.shape[2:], lambda i, j: (i, 0, 0, 0)),
            pl.BlockSpec((1,) + w_out.shape[1:], lambda i, j: (0, 0, 0)),
        ],
        out_specs=pl.BlockSpec((1, t, D_MODEL), lambda i, j: (i, j, 0)),
        out_shape=jax.ShapeDtypeStruct(x.shape, F32),
        scratch_shapes=[pltpu.VMEM((POOL_HALO + t, POOL_WIDTH), F32)],
        compiler_params=pltpu.CompilerParams(
            dimension_semantics=("parallel", "arbitrary"), vmem_limit_bytes=VMEM_LIMIT),
        name="pool_layer",
    )(x, w_u, w_in, w_in, scale, kbd, vbd, w_out)


def _rope_t(x, cos_t, sin_t):
    half = QK_ROPE // 2
    x1, x2 = x[:half], x[half:]
    return x1 * cos_t - x2 * sin_t, x2 * cos_t + x1 * sin_t


def _mla_pre_kernel(h_ref, pos_ref, inv_freq_ref, w_fm_src_ref, w_pm_ref, w_uq_t_ref, w_uk_ref,
                    w_uv_t_ref, kbd_ref, vbd_ref, w_out_ref,
                    q_ref, k_ref, v_ref, sg_ref, hp_ref, w_fm_ref):
    @pl.when((pl.program_id(0) == 0) & (pl.program_id(1) == 0))
    def _():
        for blk in range(w_fm_src_ref.shape[1] // LANES):
            cols = slice(blk * LANES, (blk + 1) * LANES)
            w_fm_ref[cols, :] = w_fm_src_ref[:, cols].astype(F32).T.astype(BF16)

    w_cq_t_ref, w_kr_t_ref = w_fm_ref.at[0:Q_LORA], w_fm_ref.at[Q_LORA:Q_LORA + QK_ROPE]
    w_gate_ref = w_pm_ref.at[:, 0:MIX_WIDTH]
    w_ckv_ref = w_pm_ref.at[:, MIX_WIDTH:MIX_WIDTH + KV_LORA]
    w_mq_ref = w_pm_ref.at[:, MIX_WIDTH + KV_LORA:MIX_WIDTH + KV_LORA + MEM_WIDTH]
    w_out_mem_ref = w_out_ref.at[0]
    t = h_ref.shape[1]
    n = ROW_CHUNK
    chunks = range(t // n)
    heads = range(MLA_HEADS)
    half = QK_ROPE // 2
    rows = [slice(c * n, (c + 1) * n) for c in chunks]

    hs = [h_ref[0, rows[c], :] for c in chunks]
    xns = [_rms_scale(h) for h in hs]
    hns = [xn.astype(BF16) for xn in xns]
    mqs = [_dot(hn, w_mq_ref[...]) for hn in hns]
    mem_os = [_memory_attention(mq, kbd_ref[0, 0], vbd_ref[0, 0]) for mq in mqs]
    gates = [_silu(_dot(hn, w_gate_ref[...])) for hn in hns]
    for c in chunks:
        sg_ref[0, rows[c], :] = gates[c][:, :MLA_WIDTH]
        hp_ref[0, rows[c], :] = hs[c] + _dot(
            (mem_os[c] * gates[c][:, MLA_WIDTH:]).astype(BF16), w_out_mem_ref[...])

    hn_ts = [xn.T.astype(BF16) for xn in xns]
    cq_ts = [_dot(w_cq_t_ref[...], hn_t) for hn_t in hn_ts]
    kr_ts = [_dot(w_kr_t_ref[...], hn_t) for hn_t in hn_ts]
    ckvns = [_rms_scale(_dot(hn, w_ckv_ref[...])).astype(BF16) for hn in hns]
    angs = [pos_ref[0, :, rows[c]].astype(F32) * inv_freq_ref[...] for c in chunks]
    cos_ts, sin_ts = [jnp.cos(a) for a in angs], [jnp.sin(a) for a in angs]
    cqn_ts = [(cq * jax.lax.rsqrt(jnp.mean(cq * cq, axis=0, keepdims=True) + EPS)
               ).astype(BF16) for cq in cq_ts]

    q_ts = [_dot(w_uq_t_ref[...], cqn_t) for cqn_t in cqn_ts]
    for c in chunks:
        for hd in heads:
            base = hd * QK_DIM
            r1, r2 = _rope_t(q_ts[c][base + QK_NOPE:base + QK_DIM], cos_ts[c], sin_ts[c])
            q_ref[0, hd, 0:QK_NOPE, rows[c]] = q_ts[c][base:base + QK_NOPE].astype(BF16)
            q_ref[0, hd, QK_NOPE:QK_NOPE + half, rows[c]] = r1.astype(BF16)
            q_ref[0, hd, QK_NOPE + half:QK_DIM, rows[c]] = r2.astype(BF16)

    k_nopes = [_dot(ckvn, w_uk_ref[...]) for ckvn in ckvns]
    v_ts = [_dot_nt(w_uv_t_ref[...], ckvn) for ckvn in ckvns]
    for c in chunks:
        r1, r2 = _rope_t(kr_ts[c], cos_ts[c], sin_ts[c])
        k_rope = jnp.concatenate([r1, r2, jnp.zeros((LANES - QK_ROPE, n), F32)], axis=0).T
        k_rope = k_rope[:, :QK_ROPE]
        kt, off = divmod(c * n, ATTN_TILE)
        for hd in heads:
            k_ref[0, hd, rows[c], :] = jnp.concatenate(
                [k_nopes[c][:, hd * QK_NOPE:(hd + 1) * QK_NOPE], k_rope], axis=-1).astype(BF16)
            v_ref[0, hd, kt, 0:V_HEAD, off:off + n] = (
                v_ts[c][hd * V_HEAD:(hd + 1) * V_HEAD].astype(BF16))
            v_ref[0, hd, kt, V_HEAD:V_EXT, off:off + n] = jnp.ones((V_EXT - V_HEAD, n), BF16)


def _mla_pre(h, pos, inv_freq, w_fm, w_pm, w_uq_t, w_uk, w_uv_t, kbd, vbd, w_out):
    b, s, _ = h.shape
    t = SEQ_TILE
    n_kt = t // ATTN_TILE
    const = lambda shape: pl.BlockSpec(shape, lambda i, j: (0,) * len(shape))
    row = lambda width: pl.BlockSpec((1, t, width), lambda i, j: (i, j, 0))
    return pl.pallas_call(
        _mla_pre_kernel,
        grid=(b, s // t),
        in_specs=[
            row(D_MODEL),
            pl.BlockSpec((1, 1, t), lambda i, j: (i, 0, j)), const(inv_freq.shape),
            const(w_fm.shape), const(w_pm.shape),
            const(w_uq_t.shape), const(w_uk.shape), const(w_uv_t.shape),
            pl.BlockSpec((1, 1) + kbd.shape[2:], lambda i, j: (i, 1, 0, 0)),
            pl.BlockSpec((1, 1) + vbd.shape[2:], lambda i, j: (i, 1, 0, 0)),
            pl.BlockSpec((1, MEM_WIDTH, D_MODEL), lambda i, j: (1, MLA_WIDTH // MEM_WIDTH, 0)),
        ],
        out_specs=[
            pl.BlockSpec((1, MLA_HEADS, QK_DIM, t), lambda i, j: (i, 0, 0, j)),
            pl.BlockSpec((1, MLA_HEADS, t, QK_DIM), lambda i, j: (i, 0, j, 0)),
            pl.BlockSpec((1, MLA_HEADS, n_kt, V_EXT, ATTN_TILE), lambda i, j: (i, 0, j, 0, 0)),
            row(MLA_WIDTH), row(D_MODEL),
        ],
        out_shape=[
            jax.ShapeDtypeStruct((b, MLA_HEADS, QK_DIM, s), BF16),
            jax.ShapeDtypeStruct((b, MLA_HEADS, s, QK_DIM), BF16),
            jax.ShapeDtypeStruct((b, MLA_HEADS, s // ATTN_TILE, V_EXT, ATTN_TILE), BF16),
            jax.ShapeDtypeStruct((b, s, MLA_WIDTH), F32),
            jax.ShapeDtypeStruct((b, s, D_MODEL), F32),
        ],
        scratch_shapes=[pltpu.VMEM(w_fm.shape[::-1], BF16)],
        compiler_params=pltpu.CompilerParams(
            dimension_semantics=("arbitrary", "arbitrary"), vmem_limit_bytes=VMEM_LIMIT),
        name="mla_pre",
    )(h, pos, inv_freq, w_fm, w_pm, w_uq_t, w_uk, w_uv_t, kbd, vbd, w_out)


def _sublane_allreduce(x, op):
    for shift in (4, 2, 1):
        x = op(x, pltpu.roll(x, shift, 0))
    return x


def _mla_attn_kernel(q_ref, k_ref, v_ref, sg_ref, hp_ref, w_out_ref, g_ref, o_ref,
                     acc_sc, tok_sc):
    i = pl.program_id(1)
    tq = q_ref.shape[3]
    tk = v_ref.shape[4]
    half = tk // 2
    heads = range(MLA_HEADS)

    def step(jt, carry, k0, kn, q0, qn, masked):
        m_old = carry
        ks = slice(jt * tk + k0, jt * tk + k0 + kn)
        qs = slice(q0, q0 + qn)
        s = [_dot(k_ref[0, hd, ks, :], q_ref[0, hd, :, qs]) for hd in heads]
        if masked:
            k_chunk = (k0 + jax.lax.broadcasted_iota(jnp.int32, (kn, qn), 0)) // CHUNK
            q_chunk = (q0 + jax.lax.broadcasted_iota(jnp.int32, (kn, qn), 1)) // CHUNK
            mask = k_chunk <= q_chunk
        m_new, alpha, p = [], [], []
        for hd in heads:
            sh = jnp.where(mask, s[hd], NEG_INF) if masked else s[hd]
            s3 = sh.reshape(kn // SUBLANES, SUBLANES, qn)
            m_prev = m_old[hd][:, qs]
            m = jnp.maximum(m_prev, _sublane_allreduce(jnp.max(s3, axis=0), jnp.maximum))
            alpha.append(jnp.exp2(m_prev - m))
            p.append(jnp.exp2(s3 - m[None]).reshape(kn, qn).astype(BF16))
            if qn != tq:
                m = jnp.concatenate(
                    ([m_old[hd][:, :q0]] if q0 else []) + [m]
                    + ([m_old[hd][:, q0 + qn:]] if q0 + qn < tq else []), axis=1)
            m_new.append(m)
        pv = [_dot(v_ref[0, hd, jt, :, k0:k0 + kn], p[hd]) for hd in heads]
        for hd in heads:
            acc = acc_sc[hd, :, qs].reshape(V_EXT // SUBLANES, SUBLANES, qn)
            acc_sc[hd, :, qs] = (alpha[hd][None] * acc).reshape(V_EXT, qn) + pv[hd]
        return tuple(m_new)

    def query_tile(n_full):
        acc_sc[...] = jnp.zeros(acc_sc.shape, F32)
        carry = tuple(jnp.full((SUBLANES, tq), NEG_INF, F32) for _ in heads)
        for jt in range(n_full):
            carry = step(jt, carry, 0, tk, 0, tq, False)
        carry = step(n_full, carry, 0, half, 0, tq, True)
        step(n_full, carry, half, tk - half, half, tq - half, True)

        for hd in heads:
            l_tot = acc_sc[hd, V_HEAD:V_HEAD + SUBLANES, :]
            o_t = acc_sc[hd, 0:V_HEAD, :].reshape(V_HEAD // SUBLANES, SUBLANES, tq) / l_tot[None]
            tok_sc[:, hd * V_HEAD:(hd + 1) * V_HEAD] = o_t.reshape(V_HEAD, tq).T

        n = tq // 2
        rows = [slice(c * n, (c + 1) * n) for c in range(tq // n)]
        branches = [(tok_sc[r, :] * sg_ref[0, r, :]).astype(BF16) for r in rows]
        outs = [hp_ref[0, r, :] + _dot(br, w_out_ref[0]) for r, br in zip(rows, branches)]
        for r, out in zip(rows, outs):
            o_ref[0, r, :] = _rms_norm(out, g_ref[...])

    for n_full in range(k_ref.shape[2] // tk):
        pl.when(i == n_full)(functools.partial(query_tile, n_full))


def _mla_attn(q_t, k, v_t, sg, hp, w_out, final_g):
    b, _, s, _ = k.shape
    t = ATTN_TILE
    const = lambda shape: pl.BlockSpec(shape, lambda i, j: (0,) * len(shape))
    row = lambda width: pl.BlockSpec((1, t, width), lambda i, j: (i, j, 0))
    return pl.pallas_call(
        _mla_attn_kernel,
        grid=(b, s // t),
        in_specs=[
            pl.BlockSpec((1, MLA_HEADS, QK_DIM, t), lambda i, j: (i, 0, 0, j)),
            pl.BlockSpec((1, MLA_HEADS, s, QK_DIM), lambda i, j: (i, 0, 0, 0)),
            pl.BlockSpec((1,) + v_t.shape[1:], lambda i, j: (i, 0, 0, 0, 0)),
            row(MLA_WIDTH), row(D_MODEL),
            pl.BlockSpec((1, MLA_WIDTH, D_MODEL), lambda i, j: (1, 0, 0)), const(final_g.shape),
        ],
        out_specs=row(D_MODEL),
        out_shape=jax.ShapeDtypeStruct((b, s, D_MODEL), F32),
        scratch_shapes=[
            pltpu.VMEM((MLA_HEADS, V_EXT, t), F32),
            pltpu.VMEM((t, MLA_WIDTH), F32),
        ],
        compiler_params=pltpu.CompilerParams(
            dimension_semantics=("parallel", "arbitrary"), vmem_limit_bytes=VMEM_LIMIT),
        name="mla_attn",
    )(q_t, k, v_t, sg, hp, w_out, final_g)


def kernel(x, mem, positions, norm_g, mem_norm_g, w_mem_kv, w_out, pool_w_in, pool_w_group,
           pool_scale, mla_w_in, mla_q_norm_g, mla_w_uq, mla_kv_norm_g, mla_w_ukv, final_norm_g):
    depth = w_mem_kv.shape[0]
    assert depth == 2 and pool_w_in.shape[0] == 1 and mla_w_in.shape[0] == 1
    bf = lambda a: a.astype(BF16)
    row = lambda a: a.reshape(1, -1)

    inv_freq = ROPE_THETA ** (-(jnp.arange(0, QK_ROPE, 2, dtype=F32) / QK_ROPE))
    kbd, vbd = _mem_kv(mem, row(mem_norm_g), bf(w_mem_kv))
    w_out_bf = bf(w_out)

    g0 = norm_g[0][:, None]
    mq_cols = (jnp.arange(POOL_IN) >= POOL_WIDTH) & (jnp.arange(POOL_IN) < POOL_WIDTH + MEM_WIDTH)
    pool_w = bf(pool_w_in[0] * g0 * jnp.where(mq_cols, MEM_Q_SCALE, 1.0))
    h1 = _pool_layer(x, _pool_fold(pool_w_in, g0, pool_w_group), pool_w, row(pool_scale[0]),
                     kbd, vbd, w_out_bf)

    mw = mla_w_in[0] * norm_g[1][:, None]
    o_ckv = Q_LORA
    o_kr = o_ckv + KV_LORA
    o_mq = o_kr + QK_ROPE
    o_gate = o_mq + MEM_WIDTH
    w_fm = bf(jnp.concatenate(
        [mw[:, :o_ckv], mw[:, o_kr:o_mq], jnp.zeros((D_MODEL, LANES - QK_ROPE), F32)], axis=1))
    w_pm = bf(jnp.concatenate(
        [mw[:, o_gate:], mw[:, o_ckv:o_kr], mw[:, o_mq:o_gate] * MEM_Q_SCALE], axis=1))
    w_ukv = (mla_w_ukv[0] * mla_kv_norm_g[0][:, None]).reshape(
        KV_LORA, MLA_HEADS, QK_NOPE + V_HEAD)
    w_uk = bf(w_ukv[:, :, :QK_NOPE].reshape(KV_LORA, MLA_HEADS * QK_NOPE))
    w_uv_t = bf(w_ukv[:, :, QK_NOPE:].reshape(KV_LORA, MLA_WIDTH).T)
    w_uq_t = bf((mla_w_uq[0] * (mla_q_norm_g[0][:, None] * Q_SCALE)).T)
    q_t, k, v_t, sg, hp = _mla_pre(
        h1, positions[:, None, :], inv_freq[:, None], w_fm, w_pm, w_uq_t, w_uk, w_uv_t,
        kbd, vbd, w_out_bf)

    return _mla_attn(q_t, k, v_t, sg, hp, w_out_bf, row(final_norm_g))
```

```python
import functools
import math

import jax
import jax.numpy as jnp
from jax.experimental import pallas as pl
from jax.experimental.pallas import tpu as pltpu

D_MODEL = 1024
CHUNK = 64
EPS = 1e-6
NEG_INF = -1e30

N_MEM = 256
MEM_HEADS = 4
MEM_HEAD_DIM = 64
MEM_WIDTH = MEM_HEADS * MEM_HEAD_DIM
MEM_BATCH = 4

POOL_WINDOWS = (2, 4, 8, 16)
POOL_GROUP_DIM = 192
POOL_WIDTH = len(POOL_WINDOWS) * POOL_GROUP_DIM
POOL_HALO = 16

MLA_HEADS = 6
QK_NOPE = 128
QK_ROPE = 64
QK_DIM = QK_NOPE + QK_ROPE
V_HEAD = 128
V_EXT = V_HEAD + 16
Q_LORA = 384
KV_LORA = 256
MLA_WIDTH = MLA_HEADS * V_HEAD
ROPE_THETA = 10000.0
LOG2_E = math.log2(math.e)
Q_SCALE = QK_DIM ** -0.5 * LOG2_E
MEM_Q_SCALE = MEM_HEAD_DIM ** -0.5 * LOG2_E

MIX_WIDTH = POOL_WIDTH + MEM_WIDTH
POOL_IN = POOL_WIDTH + MEM_WIDTH + MIX_WIDTH

LANES = 128
SUBLANES = 8
SEQ_TILE = 1024
ROW_CHUNK = 256
ATTN_TILE = 512
VMEM_LIMIT = 56 * 1024 * 1024

BF16 = jnp.bfloat16
F32 = jnp.float32


def _rms_scale(x):
    ms = jnp.mean(x * x, axis=-1, keepdims=True)
    return x * jax.lax.rsqrt(ms + EPS)


def _rms_norm(x, g):
    return _rms_scale(x) * g


def _silu(x):
    return x / (1.0 + jnp.exp2(x * -LOG2_E))


def _dot(a, b):
    return jnp.dot(a, b, preferred_element_type=F32)


def _dot_nt(a, b):
    return jax.lax.dot_general(a, b, (((1,), (1,)), ((), ())), preferred_element_type=F32)


def _memory_attention(mq, kbd, vbd):
    s = _dot(mq.astype(BF16), kbd)
    probs = []
    for h in range(MEM_HEADS):
        sh = s[:, h * N_MEM:(h + 1) * N_MEM]
        e = jnp.exp2(sh - jnp.max(sh, axis=-1, keepdims=True))
        probs.append((e / jnp.sum(e, axis=-1, keepdims=True)).astype(BF16))
    return _dot(jnp.concatenate(probs, axis=-1), vbd)


def _mem_kv_kernel(mem_ref, g_ref, w_ref, kbd_ref, vbd_ref):
    rows = range(mem_ref.shape[0])
    mem_ns = [_rms_norm(mem_ref[r], g_ref[...]).astype(BF16) for r in rows]
    row_head = jax.lax.broadcasted_iota(jnp.int32, (MEM_WIDTH, N_MEM), 0) // MEM_HEAD_DIM
    col_head = jax.lax.broadcasted_iota(jnp.int32, (N_MEM, MEM_WIDTH), 1) // MEM_HEAD_DIM
    for layer in range(w_ref.shape[0]):
        kvs = [_dot(mem_n, w_ref[layer]) for mem_n in mem_ns]
        for r in rows:
            k_t = kvs[r][:, :MEM_WIDTH].T
            v = kvs[r][:, MEM_WIDTH:]
            for h in range(MEM_HEADS):
                kbd_ref[r, layer, :, h * N_MEM:(h + 1) * N_MEM] = jnp.where(
                    row_head == h, k_t, 0.0).astype(BF16)
                vbd_ref[r, layer, h * N_MEM:(h + 1) * N_MEM, :] = jnp.where(
                    col_head == h, v, 0.0).astype(BF16)


def _mem_kv(mem, mem_norm_g, w):
    b = mem.shape[0]
    n_layers = w.shape[0]
    nb = MEM_BATCH
    return pl.pallas_call(
        _mem_kv_kernel,
        grid=(b // nb,),
        in_specs=[
            pl.BlockSpec((nb, N_MEM, D_MODEL), lambda i: (i, 0, 0)),
            pl.BlockSpec((1, D_MODEL), lambda i: (0, 0)),
            pl.BlockSpec(w.shape, lambda i: (0, 0, 0)),
        ],
        out_specs=[
            pl.BlockSpec((nb, n_layers, MEM_WIDTH, MEM_HEADS * N_MEM), lambda i: (i, 0, 0, 0)),
            pl.BlockSpec((nb, n_layers, MEM_HEADS * N_MEM, MEM_WIDTH), lambda i: (i, 0, 0, 0)),
        ],
        out_shape=[
            jax.ShapeDtypeStruct((b, n_layers, MEM_WIDTH, MEM_HEADS * N_MEM), BF16),
            jax.ShapeDtypeStruct((b, n_layers, MEM_HEADS * N_MEM, MEM_WIDTH), BF16),
        ],
        compiler_params=pltpu.CompilerParams(
            dimension_semantics=("parallel",), vmem_limit_bytes=VMEM_LIMIT),
        name="mem_kv",
    )(mem, mem_norm_g, w)


def _split_bf16(a):
    hi = a.astype(BF16)
    return hi, (a - hi.astype(F32)).astype(BF16)


def _pool_fold_kernel(w_u_ref, g_ref, w_grp_ref, o_ref):
    grp = [_split_bf16(w_grp_ref[0, gi]) for gi in range(len(POOL_WINDOWS))]
    n = ROW_CHUNK
    for c in range(w_u_ref.shape[1] // n):
        a = w_u_ref[0, c * n:(c + 1) * n, :] * g_ref[c * n:(c + 1) * n, :]
        cols = []
        for gi, (b_hi, b_lo) in enumerate(grp):
            a_hi, a_lo = _split_bf16(a[:, gi * POOL_GROUP_DIM:(gi + 1) * POOL_GROUP_DIM])
            cols.append(_dot(a_hi, b_hi) + _dot(a_hi, b_lo) + _dot(a_lo, b_hi))
        o_ref[c * n:(c + 1) * n, :] = jnp.concatenate(cols, axis=-1).astype(BF16)


def _pool_fold(pool_w_in, g_col, pool_w_group):
    return pl.pallas_call(
        _pool_fold_kernel,
        grid=(1,),
        in_specs=[
            pl.BlockSpec((1, D_MODEL, POOL_WIDTH), lambda i: (0, 0, 0)),
            pl.BlockSpec(g_col.shape, lambda i: (0, 0)),
            pl.BlockSpec(pool_w_group.shape, lambda i: (0, 0, 0, 0)),
        ],
        out_specs=pl.BlockSpec((D_MODEL, POOL_WIDTH), lambda i: (0, 0)),
        out_shape=jax.ShapeDtypeStruct((D_MODEL, POOL_WIDTH), BF16),
        compiler_params=pltpu.CompilerParams(vmem_limit_bytes=VMEM_LIMIT),
        name="pool_fold",
    )(pool_w_in, g_col, pool_w_group)


def _window_sums(a):
    s2 = a + pltpu.roll(a, 1, 0)
    s4 = s2 + pltpu.roll(s2, 2, 0)
    s8 = s4 + pltpu.roll(s4, 4, 0)
    s16 = s8 + pltpu.roll(s8, 8, 0)
    return {2: s2, 4: s4, 8: s8, 16: s16}


def _pooled_mix(ubuf, row0, n, pos0):
    pos = pos0 + jax.lax.broadcasted_iota(jnp.int32, (n, 1), 0)
    inv_cnt = {w: 1.0 / jnp.minimum(pos + 1, w).astype(F32) for w in POOL_WINDOWS}
    lane = jax.lax.broadcasted_iota(jnp.int32, (n, LANES), 1)
    mixed = []
    for c in range(POOL_WIDTH // LANES):
        cols = slice(c * LANES, (c + 1) * LANES)
        sums = _window_sums(ubuf[row0:row0 + POOL_HALO + n, cols])
        lo_grp = (c * LANES) // POOL_GROUP_DIM
        hi_grp = ((c + 1) * LANES - 1) // POOL_GROUP_DIM
        w_lo, w_hi = POOL_WINDOWS[lo_grp], POOL_WINDOWS[hi_grp]
        pooled = sums[w_lo][POOL_HALO:] * inv_cnt[w_lo]
        if hi_grp != lo_grp:
            split = hi_grp * POOL_GROUP_DIM - c * LANES
            pooled = jnp.where(lane < split, pooled, sums[w_hi][POOL_HALO:] * inv_cnt[w_hi])
        mixed.append(pooled - ubuf[POOL_HALO + row0:POOL_HALO + row0 + n, cols])
    return jnp.concatenate(mixed, axis=-1)


def _pool_layer_kernel(x_ref, w_u_ref, w_mq_ref, w_gate_ref, scale_ref,
                       kbd_ref, vbd_ref, w_out_ref, o_ref, ubuf):
    j = pl.program_id(1)
    t = x_ref.shape[1]
    n = ROW_CHUNK
    chunks = range(t // n)

    @pl.when(j == 0)
    def _():
        ubuf[0:POOL_HALO, :] = jnp.zeros((POOL_HALO, POOL_WIDTH), F32)

    @pl.when(j > 0)
    def _():
        ubuf[0:POOL_HALO, :] = ubuf[t:t + POOL_HALO, :]

    xs = [x_ref[0, c * n:(c + 1) * n, :] for c in chunks]
    hns = [_rms_scale(x).astype(BF16) for x in xs]
    for c in chunks:
        ubuf[POOL_HALO + c * n:POOL_HALO + (c + 1) * n, :] = _dot(hns[c], w_u_ref[...])
    mqs = [_dot(hn, w_mq_ref[...]) for hn in hns]
    gates = [_silu(_dot(hn, w_gate_ref[...])) for hn in hns]
    toks = [_pooled_mix(ubuf, c * n, n, j * t + c * n) * scale_ref[...] for c in chunks]
    mem_os = [_memory_attention(mq, kbd_ref[0, 0], vbd_ref[0, 0]) for mq in mqs]
    for c in chunks:
        y = _dot((toks[c] * gates[c][:, :POOL_WIDTH]).astype(BF16), w_out_ref[0, :POOL_WIDTH, :])
        y += _dot((mem_os[c] * gates[c][:, POOL_WIDTH:]).astype(BF16),
                  w_out_ref[0, POOL_WIDTH:, :])
        o_ref[0, c * n:(c + 1) * n, :] = xs[c] + y


def _pool_layer(x, w_u, w_in, scale, kbd, vbd, w_out):
    b, s, _ = x.shape
    t = SEQ_TILE
    const = lambda shape: pl.BlockSpec(shape, lambda i, j: (0,) * len(shape))
    return pl.pallas_call(
        _pool_layer_kernel,
        grid=(b, s // t),
        in_specs=[
            pl.BlockSpec((1, t, D_MODEL), lambda i, j: (i, j, 0)),
            const(w_u.shape),
            pl.BlockSpec((D_MODEL, MEM_WIDTH), lambda i, j: (0, POOL_WIDTH // MEM_WIDTH)),
            pl.BlockSpec((D_MODEL, MIX_WIDTH), lambda i, j: (0, 1)),
            const(scale.shape),
            pl.BlockSpec((1, 1) + kbd.shape[2:], lambda i, j: (i, 0, 0, 0)),
            pl.BlockSpec((1, 1) + vbd.shape[2:], lambda i, j: (i, 0, 0, 0)),
            pl.BlockSpec((1,) + w_out.shape[1:], lambda i, j: (0, 0, 0)),
        ],
        out_specs=pl.BlockSpec((1, t, D_MODEL), lambda i, j: (i, j, 0)),
        out_shape=jax.ShapeDtypeStruct(x.shape, F32),
        scratch_shapes=[pltpu.VMEM((POOL_HALO + t, POOL_WIDTH), F32)],
        compiler_params=pltpu.CompilerParams(
            dimension_semantics=("parallel", "arbitrary"), vmem_limit_bytes=VMEM_LIMIT),
        name="pool_layer",
    )(x, w_u, w_in, w_in, scale, kbd, vbd, w_out)


def _mla_prep_kernel(w_t_ref, g_ref, w_pm_ref, w_fm_ref):
    o_ckv = Q_LORA
    o_kr = o_ckv + KV_LORA
    o_mq = o_kr + QK_ROPE
    o_gate = o_mq + MEM_WIDTH
    g = g_ref[...]
    w_fm_ref[0:Q_LORA, :] = (w_t_ref[0:o_ckv, :] * g).astype(BF16)
    w_fm_ref[Q_LORA:, :] = (w_t_ref[o_kr:o_mq, :] * g).astype(BF16)
    blocks = ([(o_gate + r, 1.0) for r in range(0, MIX_WIDTH, LANES)]
              + [(o_ckv + r, 1.0) for r in range(0, KV_LORA, LANES)]
              + [(o_mq + r, MEM_Q_SCALE) for r in range(0, MEM_WIDTH, LANES)])
    for dst, (src, scale) in enumerate(blocks):
        blk = w_t_ref[src:src + LANES, :] * (g * scale)
        w_pm_ref[:, dst * LANES:(dst + 1) * LANES] = blk.T.astype(BF16)


def _mla_prep(w_t, g_row):
    n_in, d = w_t.shape
    n_pm = MIX_WIDTH + KV_LORA + MEM_WIDTH
    n_fm = Q_LORA + QK_ROPE
    return pl.pallas_call(
        _mla_prep_kernel,
        grid=(1,),
        in_specs=[pl.BlockSpec((n_in, d), lambda i: (0, 0)),
                  pl.BlockSpec(g_row.shape, lambda i: (0, 0))],
        out_specs=[pl.BlockSpec((d, n_pm), lambda i: (0, 0)),
                   pl.BlockSpec((n_fm, d), lambda i: (0, 0))],
        out_shape=[jax.ShapeDtypeStruct((d, n_pm), BF16), jax.ShapeDtypeStruct((n_fm, d), BF16)],
        compiler_params=pltpu.CompilerParams(vmem_limit_bytes=VMEM_LIMIT),
        name="mla_prep",
    )(w_t, g_row)


def _rope_t(x, cos_t, sin_t):
    half = QK_ROPE // 2
    x1, x2 = x[:half], x[half:]
    return x1 * cos_t - x2 * sin_t, x2 * cos_t + x1 * sin_t


def _mla_pre_kernel(h_ref, pos_ref, inv_freq_ref, w_fm_ref, w_pm_ref, w_uq_t_ref, w_uk_ref,
                    w_uv_t_ref, kbd_ref, vbd_ref, w_out_ref,
                    q_ref, k_ref, v_ref, sg_ref, hp_ref):
    w_cq_t_ref, w_kr_t_ref = w_fm_ref.at[0:Q_LORA], w_fm_ref.at[Q_LORA:Q_LORA + QK_ROPE]
    w_gate_ref = w_pm_ref.at[:, 0:MIX_WIDTH]
    w_ckv_ref = w_pm_ref.at[:, MIX_WIDTH:MIX_WIDTH + KV_LORA]
    w_mq_ref = w_pm_ref.at[:, MIX_WIDTH + KV_LORA:MIX_WIDTH + KV_LORA + MEM_WIDTH]
    w_out_mem_ref = w_out_ref.at[0]
    t = h_ref.shape[1]
    n = ROW_CHUNK
    chunks = range(t // n)
    heads = range(MLA_HEADS)
    half = QK_ROPE // 2
    rows = [slice(c * n, (c + 1) * n) for c in chunks]

    hs = [h_ref[0, rows[c], :] for c in chunks]
    xns = [_rms_scale(h) for h in hs]
    hns = [xn.astype(BF16) for xn in xns]
    mqs = [_dot(hn, w_mq_ref[...]) for hn in hns]
    mem_os = [_memory_attention(mq, kbd_ref[0, 0], vbd_ref[0, 0]) for mq in mqs]
    gates = [_silu(_dot(hn, w_gate_ref[...])) for hn in hns]
    for c in chunks:
        sg_ref[0, rows[c], :] = gates[c][:, :MLA_WIDTH]
        hp_ref[0, rows[c], :] = hs[c] + _dot(
            (mem_os[c] * gates[c][:, MLA_WIDTH:]).astype(BF16), w_out_mem_ref[...])

    hn_ts = [xn.T.astype(BF16) for xn in xns]
    cq_ts = [_dot(w_cq_t_ref[...], hn_t) for hn_t in hn_ts]
    kr_ts = [_dot(w_kr_t_ref[...], hn_t) for hn_t in hn_ts]
    ckvns = [_rms_scale(_dot(hn, w_ckv_ref[...])).astype(BF16) for hn in hns]
    angs = [pos_ref[0, :, rows[c]].astype(F32) * inv_freq_ref[...] for c in chunks]
    cos_ts, sin_ts = [jnp.cos(a) for a in angs], [jnp.sin(a) for a in angs]
    cqn_ts = [(cq * jax.lax.rsqrt(jnp.mean(cq * cq, axis=0, keepdims=True) + EPS)
               ).astype(BF16) for cq in cq_ts]

    q_ts = [_dot(w_uq_t_ref[...], cqn_t) for cqn_t in cqn_ts]
    for c in chunks:
        for hd in heads:
            base = hd * QK_DIM
            r1, r2 = _rope_t(q_ts[c][base + QK_NOPE:base + QK_DIM], cos_ts[c], sin_ts[c])
            q_ref[0, hd, 0:QK_NOPE, rows[c]] = q_ts[c][base:base + QK_NOPE].astype(BF16)
            q_ref[0, hd, QK_NOPE:QK_NOPE + half, rows[c]] = r1.astype(BF16)
            q_ref[0, hd, QK_NOPE + half:QK_DIM, rows[c]] = r2.astype(BF16)

    k_nopes = [_dot(ckvn, w_uk_ref[...]) for ckvn in ckvns]
    v_ts = [_dot_nt(w_uv_t_ref[...], ckvn) for ckvn in ckvns]
    for c in chunks:
        r1, r2 = _rope_t(kr_ts[c], cos_ts[c], sin_ts[c])
        k_rope = jnp.concatenate([r1, r2, jnp.zeros((LANES - QK_ROPE, n), F32)], axis=0).T
        k_rope = k_rope[:, :QK_ROPE]
        kt, off = divmod(c * n, ATTN_TILE)
        for hd in heads:
            k_ref[0, hd, rows[c], :] = jnp.concatenate(
                [k_nopes[c][:, hd * QK_NOPE:(hd + 1) * QK_NOPE], k_rope], axis=-1).astype(BF16)
            v_ref[0, hd, kt, 0:V_HEAD, off:off + n] = (
                v_ts[c][hd * V_HEAD:(hd + 1) * V_HEAD].astype(BF16))
            v_ref[0, hd, kt, V_HEAD:V_EXT, off:off + n] = jnp.ones((V_EXT - V_HEAD, n), BF16)


def _mla_pre(h, pos, inv_freq, w_fm, w_pm, w_uq_t, w_uk, w_uv_t, kbd, vbd, w_out):
    b, s, _ = h.shape
    t = SEQ_TILE
    n_kt = t // ATTN_TILE
    const = lambda shape: pl.BlockSpec(shape, lambda i, j: (0,) * len(shape))
    row = lambda width: pl.BlockSpec((1, t, width), lambda i, j: (i, j, 0))
    return pl.pallas_call(
        _mla_pre_kernel,
        grid=(b, s // t),
        in_specs=[
            row(D_MODEL),
            pl.BlockSpec((1, 1, t), lambda i, j: (i, 0, j)), const(inv_freq.shape),
            const(w_fm.shape), const(w_pm.shape),
            const(w_uq_t.shape), const(w_uk.shape), const(w_uv_t.shape),
            pl.BlockSpec((1, 1) + kbd.shape[2:], lambda i, j: (i, 1, 0, 0)),
            pl.BlockSpec((1, 1) + vbd.shape[2:], lambda i, j: (i, 1, 0, 0)),
            pl.BlockSpec((1, MEM_WIDTH, D_MODEL), lambda i, j: (1, MLA_WIDTH // MEM_WIDTH, 0)),
        ],
        out_specs=[
            pl.BlockSpec((1, MLA_HEADS, QK_DIM, t), lambda i, j: (i, 0, 0, j)),
            pl.BlockSpec((1, MLA_HEADS, t, QK_DIM), lambda i, j: (i, 0, j, 0)),
            pl.BlockSpec((1, MLA_HEADS, n_kt, V_EXT, ATTN_TILE), lambda i, j: (i, 0, j, 0, 0)),
            row(MLA_WIDTH), row(D_MODEL),
        ],
        out_shape=[
            jax.ShapeDtypeStruct((b, MLA_HEADS, QK_DIM, s), BF16),
            jax.ShapeDtypeStruct((b, MLA_HEADS, s, QK_DIM), BF16),
            jax.ShapeDtypeStruct((b, MLA_HEADS, s // ATTN_TILE, V_EXT, ATTN_TILE), BF16),
            jax.ShapeDtypeStruct((b, s, MLA_WIDTH), F32),
            jax.ShapeDtypeStruct((b, s, D_MODEL), F32),
        ],
        compiler_params=pltpu.CompilerParams(
            dimension_semantics=("parallel", "parallel"), vmem_limit_bytes=VMEM_LIMIT),
        name="mla_pre",
    )(h, pos, inv_freq, w_fm, w_pm, w_uq_t, w_uk, w_uv_t, kbd, vbd, w_out)


def _sublane_allreduce(x, op):
    for shift in (4, 2, 1):
        x = op(x, pltpu.roll(x, shift, 0))
    return x


def _mla_attn_kernel(q_ref, k_ref, v_ref, sg_ref, hp_ref, w_out_ref, g_ref, o_ref,
                     acc_sc, tok_sc):
    i = pl.program_id(1)
    tq = q_ref.shape[3]
    tk = v_ref.shape[4]
    half = tk // 2
    heads = range(MLA_HEADS)

    def step(jt, carry, k0, kn, q0, qn, masked):
        m_old = carry
        ks = slice(jt * tk + k0, jt * tk + k0 + kn)
        qs = slice(q0, q0 + qn)
        s = [_dot(k_ref[0, hd, ks, :], q_ref[0, hd, :, qs]) for hd in heads]
        if masked:
            k_chunk = (k0 + jax.lax.broadcasted_iota(jnp.int32, (kn, qn), 0)) // CHUNK
            q_chunk = (q0 + jax.lax.broadcasted_iota(jnp.int32, (kn, qn), 1)) // CHUNK
            mask = k_chunk <= q_chunk
        m_new, alpha, p = [], [], []
        for hd in heads:
            sh = jnp.where(mask, s[hd], NEG_INF) if masked else s[hd]
            s3 = sh.reshape(kn // SUBLANES, SUBLANES, qn)
            m_prev = m_old[hd][:, qs]
            m = jnp.maximum(m_prev, _sublane_allreduce(jnp.max(s3, axis=0), jnp.maximum))
            alpha.append(jnp.exp2(m_prev - m))
            p.append(jnp.exp2(s3 - m[None]).reshape(kn, qn).astype(BF16))
            if qn != tq:
                m = jnp.concatenate(
                    ([m_old[hd][:, :q0]] if q0 else []) + [m]
                    + ([m_old[hd][:, q0 + qn:]] if q0 + qn < tq else []), axis=1)
            m_new.append(m)
        pv = [_dot(v_ref[0, hd, jt, :, k0:k0 + kn], p[hd]) for hd in heads]
        for hd in heads:
            acc = acc_sc[hd, :, qs].reshape(V_EXT // SUBLANES, SUBLANES, qn)
            acc_sc[hd, :, qs] = (alpha[hd][None] * acc).reshape(V_EXT, qn) + pv[hd]
        return tuple(m_new)

    def query_tile(n_full):
        acc_sc[...] = jnp.zeros(acc_sc.shape, F32)
        carry = tuple(jnp.full((SUBLANES, tq), NEG_INF, F32) for _ in heads)
        for jt in range(n_full):
            carry = step(jt, carry, 0, tk, 0, tq, False)
        carry = step(n_full, carry, 0, half, 0, tq, True)
        step(n_full, carry, half, tk - half, half, tq - half, True)

        for hd in heads:
            l_tot = acc_sc[hd, V_HEAD:V_HEAD + SUBLANES, :]
            o_t = acc_sc[hd, 0:V_HEAD, :].reshape(V_HEAD // SUBLANES, SUBLANES, tq) / l_tot[None]
            tok_sc[:, hd * V_HEAD:(hd + 1) * V_HEAD] = o_t.reshape(V_HEAD, tq).T

        n = tq // 2
        rows = [slice(c * n, (c + 1) * n) for c in range(tq // n)]
        branches = [(tok_sc[r, :] * sg_ref[0, r, :]).astype(BF16) for r in rows]
        outs = [hp_ref[0, r, :] + _dot(br, w_out_ref[0]) for r, br in zip(rows, branches)]
        for r, out in zip(rows, outs):
            o_ref[0, r, :] = _rms_norm(out, g_ref[...])

    for n_full in range(k_ref.shape[2] // tk):
        pl.when(i == n_full)(functools.partial(query_tile, n_full))


def _mla_attn(q_t, k, v_t, sg, hp, w_out, final_g):
    b, _, s, _ = k.shape
    t = ATTN_TILE
    const = lambda shape: pl.BlockSpec(shape, lambda i, j: (0,) * len(shape))
    row = lambda width: pl.BlockSpec((1, t, width), lambda i, j: (i, j, 0))
    return pl.pallas_call(
        _mla_attn_kernel,
        grid=(b, s // t),
        in_specs=[
            pl.BlockSpec((1, MLA_HEADS, QK_DIM, t), lambda i, j: (i, 0, 0, j)),
            pl.BlockSpec((1, MLA_HEADS, s, QK_DIM), lambda i, j: (i, 0, 0, 0)),
            pl.BlockSpec((1,) + v_t.shape[1:], lambda i, j: (i, 0, 0, 0, 0)),
            row(MLA_WIDTH), row(D_MODEL),
            pl.BlockSpec((1, MLA_WIDTH, D_MODEL), lambda i, j: (1, 0, 0)), const(final_g.shape),
        ],
        out_specs=row(D_MODEL),
        out_shape=jax.ShapeDtypeStruct((b, s, D_MODEL), F32),
        scratch_shapes=[
            pltpu.VMEM((MLA_HEADS, V_EXT, t), F32),
            pltpu.VMEM((t, MLA_WIDTH), F32),
        ],
        compiler_params=pltpu.CompilerParams(
            dimension_semantics=("parallel", "arbitrary"), vmem_limit_bytes=VMEM_LIMIT),
        name="mla_attn",
    )(q_t, k, v_t, sg, hp, w_out, final_g)


def kernel(x, mem, positions, norm_g, mem_norm_g, w_mem_kv, w_out, pool_w_in, pool_w_group,
           pool_scale, mla_w_in, mla_q_norm_g, mla_w_uq, mla_kv_norm_g, mla_w_ukv, final_norm_g):
    depth = w_mem_kv.shape[0]
    assert depth == 2 and pool_w_in.shape[0] == 1 and mla_w_in.shape[0] == 1
    bf = lambda a: a.astype(BF16)
    row = lambda a: a.reshape(1, -1)

    inv_freq = ROPE_THETA ** (-(jnp.arange(0, QK_ROPE, 2, dtype=F32) / QK_ROPE))
    kbd, vbd = _mem_kv(mem, row(mem_norm_g), bf(w_mem_kv))
    w_out_bf = bf(w_out)

    g0 = norm_g[0][:, None]
    mq_cols = (jnp.arange(POOL_IN) >= POOL_WIDTH) & (jnp.arange(POOL_IN) < POOL_WIDTH + MEM_WIDTH)
    pool_w = bf(pool_w_in[0] * g0 * jnp.where(mq_cols, MEM_Q_SCALE, 1.0))
    h1 = _pool_layer(x, _pool_fold(pool_w_in, g0, pool_w_group), pool_w, row(pool_scale[0]),
                     kbd, vbd, w_out_bf)

    w_pm, w_fm = _mla_prep(mla_w_in[0].T, row(norm_g[1]))
    w_ukv = (mla_w_ukv[0] * mla_kv_norm_g[0][:, None]).reshape(
        KV_LORA, MLA_HEADS, QK_NOPE + V_HEAD)
    w_uk = bf(w_ukv[:, :, :QK_NOPE].reshape(KV_LORA, MLA_HEADS * QK_NOPE))
    w_uv_t = bf(w_ukv[:, :, QK_NOPE:].reshape(KV_LORA, MLA_WIDTH).T)
    w_uq_t = bf((mla_w_uq[0] * (mla_q_norm_g[0][:, None] * Q_SCALE)).T)
    q_t, k, v_t, sg, hp = _mla_pre(
        h1, positions[:, None, :], inv_freq[:, None], w_fm, w_pm, w_uq_t, w_uk, w_uv_t,
        kbd, vbd, w_out_bf)

    return _mla_attn(q_t, k, v_t, sg, hp, w_out_bf, row(final_norm_g))
```

```python
import functools
import math

import jax
import jax.numpy as jnp
from jax.experimental import pallas as pl
from jax.experimental.pallas import tpu as pltpu

D_MODEL = 1024
CHUNK = 64
EPS = 1e-6
NEG_INF = -1e30

N_MEM = 256
MEM_HEADS = 4
MEM_HEAD_DIM = 64
MEM_WIDTH = MEM_HEADS * MEM_HEAD_DIM
MEM_BATCH = 4

POOL_WINDOWS = (2, 4, 8, 16)
POOL_GROUP_DIM = 192
POOL_WIDTH = len(POOL_WINDOWS) * POOL_GROUP_DIM
POOL_HALO = 16

MLA_HEADS = 6
QK_NOPE = 128
QK_ROPE = 64
QK_DIM = QK_NOPE + QK_ROPE
V_HEAD = 128
V_EXT = V_HEAD + 16
Q_LORA = 384
KV_LORA = 256
MLA_WIDTH = MLA_HEADS * V_HEAD
ROPE_THETA = 10000.0
LOG2_E = math.log2(math.e)
Q_SCALE = QK_DIM ** -0.5 * LOG2_E
MEM_Q_SCALE = MEM_HEAD_DIM ** -0.5 * LOG2_E

MIX_WIDTH = POOL_WIDTH + MEM_WIDTH
POOL_IN = POOL_WIDTH + MEM_WIDTH + MIX_WIDTH

LANES = 128
SUBLANES = 8
SEQ_TILE = 1024
ROW_CHUNK = 256
ATTN_TILE = 512
VMEM_LIMIT = 56 * 1024 * 1024

BF16 = jnp.bfloat16
F32 = jnp.float32


def _rms_scale(x):
    ms = jnp.mean(x * x, axis=-1, keepdims=True)
    return x * jax.lax.rsqrt(ms + EPS)


def _rms_norm(x, g):
    return _rms_scale(x) * g


def _silu(x):
    return x / (1.0 + jnp.exp2(x * -LOG2_E))


def _dot(a, b):
    return jnp.dot(a, b, preferred_element_type=F32)


def _dot_nt(a, b):
    return jax.lax.dot_general(a, b, (((1,), (1,)), ((), ())), preferred_element_type=F32)


def _memory_attention(mq, kbd, vbd):
    s = _dot(mq.astype(BF16), kbd)
    probs = []
    for h in range(MEM_HEADS):
        sh = s[:, h * N_MEM:(h + 1) * N_MEM]
        e = jnp.exp2(sh - jnp.max(sh, axis=-1, keepdims=True))
        probs.append((e / jnp.sum(e, axis=-1, keepdims=True)).astype(BF16))
    return _dot(jnp.concatenate(probs, axis=-1), vbd)


def _mem_kv_kernel(mem_ref, g_ref, w_ref, kbd_ref, vbd_ref):
    rows = range(mem_ref.shape[0])
    mem_ns = [_rms_norm(mem_ref[r], g_ref[...]).astype(BF16) for r in rows]
    row_head = jax.lax.broadcasted_iota(jnp.int32, (MEM_WIDTH, N_MEM), 0) // MEM_HEAD_DIM
    col_head = jax.lax.broadcasted_iota(jnp.int32, (N_MEM, MEM_WIDTH), 1) // MEM_HEAD_DIM
    for layer in range(w_ref.shape[0]):
        kvs = [_dot(mem_n, w_ref[layer]) for mem_n in mem_ns]
        for r in rows:
            k_t = kvs[r][:, :MEM_WIDTH].T
            v = kvs[r][:, MEM_WIDTH:]
            for h in range(MEM_HEADS):
                kbd_ref[r, layer, :, h * N_MEM:(h + 1) * N_MEM] = jnp.where(
                    row_head == h, k_t, 0.0).astype(BF16)
                vbd_ref[r, layer, h * N_MEM:(h + 1) * N_MEM, :] = jnp.where(
                    col_head == h, v, 0.0).astype(BF16)


def _mem_kv(mem, mem_norm_g, w):
    b = mem.shape[0]
    n_layers = w.shape[0]
    nb = MEM_BATCH
    return pl.pallas_call(
        _mem_kv_kernel,
        grid=(b // nb,),
        in_specs=[
            pl.BlockSpec((nb, N_MEM, D_MODEL), lambda i: (i, 0, 0)),
            pl.BlockSpec((1, D_MODEL), lambda i: (0, 0)),
            pl.BlockSpec(w.shape, lambda i: (0, 0, 0)),
        ],
        out_specs=[
            pl.BlockSpec((nb, n_layers, MEM_WIDTH, MEM_HEADS * N_MEM), lambda i: (i, 0, 0, 0)),
            pl.BlockSpec((nb, n_layers, MEM_HEADS * N_MEM, MEM_WIDTH), lambda i: (i, 0, 0, 0)),
        ],
        out_shape=[
            jax.ShapeDtypeStruct((b, n_layers, MEM_WIDTH, MEM_HEADS * N_MEM), BF16),
            jax.ShapeDtypeStruct((b, n_layers, MEM_HEADS * N_MEM, MEM_WIDTH), BF16),
        ],
        compiler_params=pltpu.CompilerParams(
            dimension_semantics=("parallel",), vmem_limit_bytes=VMEM_LIMIT),
        name="mem_kv",
    )(mem, mem_norm_g, w)


def _split_bf16(a):
    hi = a.astype(BF16)
    return hi, (a - hi.astype(F32)).astype(BF16)


def _pool_fold_kernel(w_u_ref, g_ref, w_grp_ref, o_ref):
    grp = [_split_bf16(w_grp_ref[0, gi]) for gi in range(len(POOL_WINDOWS))]
    n = ROW_CHUNK
    for c in range(w_u_ref.shape[1] // n):
        a = w_u_ref[0, c * n:(c + 1) * n, :] * g_ref[c * n:(c + 1) * n, :]
        cols = []
        for gi, (b_hi, b_lo) in enumerate(grp):
            a_hi, a_lo = _split_bf16(a[:, gi * POOL_GROUP_DIM:(gi + 1) * POOL_GROUP_DIM])
            cols.append(_dot(a_hi, b_hi) + _dot(a_hi, b_lo) + _dot(a_lo, b_hi))
        o_ref[c * n:(c + 1) * n, :] = jnp.concatenate(cols, axis=-1).astype(BF16)


def _pool_fold(pool_w_in, g_col, pool_w_group):
    return pl.pallas_call(
        _pool_fold_kernel,
        grid=(1,),
        in_specs=[
            pl.BlockSpec((1, D_MODEL, POOL_WIDTH), lambda i: (0, 0, 0)),
            pl.BlockSpec(g_col.shape, lambda i: (0, 0)),
            pl.BlockSpec(pool_w_group.shape, lambda i: (0, 0, 0, 0)),
        ],
        out_specs=pl.BlockSpec((D_MODEL, POOL_WIDTH), lambda i: (0, 0)),
        out_shape=jax.ShapeDtypeStruct((D_MODEL, POOL_WIDTH), BF16),
        compiler_params=pltpu.CompilerParams(vmem_limit_bytes=VMEM_LIMIT),
        name="pool_fold",
    )(pool_w_in, g_col, pool_w_group)


def _window_sums(a):
    s2 = a + pltpu.roll(a, 1, 0)
    s4 = s2 + pltpu.roll(s2, 2, 0)
    s8 = s4 + pltpu.roll(s4, 4, 0)
    s16 = s8 + pltpu.roll(s8, 8, 0)
    return {2: s2, 4: s4, 8: s8, 16: s16}


def _pooled_mix(ubuf, row0, n, pos0):
    pos = pos0 + jax.lax.broadcasted_iota(jnp.int32, (n, 1), 0)
    inv_cnt = {w: 1.0 / jnp.minimum(pos + 1, w).astype(F32) for w in POOL_WINDOWS}
    lane = jax.lax.broadcasted_iota(jnp.int32, (n, LANES), 1)
    mixed = []
    for c in range(POOL_WIDTH // LANES):
        cols = slice(c * LANES, (c + 1) * LANES)
        sums = _window_sums(ubuf[row0:row0 + POOL_HALO + n, cols])
        lo_grp = (c * LANES) // POOL_GROUP_DIM
        hi_grp = ((c + 1) * LANES - 1) // POOL_GROUP_DIM
        w_lo, w_hi = POOL_WINDOWS[lo_grp], POOL_WINDOWS[hi_grp]
        pooled = sums[w_lo][POOL_HALO:] * inv_cnt[w_lo]
        if hi_grp != lo_grp:
            split = hi_grp * POOL_GROUP_DIM - c * LANES
            pooled = jnp.where(lane < split, pooled, sums[w_hi][POOL_HALO:] * inv_cnt[w_hi])
        mixed.append(pooled - ubuf[POOL_HALO + row0:POOL_HALO + row0 + n, cols])
    return jnp.concatenate(mixed, axis=-1)


def _pool_layer_kernel(x_ref, w_u_ref, w_mq_ref, w_gate_ref, scale_ref,
                       kbd_ref, vbd_ref, w_out_ref, o_ref, ubuf):
    j = pl.program_id(1)
    t = x_ref.shape[1]
    n = ROW_CHUNK
    chunks = range(t // n)

    @pl.when(j == 0)
    def _():
        ubuf[0:POOL_HALO, :] = jnp.zeros((POOL_HALO, POOL_WIDTH), F32)

    @pl.when(j > 0)
    def _():
        ubuf[0:POOL_HALO, :] = ubuf[t:t + POOL_HALO, :]

    xs = [x_ref[0, c * n:(c + 1) * n, :] for c in chunks]
    hns = [_rms_scale(x).astype(BF16) for x in xs]
    for c in chunks:
        ubuf[POOL_HALO + c * n:POOL_HALO + (c + 1) * n, :] = _dot(hns[c], w_u_ref[...])
    mqs = [_dot(hn, w_mq_ref[...]) for hn in hns]
    gates = [_silu(_dot(hn, w_gate_ref[...])) for hn in hns]
    toks = [_pooled_mix(ubuf, c * n, n, j * t + c * n) * scale_ref[...] for c in chunks]
    mem_os = [_memory_attention(mq, kbd_ref[0, 0], vbd_ref[0, 0]) for mq in mqs]
    for c in chunks:
        y = _dot((toks[c] * gates[c][:, :POOL_WIDTH]).astype(BF16), w_out_ref[0, :POOL_WIDTH, :])
        y += _dot((mem_os[c] * gates[c][:, POOL_WIDTH:]).astype(BF16),
                  w_out_ref[0, POOL_WIDTH:, :])
        o_ref[0, c * n:(c + 1) * n, :] = xs[c] + y


def _pool_layer(x, w_u, w_in, scale, kbd, vbd, w_out):
    b, s, _ = x.shape
    t = SEQ_TILE
    const = lambda shape: pl.BlockSpec(shape, lambda i, j: (0,) * len(shape))
    return pl.pallas_call(
        _pool_layer_kernel,
        grid=(b, s // t),
        in_specs=[
            pl.BlockSpec((1, t, D_MODEL), lambda i, j: (i, j, 0)),
            const(w_u.shape),
            pl.BlockSpec((D_MODEL, MEM_WIDTH), lambda i, j: (0, POOL_WIDTH // MEM_WIDTH)),
            pl.BlockSpec((D_MODEL, MIX_WIDTH), lambda i, j: (0, 1)),
            const(scale.shape),
            pl.BlockSpec((1, 1) + kbd.shape[2:], lambda i, j: (i, 0, 0, 0)),
            pl.BlockSpec((1, 1) + vbd.shape[2:], lambda i, j: (i, 0, 0, 0)),
            pl.BlockSpec((1,) + w_out.shape[1:], lambda i, j: (0, 0, 0)),
        ],
        out_specs=pl.BlockSpec((1, t, D_MODEL), lambda i, j: (i, j, 0)),
        out_shape=jax.ShapeDtypeStruct(x.shape, F32),
        scratch_shapes=[pltpu.VMEM((POOL_HALO + t, POOL_WIDTH), F32)],
        compiler_params=pltpu.CompilerParams(
            dimension_semantics=("parallel", "arbitrary"), vmem_limit_bytes=VMEM_LIMIT),
        name="pool_layer",
    )(x, w_u, w_in, w_in, scale, kbd, vbd, w_out)


def _mla_prep_kernel(w_t_ref, g_ref, w_pm_ref, w_fm_ref):
    o_ckv = Q_LORA
    o_kr = o_ckv + KV_LORA
    o_mq = o_kr + QK_ROPE
    o_gate = o_mq + MEM_WIDTH
    g = g_ref[...]
    w_fm_ref[0:Q_LORA, :] = (w_t_ref[0:o_ckv, :] * g).astype(BF16)
    w_fm_ref[Q_LORA:, :] = (w_t_ref[o_kr:o_mq, :] * g).astype(BF16)
    blocks = ([(o_gate + r, 1.0) for r in range(0, MIX_WIDTH, LANES)]
              + [(o_ckv + r, 1.0) for r in range(0, KV_LORA, LANES)]
              + [(o_mq + r, MEM_Q_SCALE) for r in range(0, MEM_WIDTH, LANES)])
    for dst, (src, scale) in enumerate(blocks):
        blk = w_t_ref[src:src + LANES, :] * (g * scale)
        w_pm_ref[:, dst * LANES:(dst + 1) * LANES] = blk.T.astype(BF16)


def _mla_prep(w_t, g_row):
    n_in, d = w_t.shape
    n_pm = MIX_WIDTH + KV_LORA + MEM_WIDTH
    n_fm = Q_LORA + QK_ROPE
    return pl.pallas_call(
        _mla_prep_kernel,
        grid=(1,),
        in_specs=[pl.BlockSpec((n_in, d), lambda i: (0, 0)),
                  pl.BlockSpec(g_row.shape, lambda i: (0, 0))],
        out_specs=[pl.BlockSpec((d, n_pm), lambda i: (0, 0)),
                   pl.BlockSpec((n_fm, d), lambda i: (0, 0))],
        out_shape=[jax.ShapeDtypeStruct((d, n_pm), BF16), jax.ShapeDtypeStruct((n_fm, d), BF16)],
        compiler_params=pltpu.CompilerParams(vmem_limit_bytes=VMEM_LIMIT),
        name="mla_prep",
    )(w_t, g_row)


def _rope_t(x, cos_t, sin_t):
    half = QK_ROPE // 2
    x1, x2 = x[:half], x[half:]
    return x1 * cos_t - x2 * sin_t, x2 * cos_t + x1 * sin_t


def _mla_pre_kernel(h_ref, pos_ref, inv_freq_ref, w_fm_ref, w_pm_ref, w_uq_t_ref, w_uk_ref,
                    w_uv_t_ref, kbd_ref, vbd_ref, w_out_ref,
                    q_ref, k_ref, v_ref, sg_ref, hp_ref):
    w_cq_t_ref, w_kr_t_ref = w_fm_ref.at[0:Q_LORA], w_fm_ref.at[Q_LORA:Q_LORA + QK_ROPE]
    w_gate_ref = w_pm_ref.at[:, 0:MIX_WIDTH]
    w_ckv_ref = w_pm_ref.at[:, MIX_WIDTH:MIX_WIDTH + KV_LORA]
    w_mq_ref = w_pm_ref.at[:, MIX_WIDTH + KV_LORA:MIX_WIDTH + KV_LORA + MEM_WIDTH]
    w_out_mem_ref = w_out_ref.at[0]
    t = h_ref.shape[1]
    n = 2 * ROW_CHUNK
    chunks = range(t // n)
    heads = range(MLA_HEADS)
    half = QK_ROPE // 2
    rows = [slice(c * n, (c + 1) * n) for c in chunks]

    hs = [h_ref[0, rows[c], :] for c in chunks]
    xns = [_rms_scale(h) for h in hs]
    hns = [xn.astype(BF16) for xn in xns]
    mqs = [_dot(hn, w_mq_ref[...]) for hn in hns]
    mem_os = [_memory_attention(mq, kbd_ref[0, 0], vbd_ref[0, 0]) for mq in mqs]
    gates = [_silu(_dot(hn, w_gate_ref[...])) for hn in hns]
    for c in chunks:
        sg_ref[0, rows[c], :] = gates[c][:, :MLA_WIDTH]
        hp_ref[0, rows[c], :] = hs[c] + _dot(
            (mem_os[c] * gates[c][:, MLA_WIDTH:]).astype(BF16), w_out_mem_ref[...])

    hn_ts = [xn.T.astype(BF16) for xn in xns]
    cq_ts = [_dot(w_cq_t_ref[...], hn_t) for hn_t in hn_ts]
    kr_ts = [_dot(w_kr_t_ref[...], hn_t) for hn_t in hn_ts]
    ckvns = [_rms_scale(_dot(hn, w_ckv_ref[...])).astype(BF16) for hn in hns]
    angs = [pos_ref[0, :, rows[c]].astype(F32) * inv_freq_ref[...] for c in chunks]
    cos_ts, sin_ts = [jnp.cos(a) for a in angs], [jnp.sin(a) for a in angs]
    cqn_ts = [(cq * jax.lax.rsqrt(jnp.mean(cq * cq, axis=0, keepdims=True) + EPS)
               ).astype(BF16) for cq in cq_ts]

    q_ts = [_dot(w_uq_t_ref[...], cqn_t) for cqn_t in cqn_ts]
    for c in chunks:
        for hd in heads:
            base = hd * QK_DIM
            r1, r2 = _rope_t(q_ts[c][base + QK_NOPE:base + QK_DIM], cos_ts[c], sin_ts[c])
            q_ref[0, hd, 0:QK_NOPE, rows[c]] = q_ts[c][base:base + QK_NOPE].astype(BF16)
            q_ref[0, hd, QK_NOPE:QK_NOPE + half, rows[c]] = r1.astype(BF16)
            q_ref[0, hd, QK_NOPE + half:QK_DIM, rows[c]] = r2.astype(BF16)

    k_nopes = [_dot(ckvn, w_uk_ref[...]) for ckvn in ckvns]
    v_ts = [_dot_nt(w_uv_t_ref[...], ckvn) for ckvn in ckvns]
    for c in chunks:
        r1, r2 = _rope_t(kr_ts[c], cos_ts[c], sin_ts[c])
        k_rope = jnp.concatenate([r1, r2, jnp.zeros((LANES - QK_ROPE, n), F32)], axis=0).T
        k_rope = k_rope[:, :QK_ROPE]
        kt, off = divmod(c * n, ATTN_TILE)
        for hd in heads:
            k_ref[0, hd, rows[c], :] = jnp.concatenate(
                [k_nopes[c][:, hd * QK_NOPE:(hd + 1) * QK_NOPE], k_rope], axis=-1).astype(BF16)
            v_ref[0, hd, kt, 0:V_HEAD, off:off + n] = (
                v_ts[c][hd * V_HEAD:(hd + 1) * V_HEAD].astype(BF16))
            v_ref[0, hd, kt, V_HEAD:V_EXT, off:off + n] = jnp.ones((V_EXT - V_HEAD, n), BF16)


def _mla_pre(h, pos, inv_freq, w_fm, w_pm, w_uq_t, w_uk, w_uv_t, kbd, vbd, w_out):
    b, s, _ = h.shape
    t = SEQ_TILE
    n_kt = t // ATTN_TILE
    const = lambda shape: pl.BlockSpec(shape, lambda i, j: (0,) * len(shape))
    row = lambda width: pl.BlockSpec((1, t, width), lambda i, j: (i, j, 0))
    return pl.pallas_call(
        _mla_pre_kernel,
        grid=(b, s // t),
        in_specs=[
            row(D_MODEL),
            pl.BlockSpec((1, 1, t), lambda i, j: (i, 0, j)), const(inv_freq.shape),
            const(w_fm.shape), const(w_pm.shape),
            const(w_uq_t.shape), const(w_uk.shape), const(w_uv_t.shape),
            pl.BlockSpec((1, 1) + kbd.shape[2:], lambda i, j: (i, 1, 0, 0)),
            pl.BlockSpec((1, 1) + vbd.shape[2:], lambda i, j: (i, 1, 0, 0)),
            pl.BlockSpec((1, MEM_WIDTH, D_MODEL), lambda i, j: (1, MLA_WIDTH // MEM_WIDTH, 0)),
        ],
        out_specs=[
            pl.BlockSpec((1, MLA_HEADS, QK_DIM, t), lambda i, j: (i, 0, 0, j)),
            pl.BlockSpec((1, MLA_HEADS, t, QK_DIM), lambda i, j: (i, 0, j, 0)),
            pl.BlockSpec((1, MLA_HEADS, n_kt, V_EXT, ATTN_TILE), lambda i, j: (i, 0, j, 0, 0)),
            row(MLA_WIDTH), row(D_MODEL),
        ],
        out_shape=[
            jax.ShapeDtypeStruct((b, MLA_HEADS, QK_DIM, s), BF16),
            jax.ShapeDtypeStruct((b, MLA_HEADS, s, QK_DIM), BF16),
            jax.ShapeDtypeStruct((b, MLA_HEADS, s // ATTN_TILE, V_EXT, ATTN_TILE), BF16),
            jax.ShapeDtypeStruct((b, s, MLA_WIDTH), F32),
            jax.ShapeDtypeStruct((b, s, D_MODEL), F32),
        ],
        compiler_params=pltpu.CompilerParams(
            dimension_semantics=("parallel", "parallel"), vmem_limit_bytes=VMEM_LIMIT),
        name="mla_pre",
    )(h, pos, inv_freq, w_fm, w_pm, w_uq_t, w_uk, w_uv_t, kbd, vbd, w_out)


def _sublane_allreduce(x, op):
    for shift in (4, 2, 1):
        x = op(x, pltpu.roll(x, shift, 0))
    return x


def _mla_attn_kernel(q_ref, k_ref, v_ref, sg_ref, hp_ref, w_out_ref, g_ref, o_ref,
                     acc_sc, tok_sc):
    i = pl.program_id(1)
    tq = q_ref.shape[3]
    tk = v_ref.shape[4]
    half = tk // 2
    heads = range(MLA_HEADS)

    def step(jt, carry, k0, kn, q0, qn, masked):
        m_old = carry
        ks = slice(jt * tk + k0, jt * tk + k0 + kn)
        qs = slice(q0, q0 + qn)
        s = [_dot(k_ref[0, hd, ks, :], q_ref[0, hd, :, qs]) for hd in heads]
        if masked:
            k_chunk = (k0 + jax.lax.broadcasted_iota(jnp.int32, (kn, qn), 0)) // CHUNK
            q_chunk = (q0 + jax.lax.broadcasted_iota(jnp.int32, (kn, qn), 1)) // CHUNK
            mask = k_chunk <= q_chunk
        m_new, alpha, p = [], [], []
        for hd in heads:
            sh = jnp.where(mask, s[hd], NEG_INF) if masked else s[hd]
            s3 = sh.reshape(kn // SUBLANES, SUBLANES, qn)
            m_prev = m_old[hd][:, qs]
            m = jnp.maximum(m_prev, _sublane_allreduce(jnp.max(s3, axis=0), jnp.maximum))
            alpha.append(jnp.exp2(m_prev - m))
            p.append(jnp.exp2(s3 - m[None]).reshape(kn, qn).astype(BF16))
            if qn != tq:
                m = jnp.concatenate(
                    ([m_old[hd][:, :q0]] if q0 else []) + [m]
                    + ([m_old[hd][:, q0 + qn:]] if q0 + qn < tq else []), axis=1)
            m_new.append(m)
        pv = [_dot(v_ref[0, hd, jt, :, k0:k0 + kn], p[hd]) for hd in heads]
        for hd in heads:
            acc = acc_sc[hd, :, qs].reshape(V_EXT // SUBLANES, SUBLANES, qn)
            acc_sc[hd, :, qs] = (alpha[hd][None] * acc).reshape(V_EXT, qn) + pv[hd]
        return tuple(m_new)

    def query_tile(n_full):
        acc_sc[...] = jnp.zeros(acc_sc.shape, F32)
        carry = tuple(jnp.full((SUBLANES, tq), NEG_INF, F32) for _ in heads)
        for jt in range(n_full):
            carry = step(jt, carry, 0, tk, 0, tq, False)
        carry = step(n_full, carry, 0, half, 0, tq, True)
        step(n_full, carry, half, tk - half, half, tq - half, True)

        for hd in heads:
            l_tot = acc_sc[hd, V_HEAD:V_HEAD + SUBLANES, :]
            o_t = acc_sc[hd, 0:V_HEAD, :].reshape(V_HEAD // SUBLANES, SUBLANES, tq) / l_tot[None]
            tok_sc[:, hd * V_HEAD:(hd + 1) * V_HEAD] = o_t.reshape(V_HEAD, tq).T

        n = tq // 2
        rows = [slice(c * n, (c + 1) * n) for c in range(tq // n)]
        branches = [(tok_sc[r, :] * sg_ref[0, r, :]).astype(BF16) for r in rows]
        outs = [hp_ref[0, r, :] + _dot(br, w_out_ref[0]) for r, br in zip(rows, branches)]
        for r, out in zip(rows, outs):
            o_ref[0, r, :] = _rms_norm(out, g_ref[...])

    for n_full in range(k_ref.shape[2] // tk):
        pl.when(i == n_full)(functools.partial(query_tile, n_full))


def _mla_attn(q_t, k, v_t, sg, hp, w_out, final_g):
    b, _, s, _ = k.shape
    t = ATTN_TILE
    const = lambda shape: pl.BlockSpec(shape, lambda i, j: (0,) * len(shape))
    row = lambda width: pl.BlockSpec((1, t, width), lambda i, j: (i, j, 0))
    return pl.pallas_call(
        _mla_attn_kernel,
        grid=(b, s // t),
        in_specs=[
            pl.BlockSpec((1, MLA_HEADS, QK_DIM, t), lambda i, j: (i, 0, 0, j)),
            pl.BlockSpec((1, MLA_HEADS, s, QK_DIM), lambda i, j: (i, 0, 0, 0)),
            pl.BlockSpec((1,) + v_t.shape[1:], lambda i, j: (i, 0, 0, 0, 0)),
            row(MLA_WIDTH), row(D_MODEL),
            pl.BlockSpec((1, MLA_WIDTH, D_MODEL), lambda i, j: (1, 0, 0)), const(final_g.shape),
        ],
        out_specs=row(D_MODEL),
        out_shape=jax.ShapeDtypeStruct((b, s, D_MODEL), F32),
        scratch_shapes=[
            pltpu.VMEM((MLA_HEADS, V_EXT, t), F32),
            pltpu.VMEM((t, MLA_WIDTH), F32),
        ],
        compiler_params=pltpu.CompilerParams(
            dimension_semantics=("parallel", "arbitrary"), vmem_limit_bytes=VMEM_LIMIT),
        name="mla_attn",
    )(q_t, k, v_t, sg, hp, w_out, final_g)


def kernel(x, mem, positions, norm_g, mem_norm_g, w_mem_kv, w_out, pool_w_in, pool_w_group,
           pool_scale, mla_w_in, mla_q_norm_g, mla_w_uq, mla_kv_norm_g, mla_w_ukv, final_norm_g):
    depth = w_mem_kv.shape[0]
    assert depth == 2 and pool_w_in.shape[0] == 1 and mla_w_in.shape[0] == 1
    bf = lambda a: a.astype(BF16)
    row = lambda a: a.reshape(1, -1)

    inv_freq = ROPE_THETA ** (-(jnp.arange(0, QK_ROPE, 2, dtype=F32) / QK_ROPE))
    kbd, vbd = _mem_kv(mem, row(mem_norm_g), bf(w_mem_kv))
    w_out_bf = bf(w_out)

    g0 = norm_g[0][:, None]
    mq_cols = (jnp.arange(POOL_IN) >= POOL_WIDTH) & (jnp.arange(POOL_IN) < POOL_WIDTH + MEM_WIDTH)
    pool_w = bf(pool_w_in[0] * g0 * jnp.where(mq_cols, MEM_Q_SCALE, 1.0))
    h1 = _pool_layer(x, _pool_fold(pool_w_in, g0, pool_w_group), pool_w, row(pool_scale[0]),
                     kbd, vbd, w_out_bf)

    w_pm, w_fm = _mla_prep(mla_w_in[0].T, row(norm_g[1]))
    w_ukv = (mla_w_ukv[0] * mla_kv_norm_g[0][:, None]).reshape(
        KV_LORA, MLA_HEADS, QK_NOPE + V_HEAD)
    w_uk = bf(w_ukv[:, :, :QK_NOPE].reshape(KV_LORA, MLA_HEADS * QK_NOPE))
    w_uv_t = bf(w_ukv[:, :, QK_NOPE:].reshape(KV_LORA, MLA_WIDTH).T)
    w_uq_t = bf((mla_w_uq[0] * (mla_q_norm_g[0][:, None] * Q_SCALE)).T)
    q_t, k, v_t, sg, hp = _mla_pre(
        h1, positions[:, None, :], inv_freq[:, None], w_fm, w_pm, w_uq_t, w_uk, w_uv_t,
        kbd, vbd, w_out_bf)

    return _mla_attn(q_t, k, v_t, sg, hp, w_out_bf, row(final_norm_g))
```

```python
import functools
import math

import jax
import jax.numpy as jnp
from jax.experimental import pallas as pl
from jax.experimental.pallas import tpu as pltpu

D_MODEL = 1024
CHUNK = 64
EPS = 1e-6
NEG_INF = -1e30

N_MEM = 256
MEM_HEADS = 4
MEM_HEAD_DIM = 64
MEM_WIDTH = MEM_HEADS * MEM_HEAD_DIM
MEM_BATCH = 4

POOL_WINDOWS = (2, 4, 8, 16)
POOL_GROUP_DIM = 192
POOL_WIDTH = len(POOL_WINDOWS) * POOL_GROUP_DIM
POOL_HALO = 16

MLA_HEADS = 6
QK_NOPE = 128
QK_ROPE = 64
QK_DIM = QK_NOPE + QK_ROPE
V_HEAD = 128
V_EXT = V_HEAD + 16
Q_LORA = 384
KV_LORA = 256
MLA_WIDTH = MLA_HEADS * V_HEAD
ROPE_THETA = 10000.0
LOG2_E = math.log2(math.e)
Q_SCALE = QK_DIM ** -0.5 * LOG2_E
MEM_Q_SCALE = MEM_HEAD_DIM ** -0.5 * LOG2_E

MIX_WIDTH = POOL_WIDTH + MEM_WIDTH
POOL_IN = POOL_WIDTH + MEM_WIDTH + MIX_WIDTH

LANES = 128
SUBLANES = 8
SEQ_TILE = 1024
ROW_CHUNK = 256
ATTN_TILE = 512
VMEM_LIMIT = 56 * 1024 * 1024

BF16 = jnp.bfloat16
F32 = jnp.float32


def _rms_scale(x):
    ms = jnp.mean(x * x, axis=-1, keepdims=True)
    return x * jax.lax.rsqrt(ms + EPS)


def _rms_norm(x, g):
    return _rms_scale(x) * g


def _silu(x):
    return x / (1.0 + jnp.exp2(x * -LOG2_E))


def _dot(a, b):
    return jnp.dot(a, b, preferred_element_type=F32)


def _dot_nt(a, b):
    return jax.lax.dot_general(a, b, (((1,), (1,)), ((), ())), preferred_element_type=F32)


def _memory_attention(mq, kbd, vbd):
    s = _dot(mq.astype(BF16), kbd)
    probs = []
    for h in range(MEM_HEADS):
        sh = s[:, h * N_MEM:(h + 1) * N_MEM]
        e = jnp.exp2(sh - jnp.max(sh, axis=-1, keepdims=True))
        probs.append((e / jnp.sum(e, axis=-1, keepdims=True)).astype(BF16))
    return _dot(jnp.concatenate(probs, axis=-1), vbd)


def _mem_kv_kernel(mem_ref, g_ref, w_ref, kbd_ref, vbd_ref):
    rows = range(mem_ref.shape[0])
    mem_ns = [_rms_norm(mem_ref[r], g_ref[...]).astype(BF16) for r in rows]
    row_head = jax.lax.broadcasted_iota(jnp.int32, (MEM_WIDTH, N_MEM), 0) // MEM_HEAD_DIM
    col_head = jax.lax.broadcasted_iota(jnp.int32, (N_MEM, MEM_WIDTH), 1) // MEM_HEAD_DIM
    for layer in range(w_ref.shape[0]):
        kvs = [_dot(mem_n, w_ref[layer]) for mem_n in mem_ns]
        for r in rows:
            k_t = kvs[r][:, :MEM_WIDTH].T
            v = kvs[r][:, MEM_WIDTH:]
            for h in range(MEM_HEADS):
                kbd_ref[r, layer, :, h * N_MEM:(h + 1) * N_MEM] = jnp.where(
                    row_head == h, k_t, 0.0).astype(BF16)
                vbd_ref[r, layer, h * N_MEM:(h + 1) * N_MEM, :] = jnp.where(
                    col_head == h, v, 0.0).astype(BF16)


def _mem_kv(mem, mem_norm_g, w):
    b = mem.shape[0]
    n_layers = w.shape[0]
    nb = MEM_BATCH
    return pl.pallas_call(
        _mem_kv_kernel,
        grid=(b // nb,),
        in_specs=[
            pl.BlockSpec((nb, N_MEM, D_MODEL), lambda i: (i, 0, 0)),
            pl.BlockSpec((1, D_MODEL), lambda i: (0, 0)),
            pl.BlockSpec(w.shape, lambda i: (0, 0, 0)),
        ],
        out_specs=[
            pl.BlockSpec((nb, n_layers, MEM_WIDTH, MEM_HEADS * N_MEM), lambda i: (i, 0, 0, 0)),
            pl.BlockSpec((nb, n_layers, MEM_HEADS * N_MEM, MEM_WIDTH), lambda i: (i, 0, 0, 0)),
        ],
        out_shape=[
            jax.ShapeDtypeStruct((b, n_layers, MEM_WIDTH, MEM_HEADS * N_MEM), BF16),
            jax.ShapeDtypeStruct((b, n_layers, MEM_HEADS * N_MEM, MEM_WIDTH), BF16),
        ],
        compiler_params=pltpu.CompilerParams(
            dimension_semantics=("parallel",), vmem_limit_bytes=VMEM_LIMIT),
        name="mem_kv",
    )(mem, mem_norm_g, w)


def _split_bf16(a):
    hi = a.astype(BF16)
    return hi, (a - hi.astype(F32)).astype(BF16)


def _pool_fold_kernel(w_u_ref, g_ref, w_grp_ref, o_ref):
    grp = [_split_bf16(w_grp_ref[0, gi]) for gi in range(len(POOL_WINDOWS))]
    n = ROW_CHUNK
    for c in range(w_u_ref.shape[1] // n):
        a = w_u_ref[0, c * n:(c + 1) * n, :] * g_ref[c * n:(c + 1) * n, :]
        cols = []
        for gi, (b_hi, b_lo) in enumerate(grp):
            a_hi, a_lo = _split_bf16(a[:, gi * POOL_GROUP_DIM:(gi + 1) * POOL_GROUP_DIM])
            cols.append(_dot(a_hi, b_hi) + _dot(a_hi, b_lo) + _dot(a_lo, b_hi))
        o_ref[c * n:(c + 1) * n, :] = jnp.concatenate(cols, axis=-1).astype(BF16)


def _pool_fold(pool_w_in, g_col, pool_w_group):
    return pl.pallas_call(
        _pool_fold_kernel,
        grid=(1,),
        in_specs=[
            pl.BlockSpec((1, D_MODEL, POOL_WIDTH), lambda i: (0, 0, 0)),
            pl.BlockSpec(g_col.shape, lambda i: (0, 0)),
            pl.BlockSpec(pool_w_group.shape, lambda i: (0, 0, 0, 0)),
        ],
        out_specs=pl.BlockSpec((D_MODEL, POOL_WIDTH), lambda i: (0, 0)),
        out_shape=jax.ShapeDtypeStruct((D_MODEL, POOL_WIDTH), BF16),
        compiler_params=pltpu.CompilerParams(vmem_limit_bytes=VMEM_LIMIT),
        name="pool_fold",
    )(pool_w_in, g_col, pool_w_group)


def _window_sums(a):
    s2 = a + pltpu.roll(a, 1, 0)
    s4 = s2 + pltpu.roll(s2, 2, 0)
    s8 = s4 + pltpu.roll(s4, 4, 0)
    s16 = s8 + pltpu.roll(s8, 8, 0)
    return {2: s2, 4: s4, 8: s8, 16: s16}


def _pooled_mix(ubuf, row0, n, pos0):
    pos = pos0 + jax.lax.broadcasted_iota(jnp.int32, (n, 1), 0)
    inv_cnt = {w: 1.0 / jnp.minimum(pos + 1, w).astype(F32) for w in POOL_WINDOWS}
    lane = jax.lax.broadcasted_iota(jnp.int32, (n, LANES), 1)
    mixed = []
    for c in range(POOL_WIDTH // LANES):
        cols = slice(c * LANES, (c + 1) * LANES)
        sums = _window_sums(ubuf[row0:row0 + POOL_HALO + n, cols])
        lo_grp = (c * LANES) // POOL_GROUP_DIM
        hi_grp = ((c + 1) * LANES - 1) // POOL_GROUP_DIM
        w_lo, w_hi = POOL_WINDOWS[lo_grp], POOL_WINDOWS[hi_grp]
        pooled = sums[w_lo][POOL_HALO:] * inv_cnt[w_lo]
        if hi_grp != lo_grp:
            split = hi_grp * POOL_GROUP_DIM - c * LANES
            pooled = jnp.where(lane < split, pooled, sums[w_hi][POOL_HALO:] * inv_cnt[w_hi])
        mixed.append(pooled - ubuf[POOL_HALO + row0:POOL_HALO + row0 + n, cols])
    return jnp.concatenate(mixed, axis=-1)


def _pool_layer_kernel(x_ref, w_u_ref, w_mq_ref, w_gate_ref, scale_ref,
                       kbd_ref, vbd_ref, w_out_ref, o_ref, ubuf):
    j = pl.program_id(1)
    t = x_ref.shape[1]
    n = ROW_CHUNK
    chunks = range(t // n)

    @pl.when(j == 0)
    def _():
        ubuf[0:POOL_HALO, :] = jnp.zeros((POOL_HALO, POOL_WIDTH), F32)

    @pl.when(j > 0)
    def _():
        ubuf[0:POOL_HALO, :] = ubuf[t:t + POOL_HALO, :]

    xs = [x_ref[0, c * n:(c + 1) * n, :] for c in chunks]
    hns = [_rms_scale(x).astype(BF16) for x in xs]
    for c in chunks:
        ubuf[POOL_HALO + c * n:POOL_HALO + (c + 1) * n, :] = _dot(hns[c], w_u_ref[...])
    mqs = [_dot(hn, w_mq_ref[...]) for hn in hns]
    gates = [_silu(_dot(hn, w_gate_ref[...])) for hn in hns]
    toks = [_pooled_mix(ubuf, c * n, n, j * t + c * n) * scale_ref[...] for c in chunks]
    mem_os = [_memory_attention(mq, kbd_ref[0, 0], vbd_ref[0, 0]) for mq in mqs]
    for c in chunks:
        y = _dot((toks[c] * gates[c][:, :POOL_WIDTH]).astype(BF16), w_out_ref[0, :POOL_WIDTH, :])
        y += _dot((mem_os[c] * gates[c][:, POOL_WIDTH:]).astype(BF16),
                  w_out_ref[0, POOL_WIDTH:, :])
        o_ref[0, c * n:(c + 1) * n, :] = xs[c] + y


def _pool_layer(x, w_u, w_in, scale, kbd, vbd, w_out):
    b, s, _ = x.shape
    t = SEQ_TILE
    const = lambda shape: pl.BlockSpec(shape, lambda i, j: (0,) * len(shape))
    return pl.pallas_call(
        _pool_layer_kernel,
        grid=(b, s // t),
        in_specs=[
            pl.BlockSpec((1, t, D_MODEL), lambda i, j: (i, j, 0)),
            const(w_u.shape),
            pl.BlockSpec((D_MODEL, MEM_WIDTH), lambda i, j: (0, POOL_WIDTH // MEM_WIDTH)),
            pl.BlockSpec((D_MODEL, MIX_WIDTH), lambda i, j: (0, 1)),
            const(scale.shape),
            pl.BlockSpec((1, 1) + kbd.shape[2:], lambda i, j: (i, 0, 0, 0)),
            pl.BlockSpec((1, 1) + vbd.shape[2:], lambda i, j: (i, 0, 0, 0)),
            pl.BlockSpec((1,) + w_out.shape[1:], lambda i, j: (0, 0, 0)),
        ],
        out_specs=pl.BlockSpec((1, t, D_MODEL), lambda i, j: (i, j, 0)),
        out_shape=jax.ShapeDtypeStruct(x.shape, F32),
        scratch_shapes=[pltpu.VMEM((POOL_HALO + t, POOL_WIDTH), F32)],
        compiler_params=pltpu.CompilerParams(
            dimension_semantics=("parallel", "arbitrary"), vmem_limit_bytes=VMEM_LIMIT),
        name="pool_layer",
    )(x, w_u, w_in, w_in, scale, kbd, vbd, w_out)


def _mla_prep_kernel(w_t_ref, g_ref, w_pm_ref, w_fm_ref):
    o_ckv = Q_LORA
    o_kr = o_ckv + KV_LORA
    o_mq = o_kr + QK_ROPE
    o_gate = o_mq + MEM_WIDTH
    g = g_ref[...]
    w_fm_ref[0:Q_LORA, :] = (w_t_ref[0:o_ckv, :] * g).astype(BF16)
    w_fm_ref[Q_LORA:, :] = (w_t_ref[o_kr:o_mq, :] * g).astype(BF16)
    blocks = ([(o_gate + r, 1.0) for r in range(0, MIX_WIDTH, LANES)]
              + [(o_ckv + r, 1.0) for r in range(0, KV_LORA, LANES)]
              + [(o_mq + r, MEM_Q_SCALE) for r in range(0, MEM_WIDTH, LANES)])
    for dst, (src, scale) in enumerate(blocks):
        blk = w_t_ref[src:src + LANES, :] * (g * scale)
        w_pm_ref[:, dst * LANES:(dst + 1) * LANES] = blk.T.astype(BF16)


def _mla_prep(w_t, g_row):
    n_in, d = w_t.shape
    n_pm = MIX_WIDTH + KV_LORA + MEM_WIDTH
    n_fm = Q_LORA + QK_ROPE
    return pl.pallas_call(
        _mla_prep_kernel,
        grid=(1,),
        in_specs=[pl.BlockSpec((n_in, d), lambda i: (0, 0)),
                  pl.BlockSpec(g_row.shape, lambda i: (0, 0))],
        out_specs=[pl.BlockSpec((d, n_pm), lambda i: (0, 0)),
                   pl.BlockSpec((n_fm, d), lambda i: (0, 0))],
        out_shape=[jax.ShapeDtypeStruct((d, n_pm), BF16), jax.ShapeDtypeStruct((n_fm, d), BF16)],
        compiler_params=pltpu.CompilerParams(vmem_limit_bytes=VMEM_LIMIT),
        name="mla_prep",
    )(w_t, g_row)


def _rope_t(x, cos_t, sin_t):
    half = QK_ROPE // 2
    x1, x2 = x[:half], x[half:]
    return x1 * cos_t - x2 * sin_t, x2 * cos_t + x1 * sin_t


def _mla_pre_kernel(h_ref, pos_ref, inv_freq_ref, w_fm_ref, w_pm_ref, w_uq_t_ref, w_uk_ref,
                    w_uv_t_ref, kbd_ref, vbd_ref, w_out_ref,
                    q_ref, k_ref, v_ref, sg_ref, hp_ref):
    w_cq_t_ref, w_kr_t_ref = w_fm_ref.at[0:Q_LORA], w_fm_ref.at[Q_LORA:Q_LORA + QK_ROPE]
    w_gate_ref = w_pm_ref.at[:, 0:MIX_WIDTH]
    w_ckv_ref = w_pm_ref.at[:, MIX_WIDTH:MIX_WIDTH + KV_LORA]
    w_mq_ref = w_pm_ref.at[:, MIX_WIDTH + KV_LORA:MIX_WIDTH + KV_LORA + MEM_WIDTH]
    w_out_mem_ref = w_out_ref.at[0]
    t = h_ref.shape[1]
    n = 2 * ROW_CHUNK
    chunks = range(t // n)
    heads = range(MLA_HEADS)
    half = QK_ROPE // 2
    rows = [slice(c * n, (c + 1) * n) for c in chunks]

    hs = [h_ref[0, rows[c], :] for c in chunks]
    xns = [_rms_scale(h) for h in hs]
    hns = [xn.astype(BF16) for xn in xns]
    mqs =[_dot(hn, w_mq_ref[...]) for hn in hns]
    mem_os = [_memory_attention(mq, kbd_ref[0, 0], vbd_ref[0, 0]) for mq in mqs]
    gates = [_silu(_dot(hn, w_gate_ref[...])) for hn in hns]
    for c in chunks:
        sg_ref[0, rows[c], :] = gates[c][:, :MLA_WIDTH]
        hp_ref[0, rows[c], :] = hs[c] + _dot(
            (mem_os[c] * gates[c][:, MLA_WIDTH:]).astype(BF16), w_out_mem_ref[...])

    hn_ts = [xn.T.astype(BF16) for xn in xns]
    cq_ts = [_dot(w_cq_t_ref[...], hn_t) for hn_t in hn_ts]
    kr_ts = [_dot(w_kr_t_ref[...], hn_t) for hn_t in hn_ts]
    ckvns = [_rms_scale(_dot(hn, w_ckv_ref[...])).astype(BF16) for hn in hns]
    angs = [pos_ref[0, :, rows[c]].astype(F32) * inv_freq_ref[...] for c in chunks]
    cos_ts, sin_ts = [jnp.cos(a) for a in angs], [jnp.sin(a) for a in angs]
    cqn_ts = [(cq * jax.lax.rsqrt(jnp.mean(cq * cq, axis=0, keepdims=True) + EPS)
               ).astype(BF16) for cq in cq_ts]

    q_ts = [_dot(w_uq_t_ref[...], cqn_t) for cqn_t in cqn_ts]
    for c in chunks:
        for hd in heads:
            base = hd * QK_DIM
            r1, r2 = _rope_t(q_ts[c][base + QK_NOPE:base + QK_DIM], cos_ts[c], sin_ts[c])
            q_ref[0, hd, 0:QK_NOPE, rows[c]] = q_ts[c][base:base + QK_NOPE].astype(BF16)
            q_ref[0, hd, QK_NOPE:QK_NOPE + half, rows[c]] = r1.astype(BF16)
            q_ref[0, hd, QK_NOPE + half:QK_DIM, rows[c]] = r2.astype(BF16)

    k_nopes = [_dot(ckvn, w_uk_ref[...]) for ckvn in ckvns]
    v_ts = [_dot_nt(w_uv_t_ref[...], ckvn) for ckvn in ckvns]
    for c in chunks:
        r1, r2 = _rope_t(kr_ts[c], cos_ts[c], sin_ts[c])
        k_rope = jnp.concatenate([r1, r2, jnp.zeros((LANES - QK_ROPE, n), F32)], axis=0).T
        k_rope = k_rope[:, :QK_ROPE]
        kt, off = divmod(c * n, ATTN_TILE)
        for hd in heads:
            k_ref[0, hd, rows[c], :] = jnp.concatenate(
                [k_nopes[c][:, hd * QK_NOPE:(hd + 1) * QK_NOPE], k_rope], axis=-1).astype(BF16)
            v_ref[0, hd, kt, 0:V_HEAD, off:off + n] = (
                v_ts[c][hd * V_HEAD:(hd + 1) * V_HEAD].astype(BF16))
            v_ref[0, hd, kt, V_HEAD:V_EXT, off:off + n] = jnp.ones((V_EXT - V_HEAD, n), BF16)


def _mla_pre(h, pos, inv_freq, w_fm, w_pm, w_uq_t, w_uk, w_uv_t, kbd, vbd, w_out):
    b, s, _ = h.shape
    t = SEQ_TILE
    n_kt = t // ATTN_TILE
    const = lambda shape: pl.BlockSpec(shape, lambda i, j: (0,) * len(shape))
    row = lambda width: pl.BlockSpec((1, t, width), lambda i, j: (i, j, 0))
    return pl.pallas_call(
        _mla_pre_kernel,
        grid=(b, s // t),
        in_specs=[
            row(D_MODEL),
            pl.BlockSpec((1, 1, t), lambda i, j: (i, 0, j)), const(inv_freq.shape),
            const(w_fm.shape), const(w_pm.shape),
            const(w_uq_t.shape), const(w_uk.shape), const(w_uv_t.shape),
            pl.BlockSpec((1, 1) + kbd.shape[2:], lambda i, j: (i, 1, 0, 0)),
            pl.BlockSpec((1, 1) + vbd.shape[2:], lambda i, j: (i, 1, 0, 0)),
            pl.BlockSpec((1, MEM_WIDTH, D_MODEL), lambda i, j: (1, MLA_WIDTH // MEM_WIDTH, 0)),
        ],
        out_specs=[
            pl.BlockSpec((1, MLA_HEADS, QK_DIM, t), lambda i, j: (i, 0, 0, j)),
            pl.BlockSpec((1, MLA_HEADS, t, QK_DIM), lambda i, j: (i, 0, j, 0)),
            pl.BlockSpec((1, MLA_HEADS, n_kt, V_EXT, ATTN_TILE), lambda i, j: (i, 0, j, 0, 0)),
            row(MLA_WIDTH), row(D_MODEL),
        ],
        out_shape=[
            jax.ShapeDtypeStruct((b, MLA_HEADS, QK_DIM, s), BF16),
            jax.ShapeDtypeStruct((b, MLA_HEADS, s, QK_DIM), BF16),
            jax.ShapeDtypeStruct((b, MLA_HEADS, s // ATTN_TILE, V_EXT, ATTN_TILE), BF16),
            jax.ShapeDtypeStruct((b, s, MLA_WIDTH), F32),
            jax.ShapeDtypeStruct((b, s, D_MODEL), F32),
        ],
        compiler_params=pltpu.CompilerParams(
            dimension_semantics=("parallel", "parallel"), vmem_limit_bytes=VMEM_LIMIT),
        name="mla_pre",
    )(h, pos, inv_freq, w_fm, w_pm, w_uq_t, w_uk, w_uv_t, kbd, vbd, w_out)


def _sublane_allreduce(x, op):
    for shift in (4, 2, 1):
        x = op(x, pltpu.roll(x, shift, 0))
    return x


def _mla_attn_kernel(q_ref, k_ref, v_ref, sg_ref, hp_ref, w_out_ref, g_ref, o_ref,
                     acc_sc, tok_sc):
    i = pl.program_id(1)
    tq = q_ref.shape[3]
    tk = v_ref.shape[4]
    half = tk // 2
    heads = range(MLA_HEADS)

    def step(jt, carry, k0, kn, q0, qn, masked):
        m_old = carry
        ks = slice(jt * tk + k0, jt * tk + k0 + kn)
        qs = slice(q0, q0 + qn)
        s = [_dot(k_ref[0, hd, ks, :], q_ref[0, hd, :, qs]) for hd in heads]
        if masked:
            k_chunk = (k0 + jax.lax.broadcasted_iota(jnp.int32, (kn, qn), 0)) // CHUNK
            q_chunk = (q0 + jax.lax.broadcasted_iota(jnp.int32, (kn, qn), 1)) // CHUNK
            mask = k_chunk <= q_chunk
        m_new, alpha, p = [], [], []
        for hd in heads:
            sh = jnp.where(mask, s[hd], NEG_INF) if masked else s[hd]
            s3 = sh.reshape(kn // SUBLANES, SUBLANES, qn)
            m_prev = m_old[hd][:, qs]
            m = jnp.maximum(m_prev, _sublane_allreduce(jnp.max(s3, axis=0), jnp.maximum))
            alpha.append(jnp.exp2(m_prev - m))
            p.append(jnp.exp2(s3 - m[None]).reshape(kn, qn).astype(BF16))
            if qn != tq:
                m = jnp.concatenate(
                    ([m_old[hd][:, :q0]] if q0 else []) + [m]
                    + ([m_old[hd][:, q0 + qn:]] if q0 + qn < tq else []), axis=1)
            m_new.append(m)
        pv = [_dot(v_ref[0, hd, jt, :, k0:k0 + kn], p[hd]) for hd in heads]
        for hd in heads:
            acc = acc_sc[hd, :, qs].reshape(V_EXT // SUBLANES, SUBLANES, qn)
            acc_sc[hd, :, qs] = (alpha[hd][None] * acc).reshape(V_EXT, qn) + pv[hd]
        return tuple(m_new)

    def query_tile(n_full):
        acc_sc[...] = jnp.zeros(acc_sc.shape, F32)
        carry = tuple(jnp.full((SUBLANES, tq), NEG_INF, F32) for _ in heads)
        for jt in range(n_full):
            carry = step(jt, carry, 0, tk, 0, tq, False)
        carry = step(n_full, carry, 0, half, 0, tq, True)
        step(n_full, carry, half, tk - half, half, tq - half, True)

        for hd in heads:
            l_tot = acc_sc[hd, V_HEAD:V_HEAD + SUBLANES, :]
            o_t = acc_sc[hd, 0:V_HEAD, :].reshape(V_HEAD // SUBLANES, SUBLANES, tq) / l_tot[None]
            tok_sc[:, hd * V_HEAD:(hd + 1) * V_HEAD] = o_t.reshape(V_HEAD, tq).T

        n = tq // 2
        rows = [slice(c * n, (c + 1) * n) for c in range(tq // n)]
        branches = [(tok_sc[r, :] * sg_ref[0, r, :]).astype(BF16) for r in rows]
        outs = [hp_ref[0, r, :] + _dot(br, w_out_ref[0]) for r, br in zip(rows, branches)]
        for r, out in zip(rows, outs):
            o_ref[0, r, :] = _rms_norm(out, g_ref[...])

    for n_full in range(k_ref.shape[2] // tk):
        pl.when(i == n_full)(functools.partial(query_tile, n_full))


def _mla_attn(q_t, k, v_t, sg, hp, w_out, final_g):
    b, _, s, _ = k.shape
    t = ATTN_TILE
    const = lambda shape: pl.BlockSpec(shape, lambda i, j: (0,) * len(shape))
    row = lambda width: pl.BlockSpec((1, t, width), lambda i, j: (i, j, 0))
    return pl.pallas_call(
        _mla_attn_kernel,
        grid=(b, s // t),
        in_specs=[
            pl.BlockSpec((1, MLA_HEADS, QK_DIM, t), lambda i, j: (i, 0, 0, j)),
            pl.BlockSpec((1, MLA_HEADS, s, QK_DIM), lambda i, j: (i, 0, 0, 0)),
            pl.BlockSpec((1,) + v_t.shape[1:], lambda i, j: (i, 0, 0, 0, 0)),
            row(MLA_WIDTH), row(D_MODEL),
            pl.BlockSpec((1, MLA_WIDTH, D_MODEL), lambda i, j: (1, 0, 0)), const(final_g.shape),
        ],
        out_specs=row(D_MODEL),
        out_shape=jax.ShapeDtypeStruct((b, s, D_MODEL), F32),
        scratch_shapes=[
            pltpu.VMEM((MLA_HEADS, V_EXT, t), F32),
            pltpu.VMEM((t, MLA_WIDTH), F32),
        ],
        compiler_params=pltpu.CompilerParams(
            dimension_semantics=("parallel", "arbitrary"), vmem_limit_bytes=VMEM_LIMIT),
        name="mla_attn",
    )(q_t, k, v_t, sg, hp, w_out, final_g)


def kernel(x, mem, positions, norm_g, mem_norm_g, w_mem_kv, w_out, pool_w_in, pool_w_group,
           pool_scale, mla_w_in, mla_q_norm_g, mla_w_uq, mla_kv_norm_g, mla_w_ukv, final_norm_g):
    depth = w_mem_kv.shape[0]
    assert depth == 2 and pool_w_in.shape[0] == 1 and mla_w_in.shape[0] == 1
    bf = lambda a: a.astype(BF16)
    row = lambda a: a.reshape(1, -1)

    inv_freq = ROPE_THETA ** (-(jnp.arange(0, QK_ROPE, 2, dtype=F32) / QK_ROPE))
    kbd, vbd = _mem_kv(mem, row(mem_norm_g), bf(w_mem_kv))
    w_out_bf = bf(w_out)

    g0 = norm_g[0][:, None]
    mq_cols = (jnp.arange(POOL_IN) >= POOL_WIDTH) & (jnp.arange(POOL_IN) < POOL_WIDTH + MEM_WIDTH)
    pool_w = bf(pool_w_in[0] * g0 * jnp.where(mq_cols, MEM_Q_SCALE, 1.0))
    h1 = _pool_layer(x, _pool_fold(pool_w_in, g0, pool_w_group), pool_w, row(pool_scale[0]),
                     kbd, vbd, w_out_bf)

    w_pm, w_fm = _mla_prep(mla_w_in[0].T, row(norm_g[1]))
    w_ukv = (mla_w_ukv[0] * mla_kv_norm_g[0][:, None]).reshape(
        KV_LORA, MLA_HEADS, QK_NOPE + V_HEAD)
    w_uk = bf(w_ukv[:, :, :QK_NOPE].reshape(KV_LORA, MLA_HEADS * QK_NOPE))
    w_uv_t = bf(w_ukv[:, :, QK_NOPE:].reshape(KV_LORA, MLA_WIDTH).T)
    w_uq_t = bf((mla_w_uq[0] * (mla_q_norm_g[0][:, None] * Q_SCALE)).T)
    q_t, k, v_t, sg, hp = _mla_pre(
        h1, positions[:, None, :], inv_freq[:, None], w_fm, w_pm, w_uq_t, w_uk, w_uv_t,
        kbd, vbd, w_out_bf)

    return _mla_attn(q_t, k, v_t, sg, hp, w_out_bf, row(final_norm_g))
```

```python
import functools
import math

import jax
import jax.numpy as jnp
from jax.experimental import pallas as pl
from jax.experimental.pallas import tpu as pltpu

D_MODEL = 1024
CHUNK = 64
EPS = 1e-6
NEG_INF = -1e30

N_MEM = 256
MEM_HEADS = 4
MEM_HEAD_DIM = 64
MEM_WIDTH = MEM_HEADS * MEM_HEAD_DIM
MEM_BATCH = 4

POOL_WINDOWS = (2, 4, 8, 16)
POOL_GROUP_DIM = 192
POOL_WIDTH = len(POOL_WINDOWS) * POOL_GROUP_DIM
POOL_HALO = 16

MLA_HEADS = 6
QK_NOPE = 128
QK_ROPE = 64
QK_DIM = QK_NOPE + QK_ROPE
V_HEAD = 128
V_EXT = V_HEAD + 16
Q_LORA = 384
KV_LORA = 256
MLA_WIDTH = MLA_HEADS * V_HEAD
ROPE_THETA = 10000.0
LOG2_E = math.log2(math.e)
Q_SCALE = QK_DIM ** -0.5 * LOG2_E
MEM_Q_SCALE = MEM_HEAD_DIM ** -0.5 * LOG2_E

MIX_WIDTH = POOL_WIDTH + MEM_WIDTH
POOL_IN = POOL_WIDTH + MEM_WIDTH + MIX_WIDTH

LANES = 128
SUBLANES = 8
SEQ_TILE = 1024
ROW_CHUNK = 256
ATTN_TILE = 512
VMEM_LIMIT = 56 * 1024 * 1024

BF16 = jnp.bfloat16
F32 = jnp.float32


def _rms_scale(x):
    ms = jnp.mean(x * x, axis=-1, keepdims=True)
    return x * jax.lax.rsqrt(ms + EPS)


def _rms_norm(x, g):
    return _rms_scale(x) * g


def _silu(x):
    return x / (1.0 + jnp.exp2(x * -LOG2_E))


def _dot(a, b):
    return jnp.dot(a, b, preferred_element_type=F32)


def _dot_nt(a, b):
    return jax.lax.dot_general(a, b, (((1,), (1,)), ((), ())), preferred_element_type=F32)


def _softmax_rows(s):
    e = jnp.exp2(s - jnp.max(s, axis=-1, keepdims=True))
    return (e / jnp.sum(e, axis=-1, keepdims=True)).astype(BF16)


def _memory_attention_heads(mq, k_t, v):
    d = MEM_HEAD_DIM
    outs = []
    for h in range(MEM_HEADS):
        p = _softmax_rows(_dot(mq[:, h * d:(h + 1) * d].astype(BF16), k_t[h]))
        outs.append(_dot(p, v[h]))
    return jnp.concatenate(outs, axis=-1)


def _memory_attention(mq, kbd, vbd):
    s = _dot(mq.astype(BF16), kbd)
    probs = [_softmax_rows(s[:, h * N_MEM:(h + 1) * N_MEM]) for h in range(MEM_HEADS)]
    return _dot(jnp.concatenate(probs, axis=-1), vbd)


def _mem_kv_kernel(mem_ref, g_ref, w_ref, kt0_ref, v0_ref, kbd1_ref, vbd1_ref):
    rows = range(mem_ref.shape[0])
    mem_ns = [_rms_norm(mem_ref[r], g_ref[...]).astype(BF16) for r in rows]
    d = MEM_HEAD_DIM
    row_head = jax.lax.broadcasted_iota(jnp.int32, (MEM_WIDTH, N_MEM), 0) // d
    col_head = jax.lax.broadcasted_iota(jnp.int32, (N_MEM, MEM_WIDTH), 1) // d
    for layer in range(w_ref.shape[0]):
        kvs = [_dot(mem_n, w_ref[layer]) for mem_n in mem_ns]
        for r in rows:
            k_t = kvs[r][:, :MEM_WIDTH].T
            v = kvs[r][:, MEM_WIDTH:]
            for h in range(MEM_HEADS):
                if layer == 0:
                    kt0_ref[r, h] = k_t[h * d:(h + 1) * d, :].astype(BF16)
                    v0_ref[r, h] = v[:, h * d:(h + 1) * d].astype(BF16)
                else:
                    kbd1_ref[r, :, h * N_MEM:(h + 1) * N_MEM] = jnp.where(
                        row_head == h, k_t, 0.0).astype(BF16)
                    vbd1_ref[r, h * N_MEM:(h + 1) * N_MEM, :] = jnp.where(
                        col_head == h, v, 0.0).astype(BF16)


def _mem_kv(mem, mem_norm_g, w):
    b = mem.shape[0]
    nb = MEM_BATCH
    shapes = [(MEM_HEADS, MEM_HEAD_DIM, N_MEM), (MEM_HEADS, N_MEM, MEM_HEAD_DIM),
              (MEM_WIDTH, MEM_HEADS * N_MEM), (MEM_HEADS * N_MEM, MEM_WIDTH)]
    return pl.pallas_call(
        _mem_kv_kernel,
        grid=(b // nb,),
        in_specs=[
            pl.BlockSpec((nb, N_MEM, D_MODEL), lambda i: (i, 0, 0)),
            pl.BlockSpec((1, D_MODEL), lambda i: (0, 0)),
            pl.BlockSpec(w.shape, lambda i: (0, 0, 0)),
        ],
        out_specs=[pl.BlockSpec((nb,) + s, lambda i, n=len(s): (i,) + (0,) * n) for s in shapes],
        out_shape=[jax.ShapeDtypeStruct((b,) + s, BF16) for s in shapes],
        compiler_params=pltpu.CompilerParams(
            dimension_semantics=("parallel",), vmem_limit_bytes=VMEM_LIMIT),
        name="mem_kv",
    )(mem, mem_norm_g, w)


def _split_bf16(a):
    hi = a.astype(BF16)
    return hi, (a - hi.astype(F32)).astype(BF16)


def _pool_fold_kernel(w_u_ref, g_ref, w_grp_ref, o_ref):
    grp = [_split_bf16(w_grp_ref[0, gi]) for gi in range(len(POOL_WINDOWS))]
    n = ROW_CHUNK
    for c in range(w_u_ref.shape[1] // n):
        a = w_u_ref[0, c * n:(c + 1) * n, :] * g_ref[c * n:(c + 1) * n, :]
        cols = []
        for gi, (b_hi, b_lo) in enumerate(grp):
            a_hi, a_lo = _split_bf16(a[:, gi * POOL_GROUP_DIM:(gi + 1) * POOL_GROUP_DIM])
            cols.append(_dot(a_hi, b_hi) + _dot(a_hi, b_lo) + _dot(a_lo, b_hi))
        o_ref[c * n:(c + 1) * n, :] = jnp.concatenate(cols, axis=-1).astype(BF16)


def _pool_fold(pool_w_in, g_col, pool_w_group):
    return pl.pallas_call(
        _pool_fold_kernel,
        grid=(1,),
        in_specs=[
            pl.BlockSpec((1, D_MODEL, POOL_WIDTH), lambda i: (0, 0, 0)),
            pl.BlockSpec(g_col.shape, lambda i: (0, 0)),
            pl.BlockSpec(pool_w_group.shape, lambda i: (0, 0, 0, 0)),
        ],
        out_specs=pl.BlockSpec((D_MODEL, POOL_WIDTH), lambda i: (0, 0)),
        out_shape=jax.ShapeDtypeStruct((D_MODEL, POOL_WIDTH), BF16),
        compiler_params=pltpu.CompilerParams(vmem_limit_bytes=VMEM_LIMIT),
        name="pool_fold",
    )(pool_w_in, g_col, pool_w_group)


def _window_sums(a):
    s2 = a + pltpu.roll(a, 1, 0)
    s4 = s2 + pltpu.roll(s2, 2, 0)
    s8 = s4 + pltpu.roll(s4, 4, 0)
    s16 = s8 + pltpu.roll(s8, 8, 0)
    return {2: s2, 4: s4, 8: s8, 16: s16}


def _pooled_mix(ubuf, row0, n, pos0):
    pos = pos0 + jax.lax.broadcasted_iota(jnp.int32, (n, 1), 0)
    inv_cnt = {w: 1.0 / jnp.minimum(pos + 1, w).astype(F32) for w in POOL_WINDOWS}
    lane = jax.lax.broadcasted_iota(jnp.int32, (n, LANES), 1)
    mixed = []
    for c in range(POOL_WIDTH // LANES):
        cols = slice(c * LANES, (c + 1) * LANES)
        sums = _window_sums(ubuf[row0:row0 + POOL_HALO + n, cols])
        lo_grp = (c * LANES) // POOL_GROUP_DIM
        hi_grp = ((c + 1) * LANES - 1) // POOL_GROUP_DIM
        w_lo, w_hi = POOL_WINDOWS[lo_grp], POOL_WINDOWS[hi_grp]
        pooled = sums[w_lo][POOL_HALO:] * inv_cnt[w_lo]
        if hi_grp != lo_grp:
            split = hi_grp * POOL_GROUP_DIM - c * LANES
            pooled = jnp.where(lane < split, pooled, sums[w_hi][POOL_HALO:] * inv_cnt[w_hi])
        mixed.append(pooled - ubuf[POOL_HALO + row0:POOL_HALO + row0 + n, cols])
    return jnp.concatenate(mixed, axis=-1)


def _pool_layer_kernel(x_ref, w_u_ref, w_mq_ref, w_gate_ref, scale_ref,
                       kt_ref, v_ref, w_out_ref, o_ref, ubuf):
    j = pl.program_id(1)
    t = x_ref.shape[1]
    n = ROW_CHUNK
    chunks = range(t // n)

    @pl.when(j == 0)
    def _():
        ubuf[0:POOL_HALO, :] = jnp.zeros((POOL_HALO, POOL_WIDTH), F32)

    @pl.when(j > 0)
    def _():
        ubuf[0:POOL_HALO, :] = ubuf[t:t + POOL_HALO, :]

    xs = [x_ref[0, c * n:(c + 1) * n, :] for c in chunks]
    hns = [_rms_scale(x).astype(BF16) for x in xs]
    for c in chunks:
        ubuf[POOL_HALO + c * n:POOL_HALO + (c + 1) * n, :] = _dot(hns[c], w_u_ref[...])
    mqs = [_dot(hn, w_mq_ref[...]) for hn in hns]
    gates = [_silu(_dot(hn, w_gate_ref[...])) for hn in hns]
    toks = [_pooled_mix(ubuf, c * n, n, j * t + c * n) * scale_ref[...] for c in chunks]
    mem_os = [_memory_attention_heads(mq, kt_ref.at[0], v_ref.at[0]) for mq in mqs]
    for c in chunks:
        y = _dot((toks[c] * gates[c][:, :POOL_WIDTH]).astype(BF16), w_out_ref[0, :POOL_WIDTH, :])
        y += _dot((mem_os[c] * gates[c][:, POOL_WIDTH:]).astype(BF16),
                  w_out_ref[0, POOL_WIDTH:, :])
        o_ref[0, c * n:(c + 1) * n, :] = xs[c] + y


def _pool_layer(x, w_u, w_in, scale, k_t, v, w_out):
    b, s, _ = x.shape
    t = SEQ_TILE
    const = lambda shape: pl.BlockSpec(shape, lambda i, j: (0,) * len(shape))
    return pl.pallas_call(
        _pool_layer_kernel,
        grid=(b, s // t),
        in_specs=[
            pl.BlockSpec((1, t, D_MODEL), lambda i, j: (i, j, 0)),
            const(w_u.shape),
            pl.BlockSpec((D_MODEL, MEM_WIDTH), lambda i, j: (0, POOL_WIDTH // MEM_WIDTH)),
            pl.BlockSpec((D_MODEL, MIX_WIDTH), lambda i, j: (0, 1)),
            const(scale.shape),
            pl.BlockSpec((1,) + k_t.shape[1:], lambda i, j: (i, 0, 0, 0)),
            pl.BlockSpec((1,) + v.shape[1:], lambda i, j: (i, 0, 0, 0)),
            pl.BlockSpec((1,) + w_out.shape[1:], lambda i, j: (0, 0, 0)),
        ],
        out_specs=pl.BlockSpec((1, t, D_MODEL), lambda i, j: (i, j, 0)),
        out_shape=jax.ShapeDtypeStruct(x.shape, F32),
        scratch_shapes=[pltpu.VMEM((POOL_HALO + t, POOL_WIDTH), F32)],
        compiler_params=pltpu.CompilerParams(
            dimension_semantics=("parallel", "arbitrary"), vmem_limit_bytes=VMEM_LIMIT),
        name="pool_layer",
    )(x, w_u, w_in, w_in, scale, k_t, v, w_out)


def _mla_prep_kernel(w_t_ref, g_ref, w_pm_ref, w_fm_ref):
    o_ckv = Q_LORA
    o_kr = o_ckv + KV_LORA
    o_mq = o_kr + QK_ROPE
    o_gate = o_mq + MEM_WIDTH
    g = g_ref[...]
    w_fm_ref[0:Q_LORA, :] = (w_t_ref[0:o_ckv, :] * g).astype(BF16)
    w_fm_ref[Q_LORA:, :] = (w_t_ref[o_kr:o_mq, :] * g).astype(BF16)
    blocks = ([(o_gate + r, 1.0) for r in range(0, MIX_WIDTH, LANES)]
              + [(o_ckv + r, 1.0) for r in range(0, KV_LORA, LANES)]
              + [(o_mq + r, MEM_Q_SCALE) for r in range(0, MEM_WIDTH, LANES)])
    for dst, (src, scale) in enumerate(blocks):
        blk = w_t_ref[src:src + LANES, :] * (g * scale)
        w_pm_ref[:, dst * LANES:(dst + 1) * LANES] = blk.T.astype(BF16)


def _mla_prep(w_t, g_row):
    n_in, d = w_t.shape
    n_pm = MIX_WIDTH + KV_LORA + MEM_WIDTH
    n_fm = Q_LORA + QK_ROPE
    return pl.pallas_call(
        _mla_prep_kernel,
        grid=(1,),
        in_specs=[pl.BlockSpec((n_in, d), lambda i: (0, 0)),
                  pl.BlockSpec(g_row.shape, lambda i: (0, 0))],
        out_specs=[pl.BlockSpec((d, n_pm), lambda i: (0, 0)),
                   pl.BlockSpec((n_fm, d), lambda i: (0, 0))],
        out_shape=[jax.ShapeDtypeStruct((d, n_pm), BF16), jax.ShapeDtypeStruct((n_fm, d), BF16)],
        compiler_params=pltpu.CompilerParams(vmem_limit_bytes=VMEM_LIMIT),
        name="mla_prep",
    )(w_t, g_row)


def _rope_t(x, cos_t, sin_t):
    half = QK_ROPE // 2
    x1, x2 = x[:half], x[half:]
    return x1 * cos_t - x2 * sin_t, x2 * cos_t + x1 * sin_t


def _mla_pre_kernel(h_ref, pos_ref, inv_freq_ref, w_fm_ref, w_pm_ref, w_uq_t_ref, w_uk_ref,
                    w_uv_t_ref, kbd_ref, vbd_ref, w_out_ref,
                    q_ref, k_ref, v_ref, sg_ref, hp_ref):
    w_cq_t_ref, w_kr_t_ref = w_fm_ref.at[0:Q_LORA], w_fm_ref.at[Q_LORA:Q_LORA + QK_ROPE]
    w_gate_ref = w_pm_ref.at[:, 0:MIX_WIDTH]
    w_ckv_ref = w_pm_ref.at[:, MIX_WIDTH:MIX_WIDTH + KV_LORA]
    w_mq_ref = w_pm_ref.at[:, MIX_WIDTH + KV_LORA:MIX_WIDTH + KV_LORA + MEM_WIDTH]
    w_out_mem_ref = w_out_ref.at[0]
    t = h_ref.shape[1]
    n = 2 * ROW_CHUNK
    chunks = range(t // n)
    heads = range(MLA_HEADS)
    half = QK_ROPE // 2
    rows = [slice(c * n, (c + 1) * n) for c in chunks]

    hs = [h_ref[0, rows[c], :] for c in chunks]
    xns = [_rms_scale(h) for h in hs]
    hns = [xn.astype(BF16) for xn in xns]
    mqs = [_dot(hn, w_mq_ref[...]) for hn in hns]
    mem_os = [_memory_attention(mq, kbd_ref[0], vbd_ref[0]) for mq in mqs]
    gates = [_silu(_dot(hn, w_gate_ref[...])) for hn in hns]
    for c in chunks:
        sg_ref[0, rows[c], :] = gates[c][:, :MLA_WIDTH]
        hp_ref[0, rows[c], :] = hs[c] + _dot(
            (mem_os[c] * gates[c][:, MLA_WIDTH:]).astype(BF16), w_out_mem_ref[...])

    hn_ts = [xn.T.astype(BF16) for xn in xns]
    cq_ts = [_dot(w_cq_t_ref[...], hn_t) for hn_t in hn_ts]
    kr_ts = [_dot(w_kr_t_ref[...], hn_t) for hn_t in hn_ts]
    ckvns = [_rms_scale(_dot(hn, w_ckv_ref[...])).astype(BF16) for hn in hns]
    angs = [pos_ref[0, :, rows[c]].astype(F32) * inv_freq_ref[...] for c in chunks]
    cos_ts, sin_ts = [jnp.cos(a) for a in angs], [jnp.sin(a) for a in angs]
    cqn_ts = [(cq * jax.lax.rsqrt(jnp.mean(cq * cq, axis=0, keepdims=True) + EPS)
               ).astype(BF16) for cq in cq_ts]

    q_ts = [_dot(w_uq_t_ref[...], cqn_t) for cqn_t in cqn_ts]
    for c in chunks:
        for hd in heads:
            base = hd * QK_DIM
            r1, r2 = _rope_t(q_ts[c][base + QK_NOPE:base + QK_DIM], cos_ts[c], sin_ts[c])
            q_ref[0, hd, 0:QK_NOPE, rows[c]] = q_ts[c][base:base + QK_NOPE].astype(BF16)
            q_ref[0, hd, QK_NOPE:QK_NOPE + half, rows[c]] = r1.astype(BF16)
            q_ref[0, hd, QK_NOPE + half:QK_DIM, rows[c]] = r2.astype(BF16)

    k_nopes = [_dot(ckvn, w_uk_ref[...]) for ckvn in ckvns]
    v_ts = [_dot_nt(w_uv_t_ref[...], ckvn) for ckvn in ckvns]
    for c in chunks:
        r1, r2 = _rope_t(kr_ts[c], cos_ts[c], sin_ts[c])
        k_rope = jnp.concatenate([r1, r2, jnp.zeros((LANES - QK_ROPE, n), F32)], axis=0).T
        k_rope = k_rope[:, :QK_ROPE]
        kt, off = divmod(c * n, ATTN_TILE)
        for hd in heads:
            k_ref[0, hd, rows[c], :] = jnp.concatenate(
                [k_nopes[c][:, hd * QK_NOPE:(hd + 1) * QK_NOPE], k_rope], axis=-1).astype(BF16)
            v_ref[0, hd, kt, 0:V_HEAD, off:off + n] = (
                v_ts[c][hd * V_HEAD:(hd + 1) * V_HEAD].astype(BF16))
            v_ref[0, hd, kt, V_HEAD:V_EXT, off:off + n] = jnp.ones((V_EXT - V_HEAD, n), BF16)


def _mla_pre(h, pos, inv_freq, w_fm, w_pm, w_uq_t, w_uk, w_uv_t, kbd, vbd, w_out):
    b, s, _ = h.shape
    t = SEQ_TILE
    n_kt = t // ATTN_TILE
    const = lambda shape: pl.BlockSpec(shape, lambda i, j: (0,) * len(shape))
    row = lambda width: pl.BlockSpec((1, t, width), lambda i, j: (i, j, 0))
    return pl.pallas_call(
        _mla_pre_kernel,
        grid=(b, s // t),
        in_specs=[
            row(D_MODEL),
            pl.BlockSpec((1, 1, t), lambda i, j: (i, 0, j)), const(inv_freq.shape),
            const(w_fm.shape), const(w_pm.shape),
            const(w_uq_t.shape), const(w_uk.shape), const(w_uv_t.shape),
            pl.BlockSpec((1,) + kbd.shape[1:], lambda i, j: (i, 0, 0)),
            pl.BlockSpec((1,) + vbd.shape[1:], lambda i, j: (i, 0, 0)),
            pl.BlockSpec((1, MEM_WIDTH, D_MODEL), lambda i, j: (1, MLA_WIDTH // MEM_WIDTH, 0)),
        ],
        out_specs=[
            pl.BlockSpec((1, MLA_HEADS, QK_DIM, t), lambda i, j: (i, 0, 0, j)),
            pl.BlockSpec((1, MLA_HEADS, t, QK_DIM), lambda i, j: (i, 0, j, 0)),
            pl.BlockSpec((1, MLA_HEADS, n_kt, V_EXT, ATTN_TILE), lambda i, j: (i, 0, j, 0, 0)),
            row(MLA_WIDTH), row(D_MODEL),
        ],
        out_shape=[
            jax.ShapeDtypeStruct((b, MLA_HEADS, QK_DIM, s), BF16),
            jax.ShapeDtypeStruct((b, MLA_HEADS, s, QK_DIM), BF16),
            jax.ShapeDtypeStruct((b, MLA_HEADS, s // ATTN_TILE, V_EXT, ATTN_TILE), BF16),
            jax.ShapeDtypeStruct((b, s, MLA_WIDTH), F32),
            jax.ShapeDtypeStruct((b, s, D_MODEL), F32),
        ],
        compiler_params=pltpu.CompilerParams(
            dimension_semantics=("parallel", "parallel"), vmem_limit_bytes=VMEM_LIMIT),
        name="mla_pre",
    )(h, pos, inv_freq, w_fm, w_pm, w_uq_t, w_uk, w_uv_t, kbd, vbd, w_out)


def _sublane_allreduce(x, op):
    for shift in (4, 2, 1):
        x = op(x, pltpu.roll(x, shift, 0))
    return x


def _mla_attn_kernel(q_ref, k_ref, v_ref, sg_ref, hp_ref, w_out_ref, g_ref, o_ref,
                     acc_sc, tok_sc):
    i = pl.program_id(1)
    tq = q_ref.shape[3]
    tk = v_ref.shape[4]
    half = tk // 2
    heads = range(MLA_HEADS)

    def step(jt, carry, k0, kn, q0, qn, masked):
        m_old = carry
        ks = slice(jt * tk + k0, jt * tk + k0 + kn)
        qs = slice(q0, q0 + qn)
        s = [_dot(k_ref[0, hd, ks, :], q_ref[0, hd, :, qs]) for hd in heads]
        if masked:
            k_chunk = (k0 + jax.lax.broadcasted_iota(jnp.int32, (kn, qn), 0)) // CHUNK
            q_chunk = (q0 + jax.lax.broadcasted_iota(jnp.int32, (kn, qn), 1)) // CHUNK
            mask = k_chunk <= q_chunk
        m_new, alpha, p = [], [], []
        for hd in heads:
            sh = jnp.where(mask, s[hd], NEG_INF) if masked else s[hd]
            s3 = sh.reshape(kn // SUBLANES, SUBLANES, qn)
            m_prev = m_old[hd][:, qs]
            m = jnp.maximum(m_prev, _sublane_allreduce(jnp.max(s3, axis=0), jnp.maximum))
            alpha.append(jnp.exp2(m_prev - m))
            p.append(jnp.exp2(s3 - m[None]).reshape(kn, qn).astype(BF16))
            if qn != tq:
                m = jnp.concatenate(
                    ([m_old[hd][:, :q0]] if q0 else []) + [m]
                    + ([m_old[hd][:, q0 + qn:]] if q0 + qn < tq else []), axis=1)
            m_new.append(m)
        pv = [_dot(v_ref[0, hd, jt, :, k0:k0 + kn], p[hd]) for hd in heads]
        for hd in heads:
            acc = acc_sc[hd, :, qs].reshape(V_EXT // SUBLANES, SUBLANES, qn)
            acc_sc[hd, :, qs] = (alpha[hd][None] * acc).reshape(V_EXT, qn) + pv[hd]
        return tuple(m_new)

    def query_tile(n_full):
        acc_sc[...] = jnp.zeros(acc_sc.shape, F32)
        carry = tuple(jnp.full((SUBLANES, tq), NEG_INF, F32) for _ in heads)
        for jt in range(n_full):
            carry = step(jt, carry, 0, tk, 0, tq, False)
        carry = step(n_full, carry, 0, half, 0, tq, True)
        step(n_full, carry, half, tk - half, half, tq - half, True)

        for hd in heads:
            l_tot = acc_sc[hd, V_HEAD:V_HEAD + SUBLANES, :]
            o_t = acc_sc[hd, 0:V_HEAD, :].reshape(V_HEAD // SUBLANES, SUBLANES, tq) / l_tot[None]
            tok_sc[:, hd * V_HEAD:(hd + 1) * V_HEAD] = o_t.reshape(V_HEAD, tq).T

        n = tq // 2
        rows = [slice(c * n, (c + 1) * n) for c in range(tq // n)]
        branches = [(tok_sc[r, :] * sg_ref[0, r, :]).astype(BF16) for r in rows]
        outs = [hp_ref[0, r, :] + _dot(br, w_out_ref[0]) for r, br in zip(rows, branches)]
        for r, out in zip(rows, outs):
            o_ref[0, r, :] = _rms_norm(out, g_ref[...])

    for n_full in range(k_ref.shape[2] // tk):
        pl.when(i == n_full)(functools.partial(query_tile, n_full))


def _mla_attn(q_t, k, v_t, sg, hp, w_out, final_g):
    b, _, s, _ = k.shape
    t = ATTN_TILE
    const = lambda shape: pl.BlockSpec(shape, lambda i, j: (0,) * len(shape))
    row = lambda width: pl.BlockSpec((1, t, width), lambda i, j: (i, j, 0))
    return pl.pallas_call(
        _mla_attn_kernel,
        grid=(b, s // t),
        in_specs=[
            pl.BlockSpec((1, MLA_HEADS, QK_DIM, t), lambda i, j: (i, 0, 0, j)),
            pl.BlockSpec((1, MLA_HEADS, s, QK_DIM), lambda i, j: (i, 0, 0, 0)),
            pl.BlockSpec((1,) + v_t.shape[1:], lambda i, j: (i, 0, 0, 0, 0)),
            row(MLA_WIDTH), row(D_MODEL),
            pl.BlockSpec((1, MLA_WIDTH, D_MODEL), lambda i, j: (1, 0, 0)), const(final_g.shape),
        ],
        out_specs=row(D_MODEL),
        out_shape=jax.ShapeDtypeStruct((b, s, D_MODEL), F32),
        scratch_shapes=[
            pltpu.VMEM((MLA_HEADS, V_EXT, t), F32),
            pltpu.VMEM((t, MLA_WIDTH), F32),
        ],
        compiler_params=pltpu.CompilerParams(
            dimension_semantics=("parallel", "arbitrary"), vmem_limit_bytes=VMEM_LIMIT),
        name="mla_attn",
    )(q_t, k, v_t, sg, hp, w_out, final_g)


def kernel(x, mem, positions, norm_g, mem_norm_g, w_mem_kv, w_out, pool_w_in, pool_w_group,
           pool_scale, mla_w_in, mla_q_norm_g, mla_w_uq, mla_kv_norm_g, mla_w_ukv, final_norm_g):
    depth = w_mem_kv.shape[0]
    assert depth == 2 and pool_w_in.shape[0] == 1 and mla_w_in.shape[0] == 1
    bf = lambda a: a.astype(BF16)
    row = lambda a: a.reshape(1, -1)

    inv_freq = ROPE_THETA ** (-(jnp.arange(0, QK_ROPE, 2, dtype=F32) / QK_ROPE))
    kt0, v0, kbd1, vbd1 = _mem_kv(mem, row(mem_norm_g), bf(w_mem_kv))
    w_out_bf = bf(w_out)

    g0 = norm_g[0][:, None]
    mq_cols = (jnp.arange(POOL_IN) >= POOL_WIDTH) & (jnp.arange(POOL_IN) < POOL_WIDTH + MEM_WIDTH)
    pool_w = bf(pool_w_in[0] * g0 * jnp.where(mq_cols, MEM_Q_SCALE, 1.0))
    h1 = _pool_layer(x, _pool_fold(pool_w_in, g0, pool_w_group), pool_w, row(pool_scale[0]),
                     kt0, v0, w_out_bf)

    w_pm, w_fm = _mla_prep(mla_w_in[0].T, row(norm_g[1]))
    w_ukv = (mla_w_ukv[0] * mla_kv_norm_g[0][:, None]).reshape(
        KV_LORA, MLA_HEADS, QK_NOPE + V_HEAD)
    w_uk = bf(w_ukv[:, :, :QK_NOPE].reshape(KV_LORA, MLA_HEADS * QK_NOPE))
    w_uv_t = bf(w_ukv[:, :, QK_NOPE:].reshape(KV_LORA, MLA_WIDTH).T)
    w_uq_t = bf((mla_w_uq[0] * (mla_q_norm_g[0][:, None] * Q_SCALE)).T)
    q_t, k, v_t, sg, hp = _mla_pre(
        h1, positions[:, None, :], inv_freq[:, None], w_fm, w_pm, w_uq_t, w_uk, w_uv_t,
        kbd1, vbd1, w_out_bf)

    return _mla_attn(q_t, k, v_t, sg, hp, w_out_bf, row(final_norm_g))
```

```python
import functools
import math

import jax
import jax.numpy as jnp
from jax.experimental import pallas as pl
from jax.experimental.pallas import tpu as pltpu

D_MODEL = 1024
CHUNK = 64
EPS = 1e-6
NEG_INF = -1e30

N_MEM = 256
MEM_HEADS = 4
MEM_HEAD_DIM = 64
MEM_WIDTH = MEM_HEADS * MEM_HEAD_DIM
MEM_BATCH = 4

POOL_WINDOWS = (2, 4, 8, 16)
POOL_GROUP_DIM = 192
POOL_WIDTH = len(POOL_WINDOWS) * POOL_GROUP_DIM
POOL_HALO = 16

MLA_HEADS = 6
QK_NOPE = 128
QK_ROPE = 64
QK_DIM = QK_NOPE + QK_ROPE
V_HEAD = 128
V_EXT = V_HEAD + 16
Q_LORA = 384
KV_LORA = 256
MLA_WIDTH = MLA_HEADS * V_HEAD
ROPE_THETA = 10000.0
LOG2_E = math.log2(math.e)
Q_SCALE = QK_DIM ** -0.5 * LOG2_E
MEM_Q_SCALE = MEM_HEAD_DIM ** -0.5 * LOG2_E

MIX_WIDTH = POOL_WIDTH + MEM_WIDTH
POOL_IN = POOL_WIDTH + MEM_WIDTH + MIX_WIDTH

LANES = 128
SUBLANES = 8
SEQ_TILE = 1024
ROW_CHUNK = 256
ATTN_TILE = 512
VMEM_LIMIT = 56 * 1024 * 1024

BF16 = jnp.bfloat16
F32 = jnp.float32


def _rms_scale(x):
    ms = jnp.mean(x * x, axis=-1, keepdims=True)
    return x * jax.lax.rsqrt(ms + EPS)


def _rms_norm(x, g):
    return _rms_scale(x) * g


def _silu(x):
    return x / (1.0 + jnp.exp2(x * -LOG2_E))


def _dot(a, b):
    return jnp.dot(a, b, preferred_element_type=F32)


def _dot_nt(a, b):
    return jax.lax.dot_general(a, b, (((1,), (1,)), ((), ())), preferred_element_type=F32)


def _softmax_rows(s):
    e = jnp.exp2(s - jnp.max(s, axis=-1, keepdims=True))
    return (e / jnp.sum(e, axis=-1, keepdims=True)).astype(BF16)


def _memory_attention_heads(mq, k_t, v):
    d = MEM_HEAD_DIM
    outs = []
    for h in range(MEM_HEADS):
        p = _softmax_rows(_dot(mq[:, h * d:(h + 1) * d].astype(BF16), k_t[h]))
        outs.append(_dot(p, v[h]))
    return jnp.concatenate(outs, axis=-1)


def _memory_attention(mq, kbd, v):
    s = _dot(mq.astype(BF16), kbd)
    outs = [_dot(_softmax_rows(s[:, h * N_MEM:(h + 1) * N_MEM]), v[h])
            for h in range(MEM_HEADS)]
    return jnp.concatenate(outs, axis=-1)


def _mem_kv_kernel(mem_ref, g_ref, w_ref, kt0_ref, v0_ref, kbd1_ref, v1_ref):
    rows = range(mem_ref.shape[0])
    mem_ns = [_rms_norm(mem_ref[r], g_ref[...]).astype(BF16) for r in rows]
    d = MEM_HEAD_DIM
    row_head = jax.lax.broadcasted_iota(jnp.int32, (MEM_WIDTH, N_MEM), 0) // d
    col_head = jax.lax.broadcasted_iota(jnp.int32, (N_MEM, MEM_WIDTH), 1) // d
    for layer in range(w_ref.shape[0]):
        kvs = [_dot(mem_n, w_ref[layer]) for mem_n in mem_ns]
        for r in rows:
            k_t = kvs[r][:, :MEM_WIDTH].T
            v = kvs[r][:, MEM_WIDTH:]
            for h in range(MEM_HEADS):
                if layer == 0:
                    kt0_ref[r, h] = k_t[h * d:(h + 1) * d, :].astype(BF16)
                    v0_ref[r, h] = v[:, h * d:(h + 1) * d].astype(BF16)
                else:
                    kbd1_ref[r, :, h * N_MEM:(h + 1) * N_MEM] = jnp.where(
                        row_head == h, k_t, 0.0).astype(BF16)
                    v1_ref[r, h] = v[:, h * d:(h + 1) * d].astype(BF16)


def _mem_kv(mem, mem_norm_g, w):
    b = mem.shape[0]
    nb = MEM_BATCH
    shapes = [(MEM_HEADS, MEM_HEAD_DIM, N_MEM), (MEM_HEADS, N_MEM, MEM_HEAD_DIM),
              (MEM_WIDTH, MEM_HEADS * N_MEM), (MEM_HEADS, N_MEM, MEM_HEAD_DIM)]
    return pl.pallas_call(
        _mem_kv_kernel,
        grid=(b // nb,),
        in_specs=[
            pl.BlockSpec((nb, N_MEM, D_MODEL), lambda i: (i, 0, 0)),
            pl.BlockSpec((1, D_MODEL), lambda i: (0, 0)),
            pl.BlockSpec(w.shape, lambda i: (0, 0, 0)),
        ],
        out_specs=[pl.BlockSpec((nb,) + s, lambda i, n=len(s): (i,) + (0,) * n) for s in shapes],
        out_shape=[jax.ShapeDtypeStruct((b,) + s, BF16) for s in shapes],
        compiler_params=pltpu.CompilerParams(
            dimension_semantics=("parallel",), vmem_limit_bytes=VMEM_LIMIT),
        name="mem_kv",
    )(mem, mem_norm_g, w)


def _split_bf16(a):
    hi = a.astype(BF16)
    return hi, (a - hi.astype(F32)).astype(BF16)


def _pool_fold_kernel(w_u_ref, g_ref, w_grp_ref, o_ref):
    grp = [_split_bf16(w_grp_ref[0, gi]) for gi in range(len(POOL_WINDOWS))]
    n = ROW_CHUNK
    for c in range(w_u_ref.shape[1] // n):
        a = w_u_ref[0, c * n:(c + 1) * n, :] * g_ref[c * n:(c + 1) * n, :]
        cols = []
        for gi, (b_hi, b_lo) in enumerate(grp):
            a_hi, a_lo = _split_bf16(a[:, gi * POOL_GROUP_DIM:(gi + 1) * POOL_GROUP_DIM])
            cols.append(_dot(a_hi, b_hi) + _dot(a_hi, b_lo) + _dot(a_lo, b_hi))
        o_ref[c * n:(c + 1) * n, :] = jnp.concatenate(cols, axis=-1).astype(BF16)


def _pool_fold(pool_w_in, g_col, pool_w_group):
    return pl.pallas_call(
        _pool_fold_kernel,
        grid=(1,),
        in_specs=[
            pl.BlockSpec((1, D_MODEL, POOL_WIDTH), lambda i: (0, 0, 0)),
            pl.BlockSpec(g_col.shape, lambda i: (0, 0)),
            pl.BlockSpec(pool_w_group.shape, lambda i: (0, 0, 0, 0)),
        ],
        out_specs=pl.BlockSpec((D_MODEL, POOL_WIDTH), lambda i: (0, 0)),
        out_shape=jax.ShapeDtypeStruct((D_MODEL, POOL_WIDTH), BF16),
        compiler_params=pltpu.CompilerParams(vmem_limit_bytes=VMEM_LIMIT),
        name="pool_fold",
    )(pool_w_in, g_col, pool_w_group)


def _window_sums(a):
    s2 = a + pltpu.roll(a, 1, 0)
    s4 = s2 + pltpu.roll(s2, 2, 0)
    s8 = s4 + pltpu.roll(s4, 4, 0)
    s16 = s8 + pltpu.roll(s8, 8, 0)
    return {2: s2, 4: s4, 8: s8, 16: s16}


def _pooled_mix(ubuf, row0, n, pos0):
    pos = pos0 + jax.lax.broadcasted_iota(jnp.int32, (n, 1), 0)
    inv_cnt = {w: 1.0 / jnp.minimum(pos + 1, w).astype(F32) for w in POOL_WINDOWS}
    lane = jax.lax.broadcasted_iota(jnp.int32, (n, LANES), 1)
    mixed = []
    for c in range(POOL_WIDTH // LANES):
        cols = slice(c * LANES, (c + 1) * LANES)
        sums = _window_sums(ubuf[row0:row0 + POOL_HALO + n, cols])
        lo_grp = (c * LANES) // POOL_GROUP_DIM
        hi_grp = ((c + 1) * LANES - 1) // POOL_GROUP_DIM
        w_lo, w_hi = POOL_WINDOWS[lo_grp], POOL_WINDOWS[hi_grp]
        pooled = sums[w_lo][POOL_HALO:] * inv_cnt[w_lo]
        if hi_grp != lo_grp:
            split = hi_grp * POOL_GROUP_DIM - c * LANES
            pooled = jnp.where(lane < split, pooled, sums[w_hi][POOL_HALO:] * inv_cnt[w_hi])
        mixed.append(pooled - ubuf[POOL_HALO + row0:POOL_HALO + row0 + n, cols])
    return jnp.concatenate(mixed, axis=-1)


def _pool_layer_kernel(x_ref, w_u_ref, w_mq_ref, w_gate_ref, scale_ref,
                       kt_ref, v_ref, w_out_ref, o_ref, ubuf):
    j = pl.program_id(1)
    t = x_ref.shape[1]
    n = ROW_CHUNK
    chunks = range(t // n)

    @pl.when(j == 0)
    def _():
        ubuf[0:POOL_HALO, :] = jnp.zeros((POOL_HALO, POOL_WIDTH), F32)

    @pl.when(j > 0)
    def _():
        ubuf[0:POOL_HALO, :] = ubuf[t:t + POOL_HALO, :]

    xs = [x_ref[0, c * n:(c + 1) * n, :] for c in chunks]
    hns = [_rms_scale(x).astype(BF16) for x in xs]
    for c in chunks:
        ubuf[POOL_HALO + c * n:POOL_HALO + (c + 1) * n, :] = _dot(hns[c], w_u_ref[...])
    mqs = [_dot(hn, w_mq_ref[...]) for hn in hns]
    gates = [_silu(_dot(hn, w_gate_ref[...])) for hn in hns]
    toks = [_pooled_mix(ubuf, c * n, n, j * t + c * n) * scale_ref[...] for c in chunks]
    mem_os = [_memory_attention_heads(mq, kt_ref.at[0], v_ref.at[0]) for mq in mqs]
    for c in chunks:
        y = _dot((toks[c] * gates[c][:, :POOL_WIDTH]).astype(BF16), w_out_ref[0, :POOL_WIDTH, :])
        y += _dot((mem_os[c] * gates[c][:, POOL_WIDTH:]).astype(BF16),
                  w_out_ref[0, POOL_WIDTH:, :])
        o_ref[0, c * n:(c + 1) * n, :] = xs[c] + y


def _pool_layer(x, w_u, w_in, scale, k_t, v, w_out):
    b, s, _ = x.shape
    t = SEQ_TILE
    const = lambda shape: pl.BlockSpec(shape, lambda i, j: (0,) * len(shape))
    return pl.pallas_call(
        _pool_layer_kernel,
        grid=(b, s // t),
        in_specs=[
            pl.BlockSpec((1, t, D_MODEL), lambda i, j: (i, j, 0)),
            const(w_u.shape),
            pl.BlockSpec((D_MODEL, MEM_WIDTH), lambda i, j: (0, POOL_WIDTH // MEM_WIDTH)),
            pl.BlockSpec((D_MODEL, MIX_WIDTH), lambda i, j: (0, 1)),
            const(scale.shape),
            pl.BlockSpec((1,) + k_t.shape[1:], lambda i, j: (i, 0, 0, 0)),
            pl.BlockSpec((1,) + v.shape[1:], lambda i, j: (i, 0, 0, 0)),
            pl.BlockSpec((1,) + w_out.shape[1:], lambda i, j: (0, 0, 0)),
        ],
        out_specs=pl.BlockSpec((1, t, D_MODEL), lambda i, j: (i, j, 0)),
        out_shape=jax.ShapeDtypeStruct(x.shape, F32),
        scratch_shapes=[pltpu.VMEM((POOL_HALO + t, POOL_WIDTH), F32)],
        compiler_params=pltpu.CompilerParams(
            dimension_semantics=("parallel", "arbitrary"), vmem_limit_bytes=VMEM_LIMIT),
        name="pool_layer",
    )(x, w_u, w_in, w_in, scale, k_t, v, w_out)


def _mla_prep_kernel(w_t_ref, g_ref, w_pm_ref, w_fm_ref):
    o_ckv = Q_LORA
    o_kr = o_ckv + KV_LORA
    o_mq = o_kr + QK_ROPE
    o_gate = o_mq + MEM_WIDTH
    g = g_ref[...]
    w_fm_ref[0:Q_LORA, :] = (w_t_ref[0:o_ckv, :] * g).astype(BF16)
    w_fm_ref[Q_LORA:, :] = (w_t_ref[o_kr:o_mq, :] * g).astype(BF16)
    blocks = ([(o_gate + r, 1.0) for r in range(0, MIX_WIDTH, LANES)]
              + [(o_ckv + r, 1.0) for r in range(0, KV_LORA, LANES)]
              + [(o_mq + r, MEM_Q_SCALE) for r in range(0, MEM_WIDTH, LANES)])
    for dst, (src, scale) in enumerate(blocks):
        blk = w_t_ref[src:src + LANES, :] * (g * scale)
        w_pm_ref[:, dst * LANES:(dst + 1) * LANES] = blk.T.astype(BF16)


def _mla_prep(w_t, g_row):
    n_in, d = w_t.shape
    n_pm = MIX_WIDTH + KV_LORA + MEM_WIDTH
    n_fm = Q_LORA + QK_ROPE
    return pl.pallas_call(
        _mla_prep_kernel,
        grid=(1,),
        in_specs=[pl.BlockSpec((n_in, d), lambda i: (0, 0)),
                  pl.BlockSpec(g_row.shape, lambda i: (0, 0))],
        out_specs=[pl.BlockSpec((d, n_pm), lambda i: (0, 0)),
                   pl.BlockSpec((n_fm, d), lambda i: (0, 0))],
        out_shape=[jax.ShapeDtypeStruct((d, n_pm), BF16), jax.ShapeDtypeStruct((n_fm, d), BF16)],
        compiler_params=pltpu.CompilerParams(vmem_limit_bytes=VMEM_LIMIT),
        name="mla_prep",
    )(w_t, g_row)


def _rope_t(x, cos_t, sin_t):
    half = QK_ROPE // 2
    x1, x2 = x[:half], x[half:]
    return x1 * cos_t - x2 * sin_t, x2 * cos_t + x1 * sin_t


def _mla_pre_kernel(h_ref, pos_ref, inv_freq_ref, w_fm_ref, w_pm_ref, w_uq_t_ref, w_uk_ref,
                    w_uv_t_ref, kbd_ref, vbd_ref, w_out_ref,
                    q_ref, k_ref, v_ref, sg_ref, hp_ref):
    w_cq_t_ref, w_kr_t_ref = w_fm_ref.at[0:Q_LORA], w_fm_ref.at[Q_LORA:Q_LORA + QK_ROPE]
    w_gate_ref = w_pm_ref.at[:, 0:MIX_WIDTH]
    w_ckv_ref = w_pm_ref.at[:, MIX_WIDTH:MIX_WIDTH + KV_LORA]
    w_mq_ref = w_pm_ref.at[:, MIX_WIDTH + KV_LORA:MIX_WIDTH + KV_LORA + MEM_WIDTH]
    w_out_mem_ref = w_out_ref.at[0]
    t = h_ref.shape[1]
    n = 2 * ROW_CHUNK
    chunks = range(t // n)
    heads = range(MLA_HEADS)
    half = QK_ROPE // 2
    rows = [slice(c * n, (c + 1) * n) for c in chunks]

    hs = [h_ref[0, rows[c], :] for c in chunks]
    xns = [_rms_scale(h) for h in hs]
    hns = [xn.astype(BF16) for xn in xns]
    mqs = [_dot(hn, w_mq_ref[...]) for hn in hns]
    mem_os = [_memory_attention(mq, kbd_ref[0], vbd_ref.at[0]) for mq in mqs]
    gates = [_silu(_dot(hn, w_gate_ref[...])) for hn in hns]
    for c in chunks:
        sg_ref[0, rows[c], :] = gates[c][:, :MLA_WIDTH]
        hp_ref[0, rows[c], :] = hs[c] + _dot(
            (mem_os[c] * gates[c][:, MLA_WIDTH:]).astype(BF16), w_out_mem_ref[...])

    hn_ts = [xn.T.astype(BF16) for xn in xns]
    cq_ts = [_dot(w_cq_t_ref[...], hn_t) for hn_t in hn_ts]
    kr_ts = [_dot(w_kr_t_ref[...], hn_t) for hn_t in hn_ts]
    ckvns = [_rms_scale(_dot(hn, w_ckv_ref[...])).astype(BF16) for hn in hns]
    angs = [pos_ref[0, :, rows[c]].astype(F32) * inv_freq_ref[...] for c in chunks]
    cos_ts, sin_ts = [jnp.cos(a) for a in angs], [jnp.sin(a) for a in angs]
    cqn_ts = [(cq * jax.lax.rsqrt(jnp.mean(cq * cq, axis=0, keepdims=True) + EPS)
               ).astype(BF16) for cq in cq_ts]

    q_ts = [_dot(w_uq_t_ref[...], cqn_t) for cqn_t in cqn_ts]
    for c in chunks:
        for hd in heads:
            base = hd * QK_DIM
            r1, r2 = _rope_t(q_ts[c][base + QK_NOPE:base + QK_DIM], cos_ts[c], sin_ts[c])
            q_ref[0, hd, 0:QK_NOPE, rows[c]] = q_ts[c][base:base + QK_NOPE].astype(BF16)
            q_ref[0, hd, QK_NOPE:QK_NOPE + half, rows[c]] = r1.astype(BF16)
            q_ref[0, hd, QK_NOPE + half:QK_DIM, rows[c]] = r2.astype(BF16)

    k_nopes = [_dot(ckvn, w_uk_ref[...]) for ckvn in ckvns]
    v_ts = [_dot_nt(w_uv_t_ref[...], ckvn) for ckvn in ckvns]
    for c in chunks:
        r1, r2 = _rope_t(kr_ts[c], cos_ts[c], sin_ts[c])
        k_rope = jnp.concatenate([r1, r2, jnp.zeros((LANES - QK_ROPE, n), F32)], axis=0).T
        k_rope = k_rope[:, :QK_ROPE]
        kt, off = divmod(c * n, ATTN_TILE)
        for hd in heads:
            k_ref[0, hd, rows[c], :] = jnp.concatenate(
                [k_nopes[c][:, hd * QK_NOPE:(hd + 1) * QK_NOPE], k_rope], axis=-1).astype(BF16)
            v_ref[0, hd, kt, 0:V_HEAD, off:off + n] = (
                v_ts[c][hd * V_HEAD:(hd + 1) * V_HEAD].astype(BF16))
            v_ref[0, hd, kt, V_HEAD:V_EXT, off:off + n] = jnp.ones((V_EXT - V_HEAD, n), BF16)


def _mla_pre(h, pos, inv_freq, w_fm, w_pm, w_uq_t, w_uk, w_uv_t, kbd, vbd, w_out):
    b, s, _ = h.shape
    t = SEQ_TILE
    n_kt = t // ATTN_TILE
    const = lambda shape: pl.BlockSpec(shape, lambda i, j: (0,) * len(shape))
    row = lambda width: pl.BlockSpec((1, t, width), lambda i, j: (i, j, 0))
    return pl.pallas_call(
        _mla_pre_kernel,
        grid=(b, s // t),
        in_specs=[
            row(D_MODEL),
            pl.BlockSpec((1, 1, t), lambda i, j: (i, 0, j)), const(inv_freq.shape),
            const(w_fm.shape), const(w_pm.shape),
            const(w_uq_t.shape), const(w_uk.shape), const(w_uv_t.shape),
            pl.BlockSpec((1,) + kbd.shape[1:], lambda i, j: (i, 0, 0)),
            pl.BlockSpec((1,) + vbd.shape[1:], lambda i, j: (i, 0, 0, 0)),
            pl.BlockSpec((1, MEM_WIDTH, D_MODEL), lambda i, j: (1, MLA_WIDTH // MEM_WIDTH, 0)),
        ],
        out_specs=[
            pl.BlockSpec((1, MLA_HEADS, QK_DIM, t), lambda i, j: (i, 0, 0, j)),
            pl.BlockSpec((1, MLA_HEADS, t, QK_DIM), lambda i, j: (i, 0, j, 0)),
            pl.BlockSpec((1, MLA_HEADS, n_kt, V_EXT, ATTN_TILE), lambda i, j: (i, 0, j, 0, 0)),
            row(MLA_WIDTH), row(D_MODEL),
        ],
        out_shape=[
            jax.ShapeDtypeStruct((b, MLA_HEADS, QK_DIM, s), BF16),
            jax.ShapeDtypeStruct((b, MLA_HEADS, s, QK_DIM), BF16),
            jax.ShapeDtypeStruct((b, MLA_HEADS, s // ATTN_TILE, V_EXT, ATTN_TILE), BF16),
            jax.ShapeDtypeStruct((b, s, MLA_WIDTH), F32),
            jax.ShapeDtypeStruct((b, s, D_MODEL), F32),
        ],
        compiler_params=pltpu.CompilerParams(
            dimension_semantics=("parallel", "parallel"), vmem_limit_bytes=VMEM_LIMIT),
        name="mla_pre",
    )(h, pos, inv_freq, w_fm, w_pm, w_uq_t, w_uk, w_uv_t, kbd, vbd, w_out)


def _sublane_allreduce(x, op):
    for shift in (4, 2, 1):
        x = op(x, pltpu.roll(x, shift, 0))
    return x


def _mla_attn_kernel(q_ref, k_ref, v_ref, sg_ref, hp_ref, w_out_ref, g_ref, o_ref,
                     acc_sc, tok_sc):
    i = pl.program_id(1)
    tq = q_ref.shape[3]
    tk = v_ref.shape[4]
    half = tk // 2
    heads = range(MLA_HEADS)

    def step(jt, carry, k0, kn, q0, qn, masked):
        m_old = carry
        ks = slice(jt * tk + k0, jt * tk + k0 + kn)
        qs = slice(q0, q0 + qn)
        s = [_dot(k_ref[0, hd, ks, :], q_ref[0, hd, :, qs]) for hd in heads]
        if masked:
            k_chunk = (k0 + jax.lax.broadcasted_iota(jnp.int32, (kn, qn), 0)) // CHUNK
            q_chunk = (q0 + jax.lax.broadcasted_iota(jnp.int32, (kn, qn), 1)) // CHUNK
            mask = k_chunk <= q_chunk
        m_new, alpha, p = [], [], []
        for hd in heads:
            sh = jnp.where(mask, s[hd], NEG_INF) if masked else s[hd]
            s3 = sh.reshape(kn // SUBLANES, SUBLANES, qn)
            m_prev = m_old[hd][:, qs]
            m = jnp.maximum(m_prev, _sublane_allreduce(jnp.max(s3, axis=0), jnp.maximum))
            alpha.append(jnp.exp2(m_prev - m))
            p.append(jnp.exp2(s3 - m[None]).reshape(kn, qn).astype(BF16))
            if qn != tq:
                m = jnp.concatenate(
                    ([m_old[hd][:, :q0]] if q0 else []) + [m]
                    + ([m_old[hd][:, q0 + qn:]] if q0 + qn < tq else []), axis=1)
            m_new.append(m)
        pv = [_dot(v_ref[0, hd, jt, :, k0:k0 + kn], p[hd]) for hd in heads]
        for hd in heads:
            acc = acc_sc[hd, :, qs].reshape(V_EXT // SUBLANES, SUBLANES, qn)
            acc_sc[hd, :, qs] = (alpha[hd][None] * acc).reshape(V_EXT, qn) + pv[hd]
        return tuple(m_new)

    def query_tile(n_full):
        acc_sc[...] = jnp.zeros(acc_sc.shape, F32)
        carry = tuple(jnp.full((SUBLANES, tq), NEG_INF, F32) for _ in heads)
        for jt in range(n_full):
            carry = step(jt, carry, 0, tk, 0, tq, False)
        carry = step(n_full, carry, 0, half, 0, tq, True)
        step(n_full, carry, half, tk - half, half, tq - half, True)

        for hd in heads:
            l_tot = acc_sc[hd, V_HEAD:V_HEAD + SUBLANES, :]
            o_t = acc_sc[hd, 0:V_HEAD, :].reshape(V_HEAD // SUBLANES, SUBLANES, tq) / l_tot[None]
            tok_sc[:, hd * V_HEAD:(hd + 1) * V_HEAD] = o_t.reshape(V_HEAD, tq).T

        n = tq // 2
        rows = [slice(c * n, (c + 1) * n) for c in range(tq // n)]
        branches = [(tok_sc[r, :] * sg_ref[0, r, :]).astype(BF16) for r in rows]
        outs = [hp_ref[0, r, :] + _dot(br, w_out_ref[0]) for r, br in zip(rows, branches)]
        for r, out in zip(rows, outs):
            o_ref[0, r, :] = _rms_norm(out, g_ref[...])

    for n_full in range(k_ref.shape[2] // tk):
        pl.when(i == n_full)(functools.partial(query_tile, n_full))


def _mla_attn(q_t, k, v_t, sg, hp, w_out, final_g):
    b, _, s, _ = k.shape
    t = ATTN_TILE
    const = lambda shape: pl.BlockSpec(shape, lambda i, j: (0,) * len(shape))
    row = lambda width: pl.BlockSpec((1, t, width), lambda i, j: (i, j, 0))
    return pl.pallas_call(
        _mla_attn_kernel,
        grid=(b, s // t),
        in_specs=[
            pl.BlockSpec((1, MLA_HEADS, QK_DIM, t), lambda i, j: (i, 0, 0, j)),
            pl.BlockSpec((1, MLA_HEADS, s, QK_DIM), lambda i, j: (i, 0, 0, 0)),
            pl.BlockSpec((1,) + v_t.shape[1:], lambda i, j: (i, 0, 0, 0, 0)),
            row(MLA_WIDTH), row(D_MODEL),
            pl.BlockSpec((1, MLA_WIDTH, D_MODEL), lambda i, j: (1, 0, 0)), const(final_g.shape),
        ],
        out_specs=row(D_MODEL),
        out_shape=jax.ShapeDtypeStruct((b, s, D_MODEL), F32),
        scratch_shapes=[
            pltpu.VMEM((MLA_HEADS, V_EXT, t), F32),
            pltpu.VMEM((t, MLA_WIDTH), F32),
        ],
        compiler_params=pltpu.CompilerParams(
            dimension_semantics=("parallel", "arbitrary"), vmem_limit_bytes=VMEM_LIMIT),
        name="mla_attn",
    )(q_t, k, v_t, sg, hp, w_out, final_g)


def kernel(x, mem, positions, norm_g, mem_norm_g, w_mem_kv, w_out, pool_w_in, pool_w_group,
           pool_scale, mla_w_in, mla_q_norm_g, mla_w_uq, mla_kv_norm_g, mla_w_ukv, final_norm_g):
    depth = w_mem_kv.shape[0]
    assert depth == 2 and pool_w_in.shape[0] == 1 and mla_w_in.shape[0] == 1
    bf = lambda a: a.astype(BF16)
    row = lambda a: a.reshape(1, -1)

    inv_freq = ROPE_THETA ** (-(jnp.arange(0, QK_ROPE, 2, dtype=F32) / QK_ROPE))
    kt0, v0, kbd1, vbd1 = _mem_kv(mem, row(mem_norm_g), bf(w_mem_kv))
    w_out_bf = bf(w_out)

    g0 = norm_g[0][:, None]
    mq_cols = (jnp.arange(POOL_IN) >= POOL_WIDTH) & (jnp.arange(POOL_IN) < POOL_WIDTH + MEM_WIDTH)
    pool_w = bf(pool_w_in[0] * g0 * jnp.where(mq_cols, MEM_Q_SCALE, 1.0))
    h1 = _pool_layer(x, _pool_fold(pool_w_in, g0, pool_w_group), pool_w, row(pool_scale[0]),
                     kt0, v0, w_out_bf)

    w_pm, w_fm = _mla_prep(mla_w_in[0].T, row(norm_g[1]))
    w_ukv = (mla_w_ukv[0] * mla_kv_norm_g[0][:, None]).reshape(
        KV_LORA, MLA_HEADS, QK_NOPE + V_HEAD)
    w_uk = bf(w_ukv[:, :, :QK_NOPE].reshape(KV_LORA, MLA_HEADS * QK_NOPE))
    w_uv_t = bf(w_ukv[:, :, QK_NOPE:].reshape(KV_LORA, MLA_WIDTH).T)
    w_uq_t = bf((mla_w_uq[0] * (mla_q_norm_g[0][:, None] * Q_SCALE)).T)
    q_t, k, v_t, sg, hp = _mla_pre(
        h1, positions[:, None, :], inv_freq[:, None], w_fm, w_pm, w_uq_t, w_uk, w_uv_t,
        kbd1, vbd1, w_out_bf)

    return _mla_attn(q_t, k, v_t, sg, hp, w_out_bf, row(final_norm_g))
```

```python
import functools
import math

import jax
import jax.numpy as jnp
from jax.experimental import pallas as pl
from jax.experimental.pallas import tpu as pltpu

D_MODEL = 1024
CHUNK = 64
EPS = 1e-6
NEG_INF = -1e30

N_MEM = 256
MEM_HEADS = 4
MEM_HEAD_DIM = 64
MEM_WIDTH = MEM_HEADS * MEM_HEAD_DIM
MEM_BATCH = 8

POOL_WINDOWS = (2, 4, 8, 16)
POOL_GROUP_DIM = 192
POOL_WIDTH = len(POOL_WINDOWS) * POOL_GROUP_DIM
POOL_HALO = 16

MLA_HEADS = 6
QK_NOPE = 128
QK_ROPE = 64
QK_DIM = QK_NOPE + QK_ROPE
V_HEAD = 128
V_EXT = V_HEAD + 16
Q_LORA = 384
KV_LORA = 256
MLA_WIDTH = MLA_HEADS * V_HEAD
ROPE_THETA = 10000.0
LOG2_E = math.log2(math.e)
Q_SCALE = QK_DIM ** -0.5 * LOG2_E
MEM_Q_SCALE = MEM_HEAD_DIM ** -0.5 * LOG2_E

MIX_WIDTH = POOL_WIDTH + MEM_WIDTH
POOL_IN = POOL_WIDTH + MEM_WIDTH + MIX_WIDTH

LANES = 128
SUBLANES = 8
SEQ_TILE = 1024
ROW_CHUNK = 256
ATTN_TILE = 512
VMEM_LIMIT = 56 * 1024 * 1024

BF16 = jnp.bfloat16
F32 = jnp.float32


def _rms_scale(x):
    ms = jnp.mean(x * x, axis=-1, keepdims=True)
    return x * jax.lax.rsqrt(ms + EPS)


def _rms_norm(x, g):
    return _rms_scale(x) * g


def _silu(x):
    return x / (1.0 + jnp.exp2(x * -LOG2_E))


def _dot(a, b):
    return jnp.dot(a, b, preferred_element_type=F32)


def _dot_nt(a, b):
    return jax.lax.dot_general(a, b, (((1,), (1,)), ((), ())), preferred_element_type=F32)


def _softmax_rows(s):
    e = jnp.exp2(s - jnp.max(s, axis=-1, keepdims=True))
    return (e / jnp.sum(e, axis=-1, keepdims=True)).astype(BF16)


def _memory_attention_heads(mq, k_t, v):
    d = MEM_HEAD_DIM
    outs = []
    for h in range(MEM_HEADS):
        p = _softmax_rows(_dot(mq[:, h * d:(h + 1) * d].astype(BF16), k_t[h]))
        outs.append(_dot(p, v[h]))
    return jnp.concatenate(outs, axis=-1)


def _memory_attention(mq, kbd, v):
    s = _dot(mq.astype(BF16), kbd)
    outs = [_dot(_softmax_rows(s[:, h * N_MEM:(h + 1) * N_MEM]), v[h])
            for h in range(MEM_HEADS)]
    return jnp.concatenate(outs, axis=-1)


def _mem_kv_kernel(mem_ref, g_ref, w_ref, kt0_ref, v0_ref, kbd1_ref, v1_ref):
    rows = range(mem_ref.shape[0])
    mem_ns = [_rms_norm(mem_ref[r], g_ref[...]).astype(BF16) for r in rows]
    d = MEM_HEAD_DIM
    row_head = jax.lax.broadcasted_iota(jnp.int32, (MEM_WIDTH, N_MEM), 0) // d
    for layer in range(w_ref.shape[0]):
        kvs = [_dot(mem_n, w_ref[layer]) for mem_n in mem_ns]
        for r in rows:
            k_t = kvs[r][:, :MEM_WIDTH].T
            v = kvs[r][:, MEM_WIDTH:]
            for h in range(MEM_HEADS):
                if layer == 0:
                    kt0_ref[r, h] = k_t[h * d:(h + 1) * d, :].astype(BF16)
                    v0_ref[r, h] = v[:, h * d:(h + 1) * d].astype(BF16)
                else:
                    kbd1_ref[r, :, h * N_MEM:(h + 1) * N_MEM] = jnp.where(
                        row_head == h, k_t, 0.0).astype(BF16)
                    v1_ref[r, h] = v[:, h * d:(h + 1) * d].astype(BF16)


def _mem_kv(mem, mem_norm_g, w):
    b = mem.shape[0]
    nb = MEM_BATCH
    shapes = [(MEM_HEADS, MEM_HEAD_DIM, N_MEM), (MEM_HEADS, N_MEM, MEM_HEAD_DIM),
              (MEM_WIDTH, MEM_HEADS * N_MEM), (MEM_HEADS, N_MEM, MEM_HEAD_DIM)]
    return pl.pallas_call(
        _mem_kv_kernel,
        grid=(b // nb,),
        in_specs=[
            pl.BlockSpec((nb, N_MEM, D_MODEL), lambda i: (i, 0, 0)),
            pl.BlockSpec((1, D_MODEL), lambda i: (0, 0)),
            pl.BlockSpec(w.shape, lambda i: (0, 0, 0)),
        ],
        out_specs=[pl.BlockSpec((nb,) + s, lambda i, n=len(s): (i,) + (0,) * n) for s in shapes],
        out_shape=[jax.ShapeDtypeStruct((b,) + s, BF16) for s in shapes],
        compiler_params=pltpu.CompilerParams(
            dimension_semantics=("parallel",), vmem_limit_bytes=VMEM_LIMIT),
        name="mem_kv",
    )(mem, mem_norm_g, w)


def _split_bf16(a):
    hi = a.astype(BF16)
    return hi, (a - hi.astype(F32)).astype(BF16)


def _pool_fold_kernel(w_u_ref, g_ref, w_grp_ref, o_ref):
    grp = [_split_bf16(w_grp_ref[0, gi]) for gi in range(len(POOL_WINDOWS))]
    n = ROW_CHUNK
    for c in range(w_u_ref.shape[1] // n):
        a = w_u_ref[0, c * n:(c + 1) * n, :] * g_ref[c * n:(c + 1) * n, :]
        cols = []
        for gi, (b_hi, b_lo) in enumerate(grp):
            a_hi, a_lo = _split_bf16(a[:, gi * POOL_GROUP_DIM:(gi + 1) * POOL_GROUP_DIM])
            cols.append(_dot(a_hi, b_hi) + _dot(a_hi, b_lo) + _dot(a_lo, b_hi))
        o_ref[c * n:(c + 1) * n, :] = jnp.concatenate(cols, axis=-1).astype(BF16)


def _pool_fold(pool_w_in, g_col, pool_w_group):
    return pl.pallas_call(
        _pool_fold_kernel,
        grid=(1,),
        in_specs=[
            pl.BlockSpec((1, D_MODEL, POOL_WIDTH), lambda i: (0, 0, 0)),
            pl.BlockSpec(g_col.shape, lambda i: (0, 0)),
            pl.BlockSpec(pool_w_group.shape, lambda i: (0, 0, 0, 0)),
        ],
        out_specs=pl.BlockSpec((D_MODEL, POOL_WIDTH), lambda i: (0, 0)),
        out_shape=jax.ShapeDtypeStruct((D_MODEL, POOL_WIDTH), BF16),
        compiler_params=pltpu.CompilerParams(vmem_limit_bytes=VMEM_LIMIT),
        name="pool_fold",
    )(pool_w_in, g_col, pool_w_group)


def _window_sums(a):
    s2 = a + pltpu.roll(a, 1, 0)
    s4 = s2 + pltpu.roll(s2, 2, 0)
    s8 = s4 + pltpu.roll(s4, 4, 0)
    s16 = s8 + pltpu.roll(s8, 8, 0)
    return {2: s2, 4: s4, 8: s8, 16: s16}


def _pooled_mix(ubuf, row0, n, pos0):
    pos = pos0 + jax.lax.broadcasted_iota(jnp.int32, (n, 1), 0)
    inv_cnt = {w: 1.0 / jnp.minimum(pos + 1, w).astype(F32) for w in POOL_WINDOWS}
    lane = jax.lax.broadcasted_iota(jnp.int32, (n, LANES), 1)
    mixed = []
    for c in range(POOL_WIDTH // LANES):
        cols = slice(c * LANES, (c + 1) * LANES)
        sums = _window_sums(ubuf[row0:row0 + POOL_HALO + n, cols])
        lo_grp = (c * LANES) // POOL_GROUP_DIM
        hi_grp = ((c + 1) * LANES - 1) // POOL_GROUP_DIM
        w_lo, w_hi = POOL_WINDOWS[lo_grp], POOL_WINDOWS[hi_grp]
        pooled = sums[w_lo][POOL_HALO:] * inv_cnt[w_lo]
        if hi_grp != lo_grp:
            split = hi_grp * POOL_GROUP_DIM - c * LANES
            pooled = jnp.where(lane < split, pooled, sums[w_hi][POOL_HALO:] * inv_cnt[w_hi])
        mixed.append(pooled - ubuf[POOL_HALO + row0:POOL_HALO + row0 + n, cols])
    return jnp.concatenate(mixed, axis=-1)


def _pool_layer_kernel(x_ref, w_u_ref, w_mq_ref, w_gate_ref, scale_ref,
                       kt_ref, v_ref, w_out_ref, o_ref, ubuf):
    j = pl.program_id(1)
    t = x_ref.shape[1]
    n = ROW_CHUNK
    chunks = range(t // n)

    @pl.when(j == 0)
    def _():
        ubuf[0:POOL_HALO, :] = jnp.zeros((POOL_HALO, POOL_WIDTH), F32)

    @pl.when(j > 0)
    def _():
        ubuf[0:POOL_HALO, :] = ubuf[t:t + POOL_HALO, :]

    xs = [x_ref[0, c * n:(c + 1) * n, :] for c in chunks]
    hns = [_rms_scale(x).astype(BF16) for x in xs]
    for c in chunks:
        ubuf[POOL_HALO + c * n:POOL_HALO + (c + 1) * n, :] = _dot(hns[c], w_u_ref[...])
    mqs = [_dot(hn, w_mq_ref[...]) for hn in hns]
    gates = [_silu(_dot(hn, w_gate_ref[...])) for hn in hns]
    toks = [_pooled_mix(ubuf, c * n, n, j * t + c * n) * scale_ref[...] for c in chunks]
    mem_os = [_memory_attention_heads(mq, kt_ref.at[0], v_ref.at[0]) for mq in mqs]
    for c in chunks:
        y = _dot((toks[c] * gates[c][:, :POOL_WIDTH]).astype(BF16), w_out_ref[0, :POOL_WIDTH, :])
        y += _dot((mem_os[c] * gates[c][:, POOL_WIDTH:]).astype(BF16),
                  w_out_ref[0, POOL_WIDTH:, :])
        o_ref[0, c * n:(c + 1) * n, :] = xs[c] + y


def _pool_layer(x, w_u, w_in, scale, k_t, v, w_out):
    b, s, _ = x.shape
    t = SEQ_TILE
    const = lambda shape: pl.BlockSpec(shape, lambda i, j: (0,) * len(shape),
                                       pipeline_mode=pl.Buffered(1))
    return pl.pallas_call(
        _pool_layer_kernel,
        grid=(b, s // t),
        in_specs=[
            pl.BlockSpec((1, t, D_MODEL), lambda i, j: (i, j, 0)),
            const(w_u.shape),
            pl.BlockSpec((D_MODEL, MEM_WIDTH), lambda i, j: (0, POOL_WIDTH // MEM_WIDTH)),
            pl.BlockSpec((D_MODEL, MIX_WIDTH), lambda i, j: (0, 1)),
            const(scale.shape),
            pl.BlockSpec((1,) + k_t.shape[1:], lambda i, j: (i, 0, 0, 0)),
            pl.BlockSpec((1,) + v.shape[1:], lambda i, j: (i, 0, 0, 0)),
            pl.BlockSpec((1,) + w_out.shape[1:], lambda i, j: (0, 0, 0)),
        ],
        out_specs=pl.BlockSpec((1, t, D_MODEL), lambda i, j: (i, j, 0)),
        out_shape=jax.ShapeDtypeStruct(x.shape, F32),
        scratch_shapes=[pltpu.VMEM((POOL_HALO + t, POOL_WIDTH), F32)],
        compiler_params=pltpu.CompilerParams(
            dimension_semantics=("parallel", "arbitrary"), vmem_limit_bytes=VMEM_LIMIT),
        name="pool_layer",
    )(x, w_u, w_in, w_in, scale, k_t, v, w_out)


def _mla_prep_kernel(w_t_ref, g_ref, w_pm_ref, w_fm_ref):
    o_ckv = Q_LORA
    o_kr = o_ckv + KV_LORA
    o_mq = o_kr + QK_ROPE
    o_gate = o_mq + MEM_WIDTH
    g = g_ref[...]
    w_fm_ref[0:Q_LORA, :] = (w_t_ref[0:o_ckv, :] * g).astype(BF16)
    w_fm_ref[Q_LORA:, :] = (w_t_ref[o_kr:o_mq, :] * g).astype(BF16)
    blocks = ([(o_gate + r, 1.0) for r in range(0, MIX_WIDTH, LANES)]
              + [(o_ckv + r, 1.0) for r in range(0, KV_LORA, LANES)]
              + [(o_mq + r, MEM_Q_SCALE) for r in range(0, MEM_WIDTH, LANES)])
    for dst, (src, scale) in enumerate(blocks):
        blk = w_t_ref[src:src + LANES, :] * (g * scale)
        w_pm_ref[:, dst * LANES:(dst + 1) * LANES] = blk.T.astype(BF16)


def _mla_prep(w_t, g_row):
    n_in, d = w_t.shape
    n_pm = MIX_WIDTH + KV_LORA + MEM_WIDTH
    n_fm = Q_LORA + QK_ROPE
    return pl.pallas_call(
        _mla_prep_kernel,
        grid=(1,),
        in_specs=[pl.BlockSpec((n_in, d), lambda i: (0, 0)),
                  pl.BlockSpec(g_row.shape, lambda i: (0, 0))],
        out_specs=[pl.BlockSpec((d, n_pm), lambda i: (0, 0)),
                   pl.BlockSpec((n_fm, d), lambda i: (0, 0))],
        out_shape=[jax.ShapeDtypeStruct((d, n_pm), BF16), jax.ShapeDtypeStruct((n_fm, d), BF16)],
        compiler_params=pltpu.CompilerParams(vmem_limit_bytes=VMEM_LIMIT),
        name="mla_prep",
    )(w_t, g_row)


def _rope_t(x, cos_t, sin_t):
    half = QK_ROPE // 2
    x1, x2 = x[:half], x[half:]
    return x1 * cos_t - x2 * sin_t, x2 * cos_t + x1 * sin_t


def _mla_pre_kernel(h_ref, pos_ref, inv_freq_ref, w_fm_ref, w_pm_ref, w_uq_t_ref, w_uk_ref,
                    w_uv_t_ref, kbd_ref, mem_v_ref, w_out_ref,
                    q_ref, k_ref, v_ref, sg_ref, hp_ref):
    w_cq_t_ref, w_kr_t_ref = w_fm_ref.at[0:Q_LORA], w_fm_ref.at[Q_LORA:Q_LORA + QK_ROPE]
    w_gate_ref = w_pm_ref.at[:, 0:MIX_WIDTH]
    w_ckv_ref = w_pm_ref.at[:, MIX_WIDTH:MIX_WIDTH + KV_LORA]
    w_mq_ref = w_pm_ref.at[:, MIX_WIDTH + KV_LORA:MIX_WIDTH + KV_LORA + MEM_WIDTH]
    w_out_mem_ref = w_out_ref.at[0]
    t = h_ref.shape[1]
    n = 2 * ROW_CHUNK
    chunks = range(t // n)
    heads = range(MLA_HEADS)
    half = QK_ROPE // 2
    rows = [slice(c * n, (c + 1) * n) for c in chunks]

    hs = [h_ref[0, rows[c], :] for c in chunks]
    xns = [_rms_scale(h) for h in hs]
    hns = [xn.astype(BF16) for xn in xns]
    mqs = [_dot(hn, w_mq_ref[...]) for hn in hns]
    mem_os = [_memory_attention(mq, kbd_ref[0], mem_v_ref.at[0]) for mq in mqs]
    gates = [_silu(_dot(hn, w_gate_ref[...])) for hn in hns]
    for c in chunks:
        sg_ref[0, rows[c], :] = gates[c][:, :MLA_WIDTH]
        hp_ref[0, rows[c], :] = hs[c] + _dot(
            (mem_os[c] * gates[c][:, MLA_WIDTH:]).astype(BF16), w_out_mem_ref[...])

    hn_ts = [xn.T.astype(BF16) for xn in xns]
    cq_ts = [_dot(w_cq_t_ref[...], hn_t) for hn_t in hn_ts]
    kr_ts = [_dot(w_kr_t_ref[...], hn_t) for hn_t in hn_ts]
    ckvns = [_rms_scale(_dot(hn, w_ckv_ref[...])).astype(BF16) for hn in hns]
    angs = [pos_ref[0, :, rows[c]].astype(F32) * inv_freq_ref[...] for c in chunks]
    cos_ts, sin_ts = [jnp.cos(a) for a in angs], [jnp.sin(a) for a in angs]
    cqn_ts = [(cq * jax.lax.rsqrt(jnp.mean(cq * cq, axis=0, keepdims=True) + EPS)
               ).astype(BF16) for cq in cq_ts]

    q_ts = [_dot(w_uq_t_ref[...], cqn_t) for cqn_t in cqn_ts]
    for c in chunks:
        for hd in heads:
            base = hd * QK_DIM
            r1, r2 = _rope_t(q_ts[c][base + QK_NOPE:base + QK_DIM], cos_ts[c], sin_ts[c])
            q_ref[0, hd, 0:QK_NOPE, rows[c]] = q_ts[c][base:base + QK_NOPE].astype(BF16)
            q_ref[0, hd, QK_NOPE:QK_NOPE + half, rows[c]] = r1.astype(BF16)
            q_ref[0, hd, QK_NOPE + half:QK_DIM, rows[c]] = r2.astype(BF16)

    k_nopes = [_dot(ckvn, w_uk_ref[...]) for ckvn in ckvns]
    v_ts = [_dot_nt(w_uv_t_ref[...], ckvn) for ckvn in ckvns]
    for c in chunks:
        r1, r2 = _rope_t(kr_ts[c], cos_ts[c], sin_ts[c])
        k_rope = jnp.concatenate([r1, r2, jnp.zeros((LANES - QK_ROPE, n), F32)], axis=0).T
        k_rope = k_rope[:, :QK_ROPE]
        kt, off = divmod(c * n, ATTN_TILE)
        for hd in heads:
            k_ref[0, hd, rows[c], :] = jnp.concatenate(
                [k_nopes[c][:, hd * QK_NOPE:(hd + 1) * QK_NOPE], k_rope], axis=-1).astype(BF16)
            v_ref[0, hd, kt, 0:V_HEAD, off:off + n] = (
                v_ts[c][hd * V_HEAD:(hd + 1) * V_HEAD].astype(BF16))
            v_ref[0, hd, kt, V_HEAD:V_EXT, off:off + n] = jnp.ones((V_EXT - V_HEAD, n), BF16)


def _mla_pre(h, pos, inv_freq, w_fm, w_pm, w_uq_t, w_uk, w_uv_t, kbd, mem_v, w_out):
    b, s, _ = h.shape
    t = SEQ_TILE
    n_kt = t // ATTN_TILE
    const = lambda shape: pl.BlockSpec(shape, lambda i, j: (0,) * len(shape),
                                       pipeline_mode=pl.Buffered(1))
    row = lambda width: pl.BlockSpec((1, t, width), lambda i, j: (i, j, 0))
    return pl.pallas_call(
        _mla_pre_kernel,
        grid=(b, s // t),
        in_specs=[
            row(D_MODEL),
            pl.BlockSpec((1, 1, t), lambda i, j: (i, 0, j)), const(inv_freq.shape),
            const(w_fm.shape), const(w_pm.shape),
            const(w_uq_t.shape), const(w_uk.shape), const(w_uv_t.shape),
            pl.BlockSpec((1,) + kbd.shape[1:], lambda i, j: (i, 0, 0)),
            pl.BlockSpec((1,) + mem_v.shape[1:], lambda i, j: (i, 0, 0, 0)),
            pl.BlockSpec((1, MEM_WIDTH, D_MODEL), lambda i, j: (1, MLA_WIDTH // MEM_WIDTH, 0)),
        ],
        out_specs=[
            pl.BlockSpec((1, MLA_HEADS, QK_DIM, t), lambda i, j: (i, 0, 0, j)),
            pl.BlockSpec((1, MLA_HEADS, t, QK_DIM), lambda i, j: (i, 0, j, 0)),
            pl.BlockSpec((1, MLA_HEADS, n_kt, V_EXT, ATTN_TILE), lambda i, j: (i, 0, j, 0, 0)),
            row(MLA_WIDTH), row(D_MODEL),
        ],
        out_shape=[
            jax.ShapeDtypeStruct((b, MLA_HEADS, QK_DIM, s), BF16),
            jax.ShapeDtypeStruct((b, MLA_HEADS, s, QK_DIM), BF16),
            jax.ShapeDtypeStruct((b, MLA_HEADS, s // ATTN_TILE, V_EXT, ATTN_TILE), BF16),
            jax.ShapeDtypeStruct((b, s, MLA_WIDTH), F32),
            jax.ShapeDtypeStruct((b, s, D_MODEL), F32),
        ],
        compiler_params=pltpu.CompilerParams(
            dimension_semantics=("parallel", "parallel"), vmem_limit_bytes=VMEM_LIMIT),
        name="mla_pre",
    )(h, pos, inv_freq, w_fm, w_pm, w_uq_t, w_uk, w_uv_t, kbd, mem_v, w_out)


def _sublane_allreduce(x, op):
    for shift in (4, 2, 1):
        x = op(x, pltpu.roll(x, shift, 0))
    return x


def _mla_attn_kernel(q_ref, k_ref, v_ref, sg_ref, hp_ref, w_out_ref, g_ref, o_ref,
                     acc_sc, tok_sc):
    i = pl.program_id(1)
    tq = q_ref.shape[3]
    tk = v_ref.shape[4]
    half = tk // 2
    heads = range(MLA_HEADS)

    def step(jt, carry, k0, kn, q0, qn, masked):
        m_old = carry
        ks = slice(jt * tk + k0, jt * tk + k0 + kn)
        qs = slice(q0, q0 + qn)
        s = [_dot(k_ref[0, hd, ks, :], q_ref[0, hd, :, qs]) for hd in heads]
        if masked:
            k_chunk = (k0 + jax.lax.broadcasted_iota(jnp.int32, (kn, qn), 0)) // CHUNK
            q_chunk = (q0 + jax.lax.broadcasted_iota(jnp.int32, (kn, qn), 1)) // CHUNK
            mask = k_chunk <= q_chunk
        m_new, alpha, p = [], [], []
        for hd in heads:
            sh = jnp.where(mask, s[hd], NEG_INF) if masked else s[hd]
            s3 = sh.reshape(kn // SUBLANES, SUBLANES, qn)
            m_prev = m_old[hd][:, qs]
            m = jnp.maximum(m_prev, _sublane_allreduce(jnp.max(s3, axis=0), jnp.maximum))
            alpha.append(jnp.exp2(m_prev - m))
            p.append(jnp.exp2(s3 - m[None]).reshape(kn, qn).astype(BF16))
            if qn != tq:
                m = jnp.concatenate(
                    ([m_old[hd][:, :q0]] if q0 else []) + [m]
                    + ([m_old[hd][:, q0 + qn:]] if q0 + qn < tq else []), axis=1)
            m_new.append(m)
        pv = [_dot(v_ref[0, hd, jt, :, k0:k0 + kn], p[hd]) for hd in heads]
        for hd in heads:
            acc = acc_sc[hd, :, qs].reshape(V_EXT // SUBLANES, SUBLANES, qn)
            acc_sc[hd, :, qs] = (alpha[hd][None] * acc).reshape(V_EXT, qn) + pv[hd]
        return tuple(m_new)

    def query_tile(n_full):
        acc_sc[...] = jnp.zeros(acc_sc.shape, F32)
        carry = tuple(jnp.full((SUBLANES, tq), NEG_INF, F32) for _ in heads)
        for jt in range(n_full):
            carry = step(jt, carry, 0, tk, 0, tq, False)
        carry = step(n_full, carry, 0, half, 0, tq, True)
        step(n_full, carry, half, tk - half, half, tq - half, True)

        for hd in heads:
            l_tot = acc_sc[hd, V_HEAD:V_HEAD + SUBLANES, :]
            o_t = acc_sc[hd, 0:V_HEAD, :].reshape(V_HEAD // SUBLANES, SUBLANES, tq) / l_tot[None]
            tok_sc[:, hd * V_HEAD:(hd + 1) * V_HEAD] = o_t.reshape(V_HEAD, tq).T

        n = tq // 2
        rows = [slice(c * n, (c + 1) * n) for c in range(tq // n)]
        branches = [(tok_sc[r, :] * sg_ref[0, r, :]).astype(BF16) for r in rows]
        outs = [hp_ref[0, r, :] + _dot(br, w_out_ref[0]) for r, br in zip(rows, branches)]
        for r, out in zip(rows, outs):
            o_ref[0, r, :] = _rms_norm(out, g_ref[...])

    for n_full in range(k_ref.shape[2] // tk):
        pl.when(i == n_full)(functools.partial(query_tile, n_full))


def _mla_attn(q_t, k, v_t, sg, hp, w_out, final_g):
    b, _, s, _ = k.shape
    t = ATTN_TILE
    const = lambda shape: pl.BlockSpec(shape, lambda i, j: (0,) * len(shape),
                                       pipeline_mode=pl.Buffered(1))
    row = lambda width: pl.BlockSpec((1, t, width), lambda i, j: (i, j, 0))
    return pl.pallas_call(
        _mla_attn_kernel,
        grid=(b, s // t),
        in_specs=[
            pl.BlockSpec((1, MLA_HEADS, QK_DIM, t), lambda i, j: (i, 0, 0, j)),
            pl.BlockSpec((1, MLA_HEADS, s, QK_DIM), lambda i, j: (i, 0, 0, 0)),
            pl.BlockSpec((1,) + v_t.shape[1:], lambda i, j: (i, 0, 0, 0, 0)),
            row(MLA_WIDTH), row(D_MODEL),
            pl.BlockSpec((1, MLA_WIDTH, D_MODEL), lambda i, j: (1, 0, 0)), const(final_g.shape),
        ],
        out_specs=row(D_MODEL),
        out_shape=jax.ShapeDtypeStruct((b, s, D_MODEL), F32),
        scratch_shapes=[
            pltpu.VMEM((MLA_HEADS, V_EXT, t), F32),
            pltpu.VMEM((t, MLA_WIDTH), F32),
        ],
        compiler_params=pltpu.CompilerParams(
            dimension_semantics=("parallel", "arbitrary"), vmem_limit_bytes=VMEM_LIMIT),
        name="mla_attn",
    )(q_t, k, v_t, sg, hp, w_out, final_g)


def kernel(x, mem, positions, norm_g, mem_norm_g, w_mem_kv, w_out, pool_w_in, pool_w_group,
           pool_scale, mla_w_in, mla_q_norm_g, mla_w_uq, mla_kv_norm_g, mla_w_ukv, final_norm_g):
    depth = w_mem_kv.shape[0]
    assert depth == 2 and pool_w_in.shape[0] == 1 and mla_w_in.shape[0] == 1
    bf = lambda a: a.astype(BF16)
    row = lambda a: a.reshape(1, -1)

    inv_freq = ROPE_THETA ** (-(jnp.arange(0, QK_ROPE, 2, dtype=F32) / QK_ROPE))
    kt0, v0, kbd1, v1 = _mem_kv(mem, row(mem_norm_g), bf(w_mem_kv))
    w_out_bf = bf(w_out)

    g0 = norm_g[0][:, None]
    mq_cols = (jnp.arange(POOL_IN) >= POOL_WIDTH) & (jnp.arange(POOL_IN) < POOL_WIDTH + MEM_WIDTH)
    pool_w = bf(pool_w_in[0] * g0 * jnp.where(mq_cols, MEM_Q_SCALE, 1.0))
    h1 = _pool_layer(x, _pool_fold(pool_w_in, g0, pool_w_group), pool_w, row(pool_scale[0]),
                     kt0, v0, w_out_bf)

    w_pm, w_fm = _mla_prep(mla_w_in[0].T, row(norm_g[1]))
    w_ukv = (mla_w_ukv[0] * mla_kv_norm_g[0][:, None]).reshape(
        KV_LORA, MLA_HEADS, QK_NOPE + V_HEAD)
    w_uk = bf(w_ukv[:, :, :QK_NOPE].reshape(KV_LORA, MLA_HEADS * QK_NOPE))
    w_uv_t = bf(w_ukv[:, :, QK_NOPE:].reshape(KV_LORA, MLA_WIDTH).T)
    w_uq_t = bf((mla_w_uq[0] * (mla_q_norm_g[0][:, None] * Q_SCALE)).T)
    q_t, k, v_t, sg, hp = _mla_pre(
        h1, positions[:, None, :], inv_freq[:, None], w_fm, w_pm, w_uq_t, w_uk, w_uv_t,
        kbd1, v1, w_out_bf)

    return _mla_attn(q_t, k, v_t, sg, hp, w_out_bf, row(final_norm_g))
```
